```python
import math
import jax, jax.numpy as jnp
from jax import lax
import numpy as np

D_MODEL = 1024
BATCH = 4
SEQ = 4096
DEPTH = 2

HEAD_DIM = 64
BLOCK = 128
A_HEADS = 8
A_PATTERNS = ((128, 1), (512, 4), (2048, 16))
A_WIDTH = A_HEADS * HEAD_DIM
B_HEADS = 4
B_QK_DIM = HEAD_DIM
B_V_DIM = 2 * HEAD_DIM
B_WIDTH = B_HEADS * B_V_DIM
C_HEADS = 8
C_Q_RANK = 768
C_KV_RANK = 256
C_NOPE = 64
C_ROPE = 32
C_V = 64
C_WIDTH = C_HEADS * C_V
ROPE_THETA = 10000.0
REL_BUCKETS = 32
REL_MAX_DIST = 2048
IN_SIZES = (A_WIDTH, A_WIDTH, A_WIDTH,
            2 * B_HEADS * B_QK_DIM, 2 * B_HEADS * B_QK_DIM, B_WIDTH,
            C_Q_RANK, C_KV_RANK, C_ROPE)
D_IN = 3 * A_WIDTH + 4 * B_HEADS * B_QK_DIM + B_WIDTH + C_Q_RANK + C_KV_RANK + C_ROPE
N_BRANCH = 3
FFN_HIDDEN = -(-8 * D_MODEL // (3 * 256)) * 256
EPS = 1e-6
NEG = -1e30

kernel_name = 'hybrid_gated_dilated_diff_mla_block'


def rmsnorm(x, g):
    xf = x.astype(jnp.float32)
    y = xf * lax.rsqrt(jnp.mean(xf * xf, axis=-1, keepdims=True) + EPS)
    return (y * g.astype(jnp.float32)).astype(x.dtype)


def t5_bucket(dist):
    dist = jnp.maximum(dist, 0)
    exact = REL_BUCKETS // 2
    log_ratio = jnp.log(jnp.maximum(dist, 1).astype(jnp.float32) / exact) / math.log(REL_MAX_DIST / exact)
    large = jnp.minimum(exact + (log_ratio * (REL_BUCKETS - exact)).astype(jnp.int32), REL_BUCKETS - 1)
    return jnp.where(dist < exact, dist, large)


def rope(x, cos, sin):
    half = x.shape[-1] // 2
    xf = x.astype(jnp.float32)
    x1, x2 = xf[..., :half], xf[..., half:]
    return jnp.concatenate([x1 * cos - x2 * sin, x2 * cos + x1 * sin], axis=-1).astype(x.dtype)


def to_query_blocks(t):
    b, s = t.shape[:2]
    return t.reshape(b, s // BLOCK, BLOCK, *t.shape[2:]).swapaxes(0, 1)


def from_query_blocks(o):
    nq, b = o.shape[:2]
    return o.swapaxes(0, 1).reshape(b, nq * BLOCK, *o.shape[3:])


def dilated_window_attention(q, k, v, bias_table):
    b, s, h, hd = q.shape
    scale = hd ** -0.5
    outs, lses = [], []
    for window, dil in A_PATTERNS:
        n_back = window // dil
        sub_len = s // dil
        nb = -(-sub_len // BLOCK)
        lp = nb * BLOCK

        def to_sub(t):
            t = t.reshape(b, sub_len, dil, h, hd).transpose(0, 2, 1, 3, 4)
            t = jnp.pad(t, ((0, 0), (0, 0), (0, lp - sub_len), (0, 0), (0, 0)))
            return t.reshape(b, dil, nb, BLOCK, h, hd)

        def window_keys(t):
            prev = jnp.pad(t, ((0, 0), (0, 0), (1, 0), (0, 0), (0, 0), (0, 0)))[:, :, :-1]
            return jnp.concatenate([prev, t], axis=3)

        qb = to_sub(q)
        kw = window_keys(to_sub(k))
        vw = window_keys(to_sub(v))
        qi = jnp.arange(BLOCK)[:, None]
        kj = jnp.arange(2 * BLOCK)[None, :]
        step = qi + BLOCK - kj
        in_band = (step >= 0) & (step <= n_back)
        blk = jnp.arange(nb)[:, None, None]
        valid = in_band[None] & ((blk > 0) | (kj[None] >= BLOCK))
        bias = bias_table[t5_bucket(step * dil)].transpose(2, 0, 1)
        sc = jnp.einsum('brnqhc,brnkhc->brnhqk', qb, kw).astype(jnp.float32) * scale + bias
        sc = jnp.where(valid[:, None], sc, NEG)
        lse = jax.nn.logsumexp(sc, axis=-1)
        p = jnp.exp(sc - lse[..., None])
        o = jnp.einsum('brnhqk,brnkhc->brnqhc', p.astype(v.dtype), vw)
        o = o.reshape(b, dil, lp, h, hd)[:, :, :sub_len].transpose(0, 2, 1, 3, 4).reshape(b, s, h, hd)
        lse = lse.transpose(0, 1, 2, 4, 3).reshape(b, dil, lp, h)[:, :, :sub_len]
        lse = lse.transpose(0, 2, 1, 3).reshape(b, s, h)
        outs.append(o)
        lses.append(lse)
    w = jax.nn.softmax(jnp.stack(lses, axis=0), axis=0)
    out = jnp.einsum('gbsh,gbshc->bshc', w, jnp.stack(outs, axis=0).astype(jnp.float32))
    return out.astype(q.dtype).reshape(b, s, h * hd)


def differential_attention(q1, q2, k1, k2, v, lam, bias_table):
    s = q1.shape[1]
    scale = q1.shape[-1] ** -0.5
    kpos = jnp.arange(s)

    def block(args):
        i, qa, qb = args
        qpos = i * BLOCK + jnp.arange(BLOCK)
        dist = qpos[:, None] - kpos[None, :]
        causal = dist >= 0
        bias = bias_table[t5_bucket(dist)].transpose(2, 0, 1)

        def probs(qx, kx):
            sc = jnp.einsum('bqhc,bkhc->bhqk', qx, kx).astype(jnp.float32) * scale + bias
            return jax.nn.softmax(jnp.where(causal, sc, NEG), axis=-1)

        p = probs(qa, k1) - lam * probs(qb, k2)
        return jnp.einsum('bhqk,bkhc->bqhc', p.astype(v.dtype), v)

    o = lax.map(block, (jnp.arange(s // BLOCK), to_query_blocks(q1), to_query_blocks(q2)))
    return from_query_blocks(o)


def mla_attention(q_nope, q_rope, k_nope, k_rope, v):
    s = q_nope.shape[1]
    scale = (C_NOPE + C_ROPE) ** -0.5
    kpos = jnp.arange(s)

    def block(args):
        i, qn, qr = args
        qpos = i * BLOCK + jnp.arange(BLOCK)
        causal = qpos[:, None] >= kpos[None, :]
        sc = (jnp.einsum('bqhc,bkhc->bhqk', qn, k_nope)
              + jnp.einsum('bqhr,bkr->bhqk', qr, k_rope)).astype(jnp.float32) * scale
        p = jax.nn.softmax(jnp.where(causal, sc, NEG), axis=-1)
        return jnp.einsum('bhqk,bkhc->bqhc', p.astype(v.dtype), v)

    o = lax.map(block, (jnp.arange(s // BLOCK), to_query_blocks(q_nope), to_query_blocks(q_rope)))
    return from_query_blocks(o)


def setup_inputs(seed: int = 0) -> dict:
    key = jax.random.key(seed)
    ks = jax.random.split(key, 24)
    f32 = jnp.float32

    def nrm(k, shape, scale):
        return jax.random.normal(k, shape, f32) * scale

    def gain(k, shape):
        return 1.0 + 0.05 * jax.random.normal(k, shape, f32)

    return {
        'x': nrm(ks[0], (BATCH, SEQ, D_MODEL), 1.0),
        'rel_bias_table': nrm(ks[1], (REL_BUCKETS, A_HEADS + B_HEADS), 0.5),
        'ln_mix_g': gain(ks[2], (DEPTH, D_MODEL)),
        'w_in': nrm(ks[3], (DEPTH, D_MODEL, D_IN), D_MODEL ** -0.5),
        'lambda_q1': nrm(ks[4], (DEPTH, B_QK_DIM), 0.1),
        'lambda_k1': nrm(ks[5], (DEPTH, B_QK_DIM), 0.1),
        'lambda_q2': nrm(ks[6], (DEPTH, B_QK_DIM), 0.1),
        'lambda_k2': nrm(ks[7], (DEPTH, B_QK_DIM), 0.1),
        'diff_subln_g': gain(ks[8], (DEPTH, B_V_DIM)),
        'mla_q_norm_g': gain(ks[9], (DEPTH, C_Q_RANK)),
        'w_uq': nrm(ks[10], (DEPTH, C_Q_RANK, C_HEADS * (C_NOPE + C_ROPE)), C_Q_RANK ** -0.5),
        'mla_kv_norm_g': gain(ks[11], (DEPTH, C_KV_RANK)),
        'w_ukv': nrm(ks[12], (DEPTH, C_KV_RANK, C_HEADS * (C_NOPE + C_V)), C_KV_RANK ** -0.5),
        'w_gate': nrm(ks[13], (DEPTH, D_MODEL, N_BRANCH * D_MODEL), D_MODEL ** -0.5),
        'b_gate': nrm(ks[14], (DEPTH, N_BRANCH * D_MODEL), 0.1),
        'w_br_a': nrm(ks[15], (DEPTH, A_WIDTH, D_MODEL), A_WIDTH ** -0.5),
        'w_br_b': nrm(ks[16], (DEPTH, B_WIDTH, D_MODEL), B_WIDTH ** -0.5),
        'w_br_c': nrm(ks[17], (DEPTH, C_WIDTH, D_MODEL), C_WIDTH ** -0.5),
        'w_o': nrm(ks[18], (DEPTH, D_MODEL, D_MODEL), D_MODEL ** -0.5),
        'ln_ffn_g': gain(ks[19], (DEPTH, D_MODEL)),
        'w_ffn_gate': nrm(ks[20], (DEPTH, D_MODEL, FFN_HIDDEN), D_MODEL ** -0.5),
        'w_ffn_up': nrm(ks[21], (DEPTH, D_MODEL, FFN_HIDDEN), D_MODEL ** -0.5),
        'w_ffn_down': nrm(ks[22], (DEPTH, FFN_HIDDEN, D_MODEL), FFN_HIDDEN ** -0.5),
        'final_norm_g': gain(ks[23], (D_MODEL,)),
    }


def reference(x, rel_bias_table, ln_mix_g, w_in, lambda_q1, lambda_k1, lambda_q2, lambda_k2,
              diff_subln_g, mla_q_norm_g, w_uq, mla_kv_norm_g, w_ukv, w_gate, b_gate,
              w_br_a, w_br_b, w_br_c, w_o, ln_ffn_g, w_ffn_gate, w_ffn_up, w_ffn_down,
              final_norm_g):
    b, s, _ = x.shape
    f32 = jnp.float32
    split_points = [int(c) for c in np.cumsum(IN_SIZES)[:-1]]
    pos = jnp.arange(s, dtype=f32)
    inv_freq = ROPE_THETA ** (-jnp.arange(0, C_ROPE, 2, dtype=f32) / C_ROPE)
    ang = pos[:, None] * inv_freq[None, :]
    cos, sin = jnp.cos(ang), jnp.sin(ang)
    bias_a = rel_bias_table[:, :A_HEADS]
    bias_b = rel_bias_table[:, A_HEADS:]

    for l in range(DEPTH):
        h = rmsnorm(x, ln_mix_g[l])
        proj = h @ w_in[l]
        qa, ka, va, qb, kb, vb, cq, ckv, kr = jnp.split(proj, split_points, axis=-1)

        ya = dilated_window_attention(qa.reshape(b, s, A_HEADS, HEAD_DIM),
                                      ka.reshape(b, s, A_HEADS, HEAD_DIM),
                                      va.reshape(b, s, A_HEADS, HEAD_DIM), bias_a)

        lam_init = 0.8 - 0.6 * math.exp(-0.3 * l)
        lam = (jnp.exp(jnp.sum(lambda_q1[l].astype(f32) * lambda_k1[l].astype(f32)))
               - jnp.exp(jnp.sum(lambda_q2[l].astype(f32) * lambda_k2[l].astype(f32))) + lam_init)
        q_pair = qb.reshape(b, s, B_HEADS, 2, B_QK_DIM)
        k_pair = kb.reshape(b, s, B_HEADS, 2, B_QK_DIM)
        ob = differential_attention(q_pair[..., 0, :], q_pair[..., 1, :],
                                    k_pair[..., 0, :], k_pair[..., 1, :],
                                    vb.reshape(b, s, B_HEADS, B_V_DIM), lam, bias_b)
        yb = (rmsnorm(ob, diff_subln_g[l]) * (1.0 - lam_init)).reshape(b, s, B_WIDTH)

        qc = (rmsnorm(cq, mla_q_norm_g[l]) @ w_uq[l]).reshape(b, s, C_HEADS, C_NOPE + C_ROPE)
        q_nope = qc[..., :C_NOPE]
        q_rope = rope(qc[..., C_NOPE:], cos[:, None, :], sin[:, None, :])
        kvc = (rmsnorm(ckv, mla_kv_norm_g[l]) @ w_ukv[l]).reshape(b, s, C_HEADS, C_NOPE + C_V)
        k_nope, vc = kvc[..., :C_NOPE], kvc[..., C_NOPE:]
        k_rope = rope(kr, cos, sin)
        yc = mla_attention(q_nope, q_rope, k_nope, k_rope, vc).reshape(b, s, C_WIDTH)

        gates = jax.nn.sigmoid((h @ w_gate[l] + b_gate[l]).astype(f32)).astype(x.dtype)
        gates = gates.reshape(b, s, N_BRANCH, D_MODEL)
        merged = (gates[:, :, 0] * (ya @ w_br_a[l]) + gates[:, :, 1] * (yb @ w_br_b[l])
                  + gates[:, :, 2] * (yc @ w_br_c[l]))
        x = x + merged @ w_o[l]

        h = rmsnorm(x, ln_ffn_g[l])
        x = x + (jax.nn.silu(h @ w_ffn_gate[l]) * (h @ w_ffn_up[l])) @ w_ffn_down[l]

    return rmsnorm(x, final_norm_g)
```

```python
import functools
import math

import numpy as np
import jax
import jax.numpy as jnp
from jax import lax
from jax.experimental import pallas as pl
from jax.experimental.pallas import tpu as pltpu

D_MODEL = 1024
HEAD_DIM = 64
A_HEADS = 8
A_PATTERNS = ((128, 1), (512, 4), (2048, 16))
A_WIDTH = A_HEADS * HEAD_DIM
A_BLOCK = 128
B_HEADS = 4
B_QK_DIM = HEAD_DIM
B_V_DIM = 2 * HEAD_DIM
B_WIDTH = B_HEADS * B_V_DIM
C_HEADS = 8
C_PAIRS = C_HEADS // 2
C_Q_RANK = 768
C_KV_RANK = 256
C_NOPE = 64
C_ROPE = 32
C_V = 64
C_WIDTH = C_HEADS * C_V
C_PAIR_LANES = 256
ROPE_THETA = 10000.0
REL_BUCKETS = 32
REL_MAX_DIST = 2048
AB_COLS = 3 * A_WIDTH + 4 * B_HEADS * B_QK_DIM + B_WIDTH
N_BRANCH = 3
FFN_HIDDEN = 2816
EPS = 1e-6
NEG = -1e30

ROW_TILE = 256
ATT_TILE = 256
B_BIAS_TILES = 8
FFN_CHUNK = 256
VMEM_LIMIT = 56 * 1024 * 1024

F32 = jnp.float32
BF16 = jnp.bfloat16


def _dot(a, b):
    return jnp.dot(a, b, preferred_element_type=F32)


def _dot_nt(a, b):
    return lax.dot_general(a, b, (((1,), (1,)), ((), ())), preferred_element_type=F32)


def _rms(x, g):
    return x * lax.rsqrt(jnp.mean(x * x, axis=-1, keepdims=True) + EPS) * g


def _swap_halves(t):
    half = t.shape[1] // 2
    return jnp.concatenate([t[:, half:], t[:, :half]], axis=1)


def _const_spec(shape):
    nd = len(shape)
    return pl.BlockSpec(shape, lambda *_: (0,) * nd, pipeline_mode=pl.Buffered(1))


def _params(n_axes):
    return pltpu.CompilerParams(dimension_semantics=("parallel",) * n_axes,
                                vmem_limit_bytes=VMEM_LIMIT)


def _t5_bucket_np(dist):
    dist = np.maximum(dist, 0)
    exact = REL_BUCKETS // 2
    ratio = np.maximum(dist, 1).astype(np.float32) / np.float32(exact)
    log_ratio = np.log(ratio).astype(np.float32) / np.float32(math.log(REL_MAX_DIST / exact))
    large = np.minimum(exact + (log_ratio * np.float32(REL_BUCKETS - exact)).astype(np.int32),
                       REL_BUCKETS - 1)
    return np.where(dist < exact, dist, large).astype(np.int32)


def _a_bucket_index(dil):
    qi = np.arange(A_BLOCK)[:, None]
    kj = np.arange(2 * A_BLOCK)[None, :]
    return _t5_bucket_np((qi + A_BLOCK - kj) * dil)


def _b_bucket_index():
    qi = np.arange(ATT_TILE)[:, None]
    kj = np.arange(ATT_TILE)[None, :]
    d = np.arange(B_BIAS_TILES)[:, None, None]
    return _t5_bucket_np(d * ATT_TILE + qi - kj)


def _c_layout():
    hq = C_NOPE + C_ROPE
    q_zero = C_HEADS * hq
    kv_zero = C_HEADS * (C_NOPE + C_V)
    half = C_ROPE // 2
    q_idx = np.full((C_PAIRS, C_PAIR_LANES), q_zero, np.int32)
    k_idx = np.full((C_PAIRS, C_PAIR_LANES), kv_zero, np.int32)
    for j in range(C_PAIRS):
        a, b = 2 * j, 2 * j + 1
        q_idx[j, 0:64] = a * hq + np.arange(64)
        q_idx[j, 64:80] = a * hq + C_NOPE + np.arange(half)
        q_idx[j, 80:96] = b * hq + C_NOPE + np.arange(half)
        q_idx[j, 128:192] = b * hq + np.arange(64)
        q_idx[j, 192:208] = a * hq + C_NOPE + half + np.arange(half)
        q_idx[j, 208:224] = b * hq + C_NOPE + half + np.arange(half)
        k_idx[j, 0:64] = a * (C_NOPE + C_V) + np.arange(64)
        k_idx[j, 128:192] = b * (C_NOPE + C_V) + np.arange(64)
    v_idx = (np.arange(C_HEADS)[:, None] * (C_NOPE + C_V) + C_NOPE + np.arange(C_V)[None, :])
    kr_idx = np.full((C_PAIR_LANES,), C_ROPE, np.int32)
    kr_idx[64:80] = np.arange(half)
    kr_idx[80:96] = np.arange(half)
    kr_idx[192:208] = half + np.arange(half)
    kr_idx[208:224] = half + np.arange(half)
    rope_lane = np.full((C_PAIR_LANES,), -1, np.int32)
    for start in (64, 80, 192, 208):
        rope_lane[start:start + half] = np.arange(half)
    sin_sign = np.zeros((C_PAIR_LANES,), np.float32)
    sin_sign[64:96] = -1.0
    sin_sign[192:224] = 1.0
    head_a = np.zeros((C_PAIR_LANES,), bool)
    head_a[0:80] = True
    head_a[192:208] = True
    head_b = np.zeros((C_PAIR_LANES,), bool)
    head_b[80:96] = True
    head_b[128:192] = True
    head_b[208:224] = True
    return dict(q=q_idx.reshape(-1), k=k_idx.reshape(-1), v=v_idx.reshape(-1), kr=kr_idx,
                rope_lane=rope_lane, sin_sign=sin_sign, head_a=head_a, head_b=head_b)


_C_LAYOUT = _c_layout()


def _rope_tables(seq):
    pos = jnp.arange(seq, dtype=F32)
    inv_freq = ROPE_THETA ** (-jnp.arange(0, C_ROPE, 2, dtype=F32) / C_ROPE)
    ang = pos[:, None] * inv_freq[None, :]
    cos, sin = jnp.cos(ang), jnp.sin(ang)
    lane = _C_LAYOUT["rope_lane"]
    is_rope = jnp.asarray(lane >= 0)
    gather = np.maximum(lane, 0)
    cos_t = jnp.where(is_rope[None, :], cos[:, gather], 1.0)
    sin_t = sin[:, gather] * jnp.asarray(_C_LAYOUT["sin_sign"])[None, :]
    return cos_t, sin_t


def _input_kernel(x_ref, g_ref, wab_ref, wcq_ref, gq_ref, wuq_ref, wckv_ref, gkv_ref, wuk_ref,
                  wuv_ref, wkr_ref, cos_ref, sin_ref, qkv_ref, qc_ref, kc_ref, vc_ref):
    h = _rms(x_ref[...], g_ref[...]).astype(BF16)
    for c in range(0, AB_COLS, 512):
        qkv_ref[:, c:c + 512] = _dot(h, wab_ref[:, c:c + 512]).astype(BF16)
    cos = cos_ref[...]
    sin = sin_ref[...]
    cqn = _rms(_dot(h, wcq_ref[...]), gq_ref[...]).astype(BF16)
    for j in range(C_PAIRS):
        cols = slice(j * C_PAIR_LANES, (j + 1) * C_PAIR_LANES)
        t = _dot(cqn, wuq_ref[:, cols])
        qc_ref[:, cols] = (t * cos + _swap_halves(t) * sin).astype(BF16)
    ckvn = _rms(_dot(h, wckv_ref[...]), gkv_ref[...]).astype(BF16)
    kr = _dot(h, wkr_ref[...])
    kr_rot = kr * cos + _swap_halves(kr) * sin
    for j in range(C_PAIRS):
        cols = slice(j * C_PAIR_LANES, (j + 1) * C_PAIR_LANES)
        kc_ref[:, cols] = (_dot(ckvn, wuk_ref[:, cols]) + kr_rot).astype(BF16)
    vc_ref[...] = _dot(ckvn, wuv_ref[...]).astype(BF16)


def _input_stage(x2, g, wab, wcq, gq, wuq, wckv, gkv, wuk, wuv, wkr, cos_t, sin_t, seq):
    n = x2.shape[0]
    tm = ROW_TILE
    pos_blocks = seq // tm
    row = lambda i: (i, 0)
    qc_cols = C_PAIRS * C_PAIR_LANES
    return pl.pallas_call(
        _input_kernel,
        grid=(n // tm,),
        in_specs=[
            pl.BlockSpec((tm, D_MODEL), row),
            _const_spec((1, D_MODEL)),
            _const_spec((D_MODEL, AB_COLS)),
            _const_spec((D_MODEL, C_Q_RANK)),
            _const_spec((1, C_Q_RANK)),
            _const_spec((C_Q_RANK, qc_cols)),
            _const_spec((D_MODEL, C_KV_RANK)),
            _const_spec((1, C_KV_RANK)),
            _const_spec((C_KV_RANK, qc_cols)),
            _const_spec((C_KV_RANK, C_WIDTH)),
            _const_spec((D_MODEL, C_PAIR_LANES)),
            pl.BlockSpec((tm, C_PAIR_LANES), lambda i: (i % pos_blocks, 0)),
            pl.BlockSpec((tm, C_PAIR_LANES), lambda i: (i % pos_blocks, 0)),
        ],
        out_specs=[
            pl.BlockSpec((tm, AB_COLS), row),
            pl.BlockSpec((tm, qc_cols), row),
            pl.BlockSpec((tm, qc_cols), row),
            pl.BlockSpec((tm, C_WIDTH), row),
        ],
        out_shape=[
            jax.ShapeDtypeStruct((n, AB_COLS), BF16),
            jax.ShapeDtypeStruct((n, qc_cols), BF16),
            jax.ShapeDtypeStruct((n, qc_cols), BF16),
            jax.ShapeDtypeStruct((n, C_WIDTH), BF16),
        ],
        compiler_params=_params(1),
        name="input_stage",
    )(x2, g, wab, wcq, gq, wuq, wckv, gkv, wuk, wuv, wkr, cos_t, sin_t)


def _dilated_kernel(q_ref, kp_ref, kc_ref, vp_ref, vc_ref, bias_ref, o_ref, lse_ref):
    n = pl.program_id(2)
    qi = lax.broadcasted_iota(jnp.int32, (A_BLOCK, A_BLOCK), 0)
    kj = lax.broadcasted_iota(jnp.int32, (A_BLOCK, A_BLOCK), 1)
    mask_prev = (kj >= qi) & (n > 0)
    mask_cur = kj <= qi
    low = kj < HEAD_DIM
    lse_all = jnp.zeros((A_BLOCK, A_BLOCK), F32)
    for p in range(A_HEADS // 2):
        cols = slice(p * 128, (p + 1) * 128)
        q2 = q_ref[0, :, cols] * jnp.asarray(HEAD_DIM ** -0.5, BF16)
        kp, kc = kp_ref[0, :, cols], kc_ref[0, :, cols]
        vp, vc = vp_ref[0, :, cols], vc_ref[0, :, cols]
        o_pair = None
        for hh in range(2):
            head = 2 * p + hh
            qm = jnp.where(low if hh == 0 else ~low, q2, jnp.zeros_like(q2))
            sp = jnp.where(mask_prev, _dot_nt(qm, kp) + bias_ref[head, :, :A_BLOCK], NEG)
            sc = jnp.where(mask_cur, _dot_nt(qm, kc) + bias_ref[head, :, A_BLOCK:], NEG)
            m = jnp.maximum(jnp.max(sp, axis=-1, keepdims=True), jnp.max(sc, axis=-1, keepdims=True))
            pp = jnp.exp(sp - m)
            pc = jnp.exp(sc - m)
            l = jnp.sum(pp, axis=-1, keepdims=True) + jnp.sum(pc, axis=-1, keepdims=True)
            o = (_dot(pp.astype(BF16), vp) + _dot(pc.astype(BF16), vc)) / l
            o_pair = o if hh == 0 else jnp.where(low, o_pair, o)
            lse_all = jnp.where(kj == head, m + jnp.log(l), lse_all)
        o_ref[0, :, cols] = o_pair
    lse_ref[0] = lse_all


def _dilated_pattern(qkv, bias, batch, seq, dil):
    sub_len = seq // dil
    nb = sub_len // A_BLOCK
    groups = AB_COLS // A_WIDTH
    view = qkv.reshape(batch, sub_len, dil * AB_COLS)
    blk = (1, A_BLOCK, A_WIDTH)
    cur = lambda off: (lambda b, r, n: (b, n, groups * r + off))
    prev = lambda off: (lambda b, r, n: (b, jnp.maximum(n - 1, 0), groups * r + off))
    o, lse = pl.pallas_call(
        _dilated_kernel,
        grid=(batch, dil, nb),
        in_specs=[
            pl.BlockSpec(blk, cur(0)),
            pl.BlockSpec(blk, prev(1)),
            pl.BlockSpec(blk, cur(1)),
            pl.BlockSpec(blk, prev(2)),
            pl.BlockSpec(blk, cur(2)),
            _const_spec((A_HEADS, A_BLOCK, 2 * A_BLOCK)),
        ],
        out_specs=[
            pl.BlockSpec(blk, lambda b, r, n: (b, n, r)),
            pl.BlockSpec((1, A_BLOCK, 128), lambda b, r, n: (b, n, r)),
        ],
        out_shape=[
            jax.ShapeDtypeStruct((batch, sub_len, dil * A_WIDTH), F32),
            jax.ShapeDtypeStruct((batch, sub_len, dil * 128), F32),
        ],
        compiler_params=_params(3),
        name=f"dilated_attention_d{dil}",
    )(view, view, view, view, view, bias)
    return o.reshape(batch * seq, A_WIDTH), lse.reshape(batch * seq, 128)


def _combine_kernel(o1_ref, o2_ref, o3_ref, l1_ref, l2_ref, l3_ref, y_ref):
    l1, l2, l3 = l1_ref[...], l2_ref[...], l3_ref[...]
    m = jnp.maximum(jnp.maximum(l1, l2), l3)
    e1, e2, e3 = jnp.exp(l1 - m), jnp.exp(l2 - m), jnp.exp(l3 - m)
    den = e1 + e2 + e3
    ws = (e1 / den, e2 / den, e3 / den)
    rows = y_ref.shape[0]
    low = lax.broadcasted_iota(jnp.int32, (rows, 128), 1) < HEAD_DIM
    for p in range(A_HEADS // 2):
        cols = slice(p * 128, (p + 1) * 128)
        acc = jnp.zeros((rows, 128), F32)
        for w, o_ref in zip(ws, (o1_ref, o2_ref, o3_ref)):
            wexp = jnp.where(low, w[:, 2 * p:2 * p + 1], w[:, 2 * p + 1:2 * p + 2])
            acc = acc + wexp * o_ref[:, cols]
        y_ref[:, cols] = acc.astype(BF16)


def _combine_patterns(outs, lses):
    n = outs[0].shape[0]
    tm = 512
    row = lambda i: (i, 0)
    return pl.pallas_call(
        _combine_kernel,
        grid=(n // tm,),
        in_specs=[pl.BlockSpec((tm, A_WIDTH), row)] * 3 + [pl.BlockSpec((tm, 128), row)] * 3,
        out_specs=pl.BlockSpec((tm, A_WIDTH), row),
        out_shape=jax.ShapeDtypeStruct((n, A_WIDTH), BF16),
        compiler_params=_params(1),
        name="dilated_combine",
    )(*outs, *lses)


def _flash_init(m_ref, l_ref, acc_ref):
    m_ref[...] = jnp.full(m_ref.shape, NEG, F32)
    l_ref[...] = jnp.zeros(l_ref.shape, F32)
    acc_ref[...] = jnp.zeros(acc_ref.shape, F32)


def _flash_update(s, v, idx, m_ref, l_ref, acc_ref):
    m_prev = m_ref[idx]
    m_new = jnp.maximum(m_prev, jnp.max(s, axis=-1, keepdims=True))
    alpha = jnp.exp(m_prev - m_new)
    p = jnp.exp(s - m_new)
    l_ref[idx] = alpha * l_ref[idx] + jnp.sum(p, axis=-1, keepdims=True)
    acc_ref[idx] = alpha * acc_ref[idx] + _dot(p.astype(BF16), v)
    m_ref[idx] = m_new


def _causal_mask(t):
    return (lax.broadcasted_iota(jnp.int32, (t, t), 0) >= lax.broadcasted_iota(jnp.int32, (t, t), 1))


def _diff_kernel(lam_init, q_ref, k_ref, v_ref, bias_ref, lam_ref, g_ref, y_ref, m_ref, l_ref, acc_ref):
    i = pl.program_id(2)
    t = ATT_TILE
    q = q_ref[...] * jnp.asarray(B_QK_DIM ** -0.5, BF16)
    low = lax.broadcasted_iota(jnp.int32, q.shape, 1) < B_QK_DIM
    zero = jnp.zeros_like(q)
    q1 = jnp.where(low, q, zero)
    q2 = jnp.where(low, zero, q)
    _flash_init(m_ref, l_ref, acc_ref)

    def chunk(c, masked):
        start = pl.multiple_of(c * t, t)
        k = k_ref[pl.ds(start, t), :]
        v = v_ref[pl.ds(start, t), :]
        bias = bias_ref[0, jnp.minimum(i - c, B_BIAS_TILES - 1)]
        for idx, qx in enumerate((q1, q2)):
            s = _dot_nt(qx, k) + bias
            if masked:
                s = jnp.where(_causal_mask(t), s, NEG)
            _flash_update(s, v, idx, m_ref, l_ref, acc_ref)

    def body(c, carry):
        chunk(c, False)
        return carry

    lax.fori_loop(0, i, body, 0)
    chunk(i, True)

    lv = lam_ref[...]
    lam = (jnp.exp(jnp.sum(lv[0:1] * lv[1:2], axis=-1, keepdims=True))
           - jnp.exp(jnp.sum(lv[2:3] * lv[3:4], axis=-1, keepdims=True)) + lam_init)
    o = acc_ref[0] / l_ref[0] - lam * (acc_ref[1] / l_ref[1])
    y_ref[...] = (_rms(o, g_ref[...]) * (1.0 - lam_init)).astype(BF16)


def _diff_attention(qkv, bias_tiles, lam_vecs, g, batch, seq, lam_init):
    t = ATT_TILE
    nq = seq // t
    q_col0 = 3 * A_WIDTH // 128
    k_col0 = q_col0 + B_HEADS
    v_col0 = k_col0 + B_HEADS
    return pl.pallas_call(
        functools.partial(_diff_kernel, lam_init),
        grid=(batch, B_HEADS, nq),
        in_specs=[
            pl.BlockSpec((t, 128), lambda b, h, i: (b * nq + i, q_col0 + h)),
            pl.BlockSpec((seq, 128), lambda b, h, i: (b, k_col0 + h)),
            pl.BlockSpec((seq, 128), lambda b, h, i: (b, v_col0 + h)),
            pl.BlockSpec((1, B_BIAS_TILES, t, t), lambda b, h, i: (h, 0, 0, 0)),
            _const_spec((4, B_QK_DIM)),
            _const_spec((1, B_V_DIM)),
        ],
        out_specs=pl.BlockSpec((t, 128), lambda b, h, i: (b * nq + i, h)),
        out_shape=jax.ShapeDtypeStruct((batch * seq, B_WIDTH), BF16),
        scratch_shapes=[
            pltpu.VMEM((2, t, 1), F32),
            pltpu.VMEM((2, t, 1), F32),
            pltpu.VMEM((2, t, B_V_DIM), F32),
        ],
        compiler_params=_params(3),
        name="diff_attention",
    )(qkv, qkv, qkv, bias_tiles, lam_vecs, g)


def _mla_kernel(q_ref, k_ref, v_ref, ma_ref, mb_ref, y_ref, m_ref, l_ref, acc_ref):
    i = pl.program_id(2)
    t = ATT_TILE
    scale = (C_NOPE + C_ROPE) ** -0.5
    q = q_ref[...]
    zero = jnp.zeros_like(q)
    qa = jnp.where(ma_ref[...] > 0, q, zero)
    qb = jnp.where(mb_ref[...] > 0, q, zero)
    _flash_init(m_ref, l_ref, acc_ref)

    def chunk(c, masked):
        start = pl.multiple_of(c * t, t)
        k = k_ref[pl.ds(start, t), :]
        v = v_ref[pl.ds(start, t), :]
        for idx, qx in enumerate((qa, qb)):
            s = _dot_nt(qx, k) * scale
            if masked:
                s = jnp.where(_causal_mask(t), s, NEG)
            _flash_update(s, v, idx, m_ref, l_ref, acc_ref)

    def body(c, carry):
        chunk(c, False)
        return carry

    lax.fori_loop(0, i, body, 0)
    chunk(i, True)

    low = lax.broadcasted_iota(jnp.int32, (t, 2 * C_V), 1) < C_V
    y_ref[...] = jnp.where(low, acc_ref[0] / l_ref[0], acc_ref[1] / l_ref[1]).astype(BF16)


def _mla_attention(qc, kc, vc, mask_a, mask_b, batch, seq):
    t = ATT_TILE
    nq = seq // t
    return pl.pallas_call(
        _mla_kernel,
        grid=(batch, C_PAIRS, nq),
        in_specs=[
            pl.BlockSpec((t, C_PAIR_LANES), lambda b, j, i: (b * nq + i, j)),
            pl.BlockSpec((seq, C_PAIR_LANES), lambda b, j, i: (b, j)),
            pl.BlockSpec((seq, 2 * C_V), lambda b, j, i: (b, j)),
            _const_spec((1, C_PAIR_LANES)),
            _const_spec((1, C_PAIR_LANES)),
        ],
        out_specs=pl.BlockSpec((t, 2 * C_V), lambda b, j, i: (b * nq + i, j)),
        out_shape=jax.ShapeDtypeStruct((batch * seq, C_WIDTH), BF16),
        scratch_shapes=[
            pltpu.VMEM((2, t, 1), F32),
            pltpu.VMEM((2, t, 1), F32),
            pltpu.VMEM((2, t, 2 * C_V), F32),
        ],
        compiler_params=_params(3),
        name="mla_attention",
    )(qc, kc, vc, mask_a, mask_b)


def _sigmoid(z):
    return 1.0 / (1.0 + jnp.exp(-z))


def _merge_kernel(x_ref, g_ref, ya_ref, yb_ref, yc_ref, wg_ref, bg_ref, wa_ref, wb_ref, wc_ref,
                  wo_ref, out_ref):
    x = x_ref[...]
    h = _rms(x, g_ref[...]).astype(BF16)
    merged = jnp.zeros(x.shape, F32)
    for k, (y_ref, w_ref) in enumerate(((ya_ref, wa_ref), (yb_ref, wb_ref), (yc_ref, wc_ref))):
        cols = slice(k * D_MODEL, (k + 1) * D_MODEL)
        gate = _sigmoid(_dot(h, wg_ref[:, cols]) + bg_ref[:, cols])
        merged = merged + gate * _dot(y_ref[...], w_ref[...])
    out_ref[...] = x + _dot(merged.astype(BF16), wo_ref[...])


def _merge_stage(x2, g, ya, yb, yc, wg, bg, wa, wb, wc, wo):
    n = x2.shape[0]
    tm = ROW_TILE
    row = lambda i: (i, 0)
    return pl.pallas_call(
        _merge_kernel,
        grid=(n // tm,),
        in_specs=[
            pl.BlockSpec((tm, D_MODEL), row),
            _const_spec((1, D_MODEL)),
            pl.BlockSpec((tm, A_WIDTH), row),
            pl.BlockSpec((tm, B_WIDTH), row),
            pl.BlockSpec((tm, C_WIDTH), row),
            _const_spec((D_MODEL, N_BRANCH * D_MODEL)),
            _const_spec((1, N_BRANCH * D_MODEL)),
            _const_spec((A_WIDTH, D_MODEL)),
            _const_spec((B_WIDTH, D_MODEL)),
            _const_spec((C_WIDTH, D_MODEL)),
            _const_spec((D_MODEL, D_MODEL)),
        ],
        out_specs=pl.BlockSpec((tm, D_MODEL), row),
        out_shape=jax.ShapeDtypeStruct((n, D_MODEL), F32),
        compiler_params=_params(1),
        name="gated_merge",
    )(x2, g, ya, yb, yc, wg, bg, wa, wb, wc, wo)


def _ffn_kernel(final_norm, x_ref, g_ref, wg_ref, wu_ref, wd_ref, gf_ref, out_ref):
    x = x_ref[...]
    h = _rms(x, g_ref[...]).astype(BF16)
    acc = jnp.zeros(x.shape, F32)
    for c in range(0, FFN_HIDDEN, FFN_CHUNK):
        cols = slice(c, c + FFN_CHUNK)
        gate = _dot(h, wg_ref[:, cols])
        act = gate * _sigmoid(gate) * _dot(h, wu_ref[:, cols])
        acc = acc + _dot(act.astype(BF16), wd_ref[cols, :])
    y = x + acc
    if final_norm:
        y = _rms(y, gf_ref[...])
    out_ref[...] = y


def _ffn_stage(x2, g, wg, wu, wd, gf, final_norm):
    n = x2.shape[0]
    tm = ROW_TILE
    row = lambda i: (i, 0)
    return pl.pallas_call(
        functools.partial(_ffn_kernel, final_norm),
        grid=(n // tm,),
        in_specs=[
            pl.BlockSpec((tm, D_MODEL), row),
            _const_spec((1, D_MODEL)),
            _const_spec((D_MODEL, FFN_HIDDEN)),
            _const_spec((D_MODEL, FFN_HIDDEN)),
            _const_spec((FFN_HIDDEN, D_MODEL)),
            _const_spec((1, D_MODEL)),
        ],
        out_specs=pl.BlockSpec((tm, D_MODEL), row),
        out_shape=jax.ShapeDtypeStruct((n, D_MODEL), F32),
        compiler_params=_params(1),
        name="swiglu_final" if final_norm else "swiglu",
    )(x2, g, wg, wu, wd, gf)


def _gather_cols(w, idx):
    padded = jnp.concatenate([w, jnp.zeros((w.shape[0], 1), w.dtype)], axis=1)
    return padded[:, idx]


def kernel(x, rel_bias_table, ln_mix_g, w_in, lambda_q1, lambda_k1, lambda_q2, lambda_k2, diff_subln_g, mla_q_norm_g, w_uq, mla_kv_norm_g, w_ukv, w_gate, b_gate, w_br_a, w_br_b, w_br_c, w_o, ln_ffn_g, w_ffn_gate, w_ffn_up, w_ffn_down, final_norm_g):
    batch, seq, _ = x.shape
    depth = w_in.shape[0]
    assert seq % (A_BLOCK * A_PATTERNS[-1][1]) == 0 and seq % ATT_TILE == 0 and seq % ROW_TILE == 0
    lay = _C_LAYOUT
    cos_t, sin_t = _rope_tables(seq)
    mask_a = jnp.asarray(lay["head_a"], F32)[None, :]
    mask_b = jnp.asarray(lay["head_b"], F32)[None, :]

    bias_a = rel_bias_table[:, :A_HEADS]
    bias_b = rel_bias_table[:, A_HEADS:]
    a_tiles = [jnp.transpose(bias_a[_a_bucket_index(dil)], (2, 0, 1)) for _, dil in A_PATTERNS]
    b_tiles = jnp.transpose(bias_b[_b_bucket_index()], (3, 0, 1, 2))

    cq0 = AB_COLS
    ckv0 = cq0 + C_Q_RANK
    kr0 = ckv0 + C_KV_RANK
    x2 = x.reshape(batch * seq, D_MODEL)
    for l in range(depth):
        w_in_l = w_in[l]
        qkv, qc, kc, vc = _input_stage(
            x2, ln_mix_g[l][None, :],
            w_in_l[:, :AB_COLS].astype(BF16),
            w_in_l[:, cq0:ckv0].astype(BF16), mla_q_norm_g[l][None, :],
            _gather_cols(w_uq[l], lay["q"]).astype(BF16),
            w_in_l[:, ckv0:kr0].astype(BF16), mla_kv_norm_g[l][None, :],
            _gather_cols(w_ukv[l], lay["k"]).astype(BF16),
            w_ukv[l][:, lay["v"]].astype(BF16),
            _gather_cols(w_in_l[:, kr0:], lay["kr"]).astype(BF16),
            cos_t, sin_t, seq)

        outs, lses = [], []
        for (_, dil), tile in zip(A_PATTERNS, a_tiles):
            o, lse = _dilated_pattern(qkv, tile, batch, seq, dil)
            outs.append(o)
            lses.append(lse)
        ya = _combine_patterns(outs, lses)

        lam_init = 0.8 - 0.6 * math.exp(-0.3 * l)
        lam_vecs = jnp.stack([lambda_q1[l], lambda_k1[l], lambda_q2[l], lambda_k2[l]]).astype(F32)
        yb = _diff_attention(qkv, b_tiles, lam_vecs, diff_subln_g[l][None, :], batch, seq, lam_init)
        yc = _mla_attention(qc, kc, vc, mask_a, mask_b, batch, seq)

        x2 = _merge_stage(x2, ln_mix_g[l][None, :], ya, yb, yc,
                          w_gate[l].astype(BF16), b_gate[l][None, :],
                          w_br_a[l].astype(BF16), w_br_b[l].astype(BF16), w_br_c[l].astype(BF16),
                          w_o[l].astype(BF16))
        x2 = _ffn_stage(x2, ln_ffn_g[l][None, :], w_ffn_gate[l].astype(BF16),
                        w_ffn_up[l].astype(BF16), w_ffn_down[l].astype(BF16),
                        final_norm_g[None, :], final_norm=(l == depth - 1))
    return x2.reshape(batch, seq, D_MODEL)
```

```python
import functools
import math

import numpy as np
import jax
import jax.numpy as jnp
from jax import lax
from jax.experimental import pallas as pl
from jax.experimental.pallas import tpu as pltpu

D_MODEL = 1024
HEAD_DIM = 64
A_HEADS = 8
A_PATTERNS = ((128, 1), (512, 4), (2048, 16))
A_WIDTH = A_HEADS * HEAD_DIM
A_BLOCK = 128
B_HEADS = 4
B_QK_DIM = HEAD_DIM
B_V_DIM = 2 * HEAD_DIM
B_WIDTH = B_HEADS * B_V_DIM
C_HEADS = 8
C_PAIRS = C_HEADS // 2
C_Q_RANK = 768
C_KV_RANK = 256
C_NOPE = 64
C_ROPE = 32
C_V = 64
C_WIDTH = C_HEADS * C_V
C_PAIR_LANES = 256
C_COLS = C_PAIRS * C_PAIR_LANES
ROPE_THETA = 10000.0
REL_BUCKETS = 32
REL_MAX_DIST = 2048
AB_COLS = 3 * A_WIDTH + 4 * B_HEADS * B_QK_DIM + B_WIDTH
QB_COL0 = 3 * A_WIDTH
KB_COL0 = QB_COL0 + 2 * B_HEADS * B_QK_DIM
VB_COL0 = KB_COL0 + 2 * B_HEADS * B_QK_DIM
N_BRANCH = 3
FFN_HIDDEN = 2816
EPS = 1e-6
NEG = -1e30

ROW_TILE = 256
ATT_TILE = 512
BIAS_TILE = 256
B_BIAS_TILES = 8
FFN_CHUNK = 256
VMEM_LIMIT = 56 * 1024 * 1024

F32 = jnp.float32
BF16 = jnp.bfloat16


def _dot(a, b):
    return jnp.dot(a, b, preferred_element_type=F32)


def _dot_nt(a, b):
    return lax.dot_general(a, b, (((1,), (1,)), ((), ())), preferred_element_type=F32)


def _rms(x, g):
    return x * lax.rsqrt(jnp.mean(x * x, axis=-1, keepdims=True) + EPS) * g


def _swap_halves(t):
    half = t.shape[1] // 2
    return jnp.concatenate([t[:, half:], t[:, :half]], axis=1)


def _const_spec(shape):
    nd = len(shape)
    return pl.BlockSpec(shape, lambda *_: (0,) * nd, pipeline_mode=pl.Buffered(1))


def _params(n_axes):
    return pltpu.CompilerParams(dimension_semantics=("parallel",) * n_axes,
                                vmem_limit_bytes=VMEM_LIMIT)


def _t5_bucket_np(dist):
    dist = np.maximum(dist, 0)
    exact = REL_BUCKETS // 2
    ratio = np.maximum(dist, 1).astype(np.float32) / np.float32(exact)
    log_ratio = np.log(ratio).astype(np.float32) / np.float32(math.log(REL_MAX_DIST / exact))
    large = np.minimum(exact + (log_ratio * np.float32(REL_BUCKETS - exact)).astype(np.int32),
                       REL_BUCKETS - 1)
    return np.where(dist < exact, dist, large).astype(np.int32)


def _a_bucket_index():
    qi = np.arange(A_BLOCK)[:, None]
    kj = np.arange(2 * A_BLOCK)[None, :]
    step = qi + A_BLOCK - kj
    return np.concatenate([_t5_bucket_np(step * dil) for _, dil in A_PATTERNS], axis=0)


def _b_bucket_index():
    kj = np.arange(BIAS_TILE)[:, None]
    qi = np.arange(BIAS_TILE)[None, :]
    e = np.arange(B_BIAS_TILES)[:, None, None]
    idx = _t5_bucket_np(e * BIAS_TILE + qi - kj)
    assert (idx[-1] == REL_BUCKETS - 1).all()
    return idx.reshape(B_BIAS_TILES * BIAS_TILE, BIAS_TILE)


def _c_layout():
    hq = C_NOPE + C_ROPE
    q_zero = C_HEADS * hq
    kv_zero = C_HEADS * (C_NOPE + C_V)
    half = C_ROPE // 2
    q_idx = np.full((C_PAIRS, C_PAIR_LANES), q_zero, np.int32)
    k_idx = np.full((C_PAIRS, C_PAIR_LANES), kv_zero, np.int32)
    for j in range(C_PAIRS):
        a, b = 2 * j, 2 * j + 1
        q_idx[j, 0:64] = a * hq + np.arange(64)
        q_idx[j, 64:80] = a * hq + C_NOPE + np.arange(half)
        q_idx[j, 80:96] = b * hq + C_NOPE + np.arange(half)
        q_idx[j, 128:192] = b * hq + np.arange(64)
        q_idx[j, 192:208] = a * hq + C_NOPE + half + np.arange(half)
        q_idx[j, 208:224] = b * hq + C_NOPE + half + np.arange(half)
        k_idx[j, 0:64] = a * (C_NOPE + C_V) + np.arange(64)
        k_idx[j, 128:192] = b * (C_NOPE + C_V) + np.arange(64)
    v_idx = (np.arange(C_HEADS)[:, None] * (C_NOPE + C_V) + C_NOPE + np.arange(C_V)[None, :])
    kr_idx = np.full((C_PAIR_LANES,), C_ROPE, np.int32)
    kr_idx[64:80] = np.arange(half)
    kr_idx[80:96] = np.arange(half)
    kr_idx[192:208] = half + np.arange(half)
    kr_idx[208:224] = half + np.arange(half)
    rope_lane = np.full((C_PAIR_LANES,), -1, np.int32)
    for start in (64, 80, 192, 208):
        rope_lane[start:start + half] = np.arange(half)
    sin_sign = np.zeros((C_PAIR_LANES,), np.float32)
    sin_sign[64:96] = -1.0
    sin_sign[192:224] = 1.0
    return dict(q=q_idx.reshape(-1), k=k_idx.reshape(-1), v=v_idx.reshape(-1), kr=kr_idx,
                rope_lane=rope_lane, sin_sign=sin_sign)


_C_LAYOUT = _c_layout()
C_HEAD_A_RANGES = ((0, 80), (192, 208))
C_HEAD_B_RANGES = ((80, 96), (128, 192), (208, 224))


def _rope_tables(seq):
    pos = jnp.arange(seq, dtype=F32)
    inv_freq = ROPE_THETA ** (-jnp.arange(0, C_ROPE, 2, dtype=F32) / C_ROPE)
    ang = pos[:, None] * inv_freq[None, :]
    cos, sin = jnp.cos(ang), jnp.sin(ang)
    lane = _C_LAYOUT["rope_lane"]
    is_rope = jnp.asarray(lane >= 0)
    gather = np.maximum(lane, 0)
    cos_t = jnp.where(is_rope[None, :], cos[:, gather], 1.0)
    sin_t = sin[:, gather] * jnp.asarray(_C_LAYOUT["sin_sign"])[None, :]
    return cos_t, sin_t


def _bias_expand_kernel(n_heads, head0, table_ref, idx_ref, out_ref):
    idx = idx_ref[...]
    hit = [idx == b for b in range(1, REL_BUCKETS)]
    for h in range(n_heads):
        acc = jnp.full(idx.shape, table_ref[0, head0 + h], F32)
        for b in range(1, REL_BUCKETS):
            acc = jnp.where(hit[b - 1], table_ref[b, head0 + h], acc)
        out_ref[h] = acc


def _bias_expand(table, idx, n_heads, head0, block_rows, name):
    rows, cols = idx.shape
    return pl.pallas_call(
        functools.partial(_bias_expand_kernel, n_heads, head0),
        grid=(rows // block_rows,),
        in_specs=[
            pl.BlockSpec(memory_space=pltpu.SMEM),
            pl.BlockSpec((block_rows, cols), lambda i: (i, 0)),
        ],
        out_specs=pl.BlockSpec((n_heads, block_rows, cols), lambda i: (0, i, 0)),
        out_shape=jax.ShapeDtypeStruct((n_heads, rows, cols), F32),
        compiler_params=_params(1),
        name=name,
    )(table, jnp.asarray(idx))


def _input_kernel(x_ref, g_ref, wab_ref, wcq_ref, gq_ref, wuq_ref, wckv_ref, gkv_ref, wuk_ref,
                  wuv_ref, wkr_ref, cos_ref, sin_ref,
                  qkv_ref, qbt_ref, vbt_ref, qct_ref, kc_ref, vct_ref):
    h = _rms(x_ref[...], g_ref[...]).astype(BF16)
    for c in range(0, AB_COLS, 512):
        t = _dot(h, wab_ref[:, c:c + 512])
        qkv_ref[:, c:c + 512] = t.astype(BF16)
        if c == QB_COL0:
            qbt_ref[...] = t.T.astype(BF16)
        if c == VB_COL0:
            vbt_ref[...] = t.T.astype(BF16)
    cos = cos_ref[...]
    sin = sin_ref[...]
    cqn = _rms(_dot(h, wcq_ref[...]), gq_ref[...]).astype(BF16)
    for j in range(C_PAIRS):
        cols = slice(j * C_PAIR_LANES, (j + 1) * C_PAIR_LANES)
        t = _dot(cqn, wuq_ref[:, cols])
        qct_ref[cols, :] = (t * cos + _swap_halves(t) * sin).T.astype(BF16)
    ckvn = _rms(_dot(h, wckv_ref[...]), gkv_ref[...]).astype(BF16)
    kr = _dot(h, wkr_ref[...])
    kr_rot = kr * cos + _swap_halves(kr) * sin
    for j in range(C_PAIRS):
        cols = slice(j * C_PAIR_LANES, (j + 1) * C_PAIR_LANES)
        kc_ref[:, cols] = (_dot(ckvn, wuk_ref[:, cols]) + kr_rot).astype(BF16)
    vct_ref[...] = _dot(ckvn, wuv_ref[...]).T.astype(BF16)


def _input_stage(x2, g, wab, wcq, gq, wuq, wckv, gkv, wuk, wuv, wkr, cos_t, sin_t, seq):
    n = x2.shape[0]
    tm = ROW_TILE
    pos_blocks = seq // tm
    row = lambda i: (i, 0)
    col = lambda i: (0, i)
    return pl.pallas_call(
        _input_kernel,
        grid=(n // tm,),
        in_specs=[
            pl.BlockSpec((tm, D_MODEL), row),
            _const_spec((1, D_MODEL)),
            _const_spec((D_MODEL, AB_COLS)),
            _const_spec((D_MODEL, C_Q_RANK)),
            _const_spec((1, C_Q_RANK)),
            _const_spec((C_Q_RANK, C_COLS)),
            _const_spec((D_MODEL, C_KV_RANK)),
            _const_spec((1, C_KV_RANK)),
            _const_spec((C_KV_RANK, C_COLS)),
            _const_spec((C_KV_RANK, C_WIDTH)),
            _const_spec((D_MODEL, C_PAIR_LANES)),
            pl.BlockSpec((tm, C_PAIR_LANES), lambda i: (i % pos_blocks, 0)),
            pl.BlockSpec((tm, C_PAIR_LANES), lambda i: (i % pos_blocks, 0)),
        ],
        out_specs=[
            pl.BlockSpec((tm, AB_COLS), row),
            pl.BlockSpec((B_WIDTH, tm), col),
            pl.BlockSpec((B_WIDTH, tm), col),
            pl.BlockSpec((C_COLS, tm), col),
            pl.BlockSpec((tm, C_COLS), row),
            pl.BlockSpec((C_WIDTH, tm), col),
        ],
        out_shape=[
            jax.ShapeDtypeStruct((n, AB_COLS), BF16),
            jax.ShapeDtypeStruct((B_WIDTH, n), BF16),
            jax.ShapeDtypeStruct((B_WIDTH, n), BF16),
            jax.ShapeDtypeStruct((C_COLS, n), BF16),
            jax.ShapeDtypeStruct((n, C_COLS), BF16),
            jax.ShapeDtypeStruct((C_WIDTH, n), BF16),
        ],
        compiler_params=_params(1),
        name="input_stage",
    )(x2, g, wab, wcq, gq, wuq, wckv, gkv, wuk, wuv, wkr, cos_t, sin_t)


def _dilated_kernel(q_ref, kp_ref, kc_ref, vp_ref, vc_ref, bias_ref, o_ref, lse_ref):
    n = pl.program_id(2)
    qi = lax.broadcasted_iota(jnp.int32, (A_BLOCK, A_BLOCK), 0)
    kj = lax.broadcasted_iota(jnp.int32, (A_BLOCK, A_BLOCK), 1)
    mask_prev = (kj >= qi) & (n > 0)
    mask_cur = kj <= qi
    low = kj < HEAD_DIM
    lse_all = jnp.zeros((A_BLOCK, A_BLOCK), F32)
    for p in range(A_HEADS // 2):
        cols = slice(p * 128, (p + 1) * 128)
        q2 = q_ref[0, :, cols] * jnp.asarray(HEAD_DIM ** -0.5, BF16)
        kp, kc = kp_ref[0, :, cols], kc_ref[0, :, cols]
        vp, vc = vp_ref[0, :, cols], vc_ref[0, :, cols]
        o_pair = None
        for hh in range(2):
            head = 2 * p + hh
            qm = jnp.where(low if hh == 0 else ~low, q2, jnp.zeros_like(q2))
            sp = jnp.where(mask_prev, _dot_nt(qm, kp) + bias_ref[head, :, :A_BLOCK], NEG)
            sc = jnp.where(mask_cur, _dot_nt(qm, kc) + bias_ref[head, :, A_BLOCK:], NEG)
            m = jnp.maximum(jnp.max(sp, axis=-1, keepdims=True), jnp.max(sc, axis=-1, keepdims=True))
            pp = jnp.exp(sp - m)
            pc = jnp.exp(sc - m)
            l = jnp.sum(pp, axis=-1, keepdims=True) + jnp.sum(pc, axis=-1, keepdims=True)
            o = (_dot(pp.astype(BF16), vp) + _dot(pc.astype(BF16), vc)) / l
            o_pair = o if hh == 0 else jnp.where(low, o_pair, o)
            lse_all = jnp.where(kj == head, m + jnp.log(l), lse_all)
        o_ref[0, :, cols] = o_pair
    lse_ref[0] = lse_all


def _dilated_pattern(qkv, bias, batch, seq, dil):
    sub_len = seq // dil
    nb = sub_len // A_BLOCK
    groups = AB_COLS // A_WIDTH
    view = qkv.reshape(batch, sub_len, dil * AB_COLS)
    blk = (1, A_BLOCK, A_WIDTH)
    cur = lambda off: (lambda b, r, n: (b, n, groups * r + off))
    prev = lambda off: (lambda b, r, n: (b, jnp.maximum(n - 1, 0), groups * r + off))
    o, lse = pl.pallas_call(
        _dilated_kernel,
        grid=(batch, dil, nb),
        in_specs=[
            pl.BlockSpec(blk, cur(0)),
            pl.BlockSpec(blk, prev(1)),
            pl.BlockSpec(blk, cur(1)),
            pl.BlockSpec(blk, prev(2)),
            pl.BlockSpec(blk, cur(2)),
            _const_spec((A_HEADS, A_BLOCK, 2 * A_BLOCK)),
        ],
        out_specs=[
            pl.BlockSpec(blk, lambda b, r, n: (b, n, r)),
            pl.BlockSpec((1, A_BLOCK, 128), lambda b, r, n: (b, n, r)),
        ],
        out_shape=[
            jax.ShapeDtypeStruct((batch, sub_len, dil * A_WIDTH), F32),
            jax.ShapeDtypeStruct((batch, sub_len, dil * 128), F32),
        ],
        compiler_params=_params(3),
        name=f"dilated_attention_d{dil}",
    )(view, view, view, view, view, bias)
    return o.reshape(batch * seq, A_WIDTH), lse.reshape(batch * seq, 128)


def _combine_kernel(o1_ref, o2_ref, o3_ref, l1_ref, l2_ref, l3_ref, y_ref):
    l1, l2, l3 = l1_ref[...], l2_ref[...], l3_ref[...]
    m = jnp.maximum(jnp.maximum(l1, l2), l3)
    e1, e2, e3 = jnp.exp(l1 - m), jnp.exp(l2 - m), jnp.exp(l3 - m)
    den = e1 + e2 + e3
    ws = (e1 / den, e2 / den, e3 / den)
    rows = y_ref.shape[0]
    low = lax.broadcasted_iota(jnp.int32, (rows, 128), 1) < HEAD_DIM
    for p in range(A_HEADS // 2):
        cols = slice(p * 128, (p + 1) * 128)
        acc = jnp.zeros((rows, 128), F32)
        for w, o_ref in zip(ws, (o1_ref, o2_ref, o3_ref)):
            wexp = jnp.where(low, w[:, 2 * p:2 * p + 1], w[:, 2 * p + 1:2 * p + 2])
            acc = acc + wexp * o_ref[:, cols]
        y_ref[:, cols] = acc.astype(BF16)


def _combine_patterns(outs, lses):
    n = outs[0].shape[0]
    tm = 512
    row = lambda i: (i, 0)
    return pl.pallas_call(
        _combine_kernel,
        grid=(n // tm,),
        in_specs=[pl.BlockSpec((tm, A_WIDTH), row)] * 3 + [pl.BlockSpec((tm, 128), row)] * 3,
        out_specs=pl.BlockSpec((tm, A_WIDTH), row),
        out_shape=jax.ShapeDtypeStruct((n, A_WIDTH), BF16),
        compiler_params=_params(1),
        name="dilated_combine",
    )(*outs, *lses)


def _flash_init(m_ref, l_ref, acc_ref):
    m_ref[...] = jnp.full(m_ref.shape, NEG, F32)
    l_ref[...] = jnp.zeros(l_ref.shape, F32)
    acc_ref[...] = jnp.zeros(acc_ref.shape, F32)


def _flash_update(s, v_t, idx, m_ref, l_ref, acc_ref):
    m_prev = m_ref[idx]
    m_new = jnp.maximum(m_prev, jnp.max(s, axis=0, keepdims=True))
    alpha = jnp.exp(m_prev - m_new)
    p = jnp.exp(s - m_new)
    l_ref[idx] = alpha * l_ref[idx] + jnp.sum(p, axis=0, keepdims=True)
    acc_ref[idx] = alpha * acc_ref[idx] + _dot(v_t, p.astype(BF16))
    m_ref[idx] = m_new


def _causal_mask_t(t):
    key = lax.broadcasted_iota(jnp.int32, (t, t), 0)
    query = lax.broadcasted_iota(jnp.int32, (t, t), 1)
    return key <= query


def _row_in_ranges(shape, ranges):
    row = lax.broadcasted_iota(jnp.int32, shape, 0)
    hit = None
    for lo, hi in ranges:
        r = (row >= lo) & (row < hi)
        hit = r if hit is None else hit | r
    return hit


def _diff_bias(bias_ref, d):
    sub = ATT_TILE // BIAS_TILE
    rows = []
    for ka in range(sub):
        tiles = []
        for qa in range(sub):
            e = sub * d + qa - ka
            e = min(max(e, 0), B_BIAS_TILES - 1) if isinstance(e, int) else jnp.clip(e, 0, B_BIAS_TILES - 1)
            tiles.append(bias_ref[0, e])
        rows.append(jnp.concatenate(tiles, axis=1))
    return jnp.concatenate(rows, axis=0)


def _diff_kernel(lam_init, qt_ref, k_ref, vt_ref, bias_ref, lam_ref, g_ref, y_ref, m_ref, l_ref, acc_ref):
    i = pl.program_id(2)
    t = ATT_TILE
    qf = qt_ref[...].astype(F32) * (B_QK_DIM ** -0.5)
    first = lax.broadcasted_iota(jnp.int32, qf.shape, 0) < B_QK_DIM
    q1 = jnp.where(first, qf, 0.0).astype(BF16)
    q2 = jnp.where(first, 0.0, qf).astype(BF16)
    _flash_init(m_ref, l_ref, acc_ref)

    def chunk(c, d, masked):
        start = pl.multiple_of(c * t, t)
        k = k_ref[pl.ds(start, t), :]
        v_t = vt_ref[:, pl.ds(start, t)]
        bias = _diff_bias(bias_ref, d)
        for idx, qx in enumerate((q1, q2)):
            s = _dot(k, qx) + bias
            if masked:
                s = jnp.where(_causal_mask_t(t), s, NEG)
            _flash_update(s, v_t, idx, m_ref, l_ref, acc_ref)

    def body(c, carry):
        chunk(c, i - c, False)
        return carry

    lax.fori_loop(0, i, body, 0)
    chunk(i, 0, True)

    lv = lam_ref[...]
    lam = (jnp.exp(jnp.sum(lv[0:1] * lv[1:2], axis=-1, keepdims=True))
           - jnp.exp(jnp.sum(lv[2:3] * lv[3:4], axis=-1, keepdims=True)) + lam_init)
    o = acc_ref[0] / l_ref[0] - lam * (acc_ref[1] / l_ref[1])
    y = o * lax.rsqrt(jnp.mean(o * o, axis=0, keepdims=True) + EPS) * g_ref[...] * (1.0 - lam_init)
    y_ref[...] = y.T.astype(BF16)


def _diff_attention(qb_t, qkv, vb_t, bias_tiles, lam_vecs, g_col, batch, seq, lam_init):
    t = ATT_TILE
    nq = seq // t
    k_col0 = KB_COL0 // 128
    return pl.pallas_call(
        functools.partial(_diff_kernel, lam_init),
        grid=(batch, B_HEADS, nq),
        in_specs=[
            pl.BlockSpec((128, t), lambda b, h, i: (h, b * nq + i)),
            pl.BlockSpec((seq, 128), lambda b, h, i: (b, k_col0 + h)),
            pl.BlockSpec((128, seq), lambda b, h, i: (h, b)),
            pl.BlockSpec((1, B_BIAS_TILES, BIAS_TILE, BIAS_TILE), lambda b, h, i: (h, 0, 0, 0)),
            _const_spec((4, B_QK_DIM)),
            _const_spec((B_V_DIM, 1)),
        ],
        out_specs=pl.BlockSpec((t, 128), lambda b, h, i: (b * nq + i, h)),
        out_shape=jax.ShapeDtypeStruct((batch * seq, B_WIDTH), BF16),
        scratch_shapes=[
            pltpu.VMEM((2, 1, t), F32),
            pltpu.VMEM((2, 1, t), F32),
            pltpu.VMEM((2, B_V_DIM, t), F32),
        ],
        compiler_params=_params(3),
        name="diff_attention",
    )(qb_t, qkv, vb_t, bias_tiles, lam_vecs, g_col)


def _mla_kernel(qt_ref, k_ref, vt_ref, y_ref, m_ref, l_ref, acc_ref):
    i = pl.program_id(2)
    t = ATT_TILE
    scale = (C_NOPE + C_ROPE) ** -0.5
    qf = qt_ref[...].astype(F32)
    qa = jnp.where(_row_in_ranges(qf.shape, C_HEAD_A_RANGES), qf, 0.0).astype(BF16)
    qb = jnp.where(_row_in_ranges(qf.shape, C_HEAD_B_RANGES), qf, 0.0).astype(BF16)
    _flash_init(m_ref, l_ref, acc_ref)

    def chunk(c, masked):
        start = pl.multiple_of(c * t, t)
        k = k_ref[pl.ds(start, t), :]
        v_t = vt_ref[:, pl.ds(start, t)]
        for idx, qx in enumerate((qa, qb)):
            s = _dot(k, qx) * scale
            if masked:
                s = jnp.where(_causal_mask_t(t), s, NEG)
            _flash_update(s, v_t[idx * C_V:(idx + 1) * C_V], idx, m_ref, l_ref, acc_ref)

    def body(c, carry):
        chunk(c, False)
        return carry

    lax.fori_loop(0, i, body, 0)
    chunk(i, True)

    o = jnp.concatenate([acc_ref[0] / l_ref[0], acc_ref[1] / l_ref[1]], axis=0)
    y_ref[...] = o.T.astype(BF16)


def _mla_attention(qc_t, kc, vc_t, batch, seq):
    t = ATT_TILE
    nq = seq // t
    return pl.pallas_call(
        _mla_kernel,
        grid=(batch, C_PAIRS, nq),
        in_specs=[
            pl.BlockSpec((C_PAIR_LANES, t), lambda b, j, i: (j, b * nq + i)),
            pl.BlockSpec((seq, C_PAIR_LANES), lambda b, j, i: (b, j)),
            pl.BlockSpec((2 * C_V, seq), lambda b, j, i: (j, b)),
        ],
        out_specs=pl.BlockSpec((t, 2 * C_V), lambda b, j, i: (b * nq + i, j)),
        out_shape=jax.ShapeDtypeStruct((batch * seq, C_WIDTH), BF16),
        scratch_shapes=[
            pltpu.VMEM((2, 1, t), F32),
            pltpu.VMEM((2, 1, t), F32),
            pltpu.VMEM((2, C_V, t), F32),
        ],
        compiler_params=_params(3),
        name="mla_attention",
    )(qc_t, kc, vc_t)


def _sigmoid(z):
    return 1.0 / (1.0 + jnp.exp(-z))


def _merge_kernel(x_ref, g_ref, ya_ref, yb_ref, yc_ref, wg_ref, bg_ref, wa_ref, wb_ref, wc_ref,
                  wo_ref, out_ref):
    x = x_ref[...]
    h = _rms(x, g_ref[...]).astype(BF16)
    merged = jnp.zeros(x.shape, F32)
    for k, (y_ref, w_ref) in enumerate(((ya_ref, wa_ref), (yb_ref, wb_ref), (yc_ref, wc_ref))):
        cols = slice(k * D_MODEL, (k + 1) * D_MODEL)
        gate = _sigmoid(_dot(h, wg_ref[:, cols]) + bg_ref[:, cols])
        merged = merged + gate * _dot(y_ref[...], w_ref[...])
    out_ref[...] = x + _dot(merged.astype(BF16), wo_ref[...])


def _merge_stage(x2, g, ya, yb, yc, wg, bg, wa, wb, wc, wo):
    n = x2.shape[0]
    tm = ROW_TILE
    row = lambda i: (i, 0)
    return pl.pallas_call(
        _merge_kernel,
        grid=(n // tm,),
        in_specs=[
            pl.BlockSpec((tm, D_MODEL), row),
            _const_spec((1, D_MODEL)),
            pl.BlockSpec((tm, A_WIDTH), row),
            pl.BlockSpec((tm, B_WIDTH), row),
            pl.BlockSpec((tm, C_WIDTH), row),
            _const_spec((D_MODEL, N_BRANCH * D_MODEL)),
            _const_spec((1, N_BRANCH * D_MODEL)),
            _const_spec((A_WIDTH, D_MODEL)),
            _const_spec((B_WIDTH, D_MODEL)),
            _const_spec((C_WIDTH, D_MODEL)),
            _const_spec((D_MODEL, D_MODEL)),
        ],
        out_specs=pl.BlockSpec((tm, D_MODEL), row),
        out_shape=jax.ShapeDtypeStruct((n, D_MODEL), F32),
        compiler_params=_params(1),
        name="gated_merge",
    )(x2, g, ya, yb, yc, wg, bg, wa, wb, wc, wo)


def _ffn_kernel(final_norm, x_ref, g_ref, wg_ref, wu_ref, wd_ref, gf_ref, out_ref):
    x = x_ref[...]
    h = _rms(x, g_ref[...]).astype(BF16)
    acc = jnp.zeros(x.shape, F32)
    for c in range(0, FFN_HIDDEN, FFN_CHUNK):
        cols = slice(c, c + FFN_CHUNK)
        gate = _dot(h, wg_ref[:, cols])
        act = gate * _sigmoid(gate) * _dot(h, wu_ref[:, cols])
        acc = acc + _dot(act.astype(BF16), wd_ref[cols, :])
    y = x + acc
    if final_norm:
        y = _rms(y, gf_ref[...])
    out_ref[...] = y


def _ffn_stage(x2, g, wg, wu, wd, gf, final_norm):
    n = x2.shape[0]
    tm = ROW_TILE
    row = lambda i: (i, 0)
    return pl.pallas_call(
        functools.partial(_ffn_kernel, final_norm),
        grid=(n // tm,),
        in_specs=[
            pl.BlockSpec((tm, D_MODEL), row),
            _const_spec((1, D_MODEL)),
            _const_spec((D_MODEL, FFN_HIDDEN)),
            _const_spec((D_MODEL, FFN_HIDDEN)),
            _const_spec((FFN_HIDDEN, D_MODEL)),
            _const_spec((1, D_MODEL)),
        ],
        out_specs=pl.BlockSpec((tm, D_MODEL), row),
        out_shape=jax.ShapeDtypeStruct((n, D_MODEL), F32),
        compiler_params=_params(1),
        name="swiglu_final" if final_norm else "swiglu",
    )(x2, g, wg, wu, wd, gf)


def _gather_cols(w, idx):
    padded = jnp.concatenate([w, jnp.zeros((w.shape[0], 1), w.dtype)], axis=1)
    return padded[:, idx]


def kernel(x, rel_bias_table, ln_mix_g, w_in, lambda_q1, lambda_k1, lambda_q2, lambda_k2, diff_subln_g, mla_q_norm_g, w_uq, mla_kv_norm_g, w_ukv, w_gate, b_gate, w_br_a, w_br_b, w_br_c, w_o, ln_ffn_g, w_ffn_gate, w_ffn_up, w_ffn_down, final_norm_g):
    batch, seq, _ = x.shape
    depth = w_in.shape[0]
    assert seq % (A_BLOCK * A_PATTERNS[-1][1]) == 0 and seq % ATT_TILE == 0 and seq % ROW_TILE == 0
    lay = _C_LAYOUT
    cos_t, sin_t = _rope_tables(seq)

    a_tiles = _bias_expand(rel_bias_table, _a_bucket_index(), A_HEADS, 0, A_BLOCK, "bias_expand_a")
    b_tiles = _bias_expand(rel_bias_table, _b_bucket_index(), B_HEADS, A_HEADS, BIAS_TILE,
                           "bias_expand_b").reshape(B_HEADS, B_BIAS_TILES, BIAS_TILE, BIAS_TILE)

    cq0 = AB_COLS
    ckv0 = cq0 + C_Q_RANK
    kr0 = ckv0 + C_KV_RANK
    x2 = x.reshape(batch * seq, D_MODEL)
    for l in range(depth):
        w_in_l = w_in[l]
        qkv, qb_t, vb_t, qc_t, kc, vc_t = _input_stage(
            x2, ln_mix_g[l][None, :],
            w_in_l[:, :AB_COLS].astype(BF16),
            w_in_l[:, cq0:ckv0].astype(BF16), mla_q_norm_g[l][None, :],
            _gather_cols(w_uq[l], lay["q"]).astype(BF16),
            w_in_l[:, ckv0:kr0].astype(BF16), mla_kv_norm_g[l][None, :],
            _gather_cols(w_ukv[l], lay["k"]).astype(BF16),
            w_ukv[l][:, lay["v"]].astype(BF16),
            _gather_cols(w_in_l[:, kr0:], lay["kr"]).astype(BF16),
            cos_t, sin_t, seq)

        outs, lses = [], []
        for g, (_, dil) in enumerate(A_PATTERNS):
            o, lse = _dilated_pattern(qkv, a_tiles[:, g * A_BLOCK:(g + 1) * A_BLOCK], batch, seq, dil)
            outs.append(o)
            lses.append(lse)
        ya = _combine_patterns(outs, lses)

        lam_init = 0.8 - 0.6 * math.exp(-0.3 * l)
        lam_vecs = jnp.stack([lambda_q1[l], lambda_k1[l], lambda_q2[l], lambda_k2[l]]).astype(F32)
        yb = _diff_attention(qb_t, qkv, vb_t, b_tiles, lam_vecs, diff_subln_g[l][:, None],
                             batch, seq, lam_init)
        yc = _mla_attention(qc_t, kc, vc_t, batch, seq)

        x2 = _merge_stage(x2, ln_mix_g[l][None, :], ya, yb, yc,
                          w_gate[l].astype(BF16), b_gate[l][None, :],
                          w_br_a[l].astype(BF16), w_br_b[l].astype(BF16), w_br_c[l].astype(BF16),
                          w_o[l].astype(BF16))
        x2 = _ffn_stage(x2, ln_ffn_g[l][None, :], w_ffn_gate[l].astype(BF16),
                        w_ffn_up[l].astype(BF16), w_ffn_down[l].astype(BF16),
                        final_norm_g[None, :], final_norm=(l == depth - 1))
    return x2.reshape(batch, seq, D_MODEL)
```

```python
import functools
import math

import numpy as np
import jax
import jax.numpy as jnp
from jax import lax
from jax.experimental import pallas as pl
from jax.experimental.pallas import tpu as pltpu

D_MODEL = 1024
HEAD_DIM = 64
A_HEADS = 8
A_PATTERNS = ((128, 1), (512, 4), (2048, 16))
A_WIDTH = A_HEADS * HEAD_DIM
A_BLOCK = 128
A_BLOCKS_PER_STEP = 8
B_HEADS = 4
B_QK_DIM = HEAD_DIM
B_V_DIM = 2 * HEAD_DIM
B_WIDTH = B_HEADS * B_V_DIM
C_HEADS = 8
C_PAIRS = C_HEADS // 2
C_Q_RANK = 768
C_KV_RANK = 256
C_NOPE = 64
C_ROPE = 32
C_V = 64
C_WIDTH = C_HEADS * C_V
C_PAIR_LANES = 256
C_COLS = C_PAIRS * C_PAIR_LANES
ROPE_THETA = 10000.0
REL_BUCKETS = 32
REL_MAX_DIST = 2048
AB_COLS = 3 * A_WIDTH + 4 * B_HEADS * B_QK_DIM + B_WIDTH
QB_COL0 = 3 * A_WIDTH
KB_COL0 = QB_COL0 + 2 * B_HEADS * B_QK_DIM
VB_COL0 = KB_COL0 + 2 * B_HEADS * B_QK_DIM
N_BRANCH = 3
FFN_HIDDEN = 2816
EPS = 1e-6
NEG = -1e30
LOG2E = math.log2(math.e)
B_SCORE_SCALE = B_QK_DIM ** -0.5 * LOG2E
C_SCORE_SCALE = (C_NOPE + C_ROPE) ** -0.5 * LOG2E
ONES_ROWS = 16

ROW_TILE = 256
ATT_TILE = 512
BIAS_TILE = 256
B_BIAS_TILES = 8
FFN_CHUNK = 256
VMEM_LIMIT = 56 * 1024 * 1024

F32 = jnp.float32
BF16 = jnp.bfloat16


def _dot(a, b):
    return jnp.dot(a, b, preferred_element_type=F32)


def _dot_nt(a, b):
    return lax.dot_general(a, b, (((1,), (1,)), ((), ())), preferred_element_type=F32)


def _rms(x, g):
    return x * lax.rsqrt(jnp.mean(x * x, axis=-1, keepdims=True) + EPS) * g


def _swap_halves(t):
    half = t.shape[1] // 2
    return jnp.concatenate([t[:, half:], t[:, :half]], axis=1)


def _const_spec(shape):
    nd = len(shape)
    return pl.BlockSpec(shape, lambda *_: (0,) * nd, pipeline_mode=pl.Buffered(1))


def _params(n_axes):
    return pltpu.CompilerParams(dimension_semantics=("parallel",) * n_axes,
                                vmem_limit_bytes=VMEM_LIMIT)


def _t5_bucket_np(dist):
    dist = np.maximum(dist, 0)
    exact = REL_BUCKETS // 2
    ratio = np.maximum(dist, 1).astype(np.float32) / np.float32(exact)
    log_ratio = np.log(ratio).astype(np.float32) / np.float32(math.log(REL_MAX_DIST / exact))
    large = np.minimum(exact + (log_ratio * np.float32(REL_BUCKETS - exact)).astype(np.int32),
                       REL_BUCKETS - 1)
    return np.where(dist < exact, dist, large).astype(np.int32)


MASKED_BUCKET = REL_BUCKETS


def _a_bucket_index():
    kj = np.arange(2 * A_BLOCK)[:, None]
    qi = np.arange(A_BLOCK)[None, :]
    step = qi + A_BLOCK - kj
    in_band = (step >= 0) & (step <= A_BLOCK)
    tiles = []
    for _, dil in A_PATTERNS:
        bucket = _t5_bucket_np(step * dil)
        for first_block in (False, True):
            valid = in_band & (kj >= A_BLOCK) if first_block else in_band
            tiles.append(np.where(valid, bucket, MASKED_BUCKET))
    return np.concatenate(tiles, axis=0).astype(np.int32)


def _b_bucket_index():
    kj = np.arange(BIAS_TILE)[:, None]
    qi = np.arange(BIAS_TILE)[None, :]
    e = np.arange(B_BIAS_TILES)[:, None, None]
    idx = _t5_bucket_np(e * BIAS_TILE + qi - kj)
    assert (idx[-1] == REL_BUCKETS - 1).all()
    return idx.reshape(B_BIAS_TILES * BIAS_TILE, BIAS_TILE)


def _c_layout():
    hq = C_NOPE + C_ROPE
    q_zero = C_HEADS * hq
    kv_zero = C_HEADS * (C_NOPE + C_V)
    half = C_ROPE // 2
    q_idx = np.full((C_PAIRS, C_PAIR_LANES), q_zero, np.int32)
    k_idx = np.full((C_PAIRS, C_PAIR_LANES), kv_zero, np.int32)
    for j in range(C_PAIRS):
        a, b = 2 * j, 2 * j + 1
        q_idx[j, 0:64] = a * hq + np.arange(64)
        q_idx[j, 64:80] = a * hq + C_NOPE + np.arange(half)
        q_idx[j, 80:96] = b * hq + C_NOPE + np.arange(half)
        q_idx[j, 128:192] = b * hq + np.arange(64)
        q_idx[j, 192:208] = a * hq + C_NOPE + half + np.arange(half)
        q_idx[j, 208:224] = b * hq + C_NOPE + half + np.arange(half)
        k_idx[j, 0:64] = a * (C_NOPE + C_V) + np.arange(64)
        k_idx[j, 128:192] = b * (C_NOPE + C_V) + np.arange(64)
    v_idx = (np.arange(C_HEADS)[:, None] * (C_NOPE + C_V) + C_NOPE + np.arange(C_V)[None, :])
    kr_idx = np.full((C_PAIR_LANES,), C_ROPE, np.int32)
    kr_idx[64:80] = np.arange(half)
    kr_idx[80:96] = np.arange(half)
    kr_idx[192:208] = half + np.arange(half)
    kr_idx[208:224] = half + np.arange(half)
    rope_lane = np.full((C_PAIR_LANES,), -1, np.int32)
    for start in (64, 80, 192, 208):
        rope_lane[start:start + half] = np.arange(half)
    sin_sign = np.zeros((C_PAIR_LANES,), np.float32)
    sin_sign[64:96] = -1.0
    sin_sign[192:224] = 1.0
    return dict(q=q_idx.reshape(-1), k=k_idx.reshape(-1), v=v_idx.reshape(-1), kr=kr_idx,
                rope_lane=rope_lane, sin_sign=sin_sign)


_C_LAYOUT = _c_layout()
C_HEAD_A_RANGES = ((0, 80), (192, 208))
C_HEAD_B_RANGES = ((80, 96), (128, 192), (208, 224))


def _rope_tables(seq):
    pos = jnp.arange(seq, dtype=F32)
    inv_freq = ROPE_THETA ** (-jnp.arange(0, C_ROPE, 2, dtype=F32) / C_ROPE)
    ang = pos[:, None] * inv_freq[None, :]
    cos, sin = jnp.cos(ang), jnp.sin(ang)
    lane = _C_LAYOUT["rope_lane"]
    is_rope = jnp.asarray(lane >= 0)
    gather = np.maximum(lane, 0)
    cos_t = jnp.where(is_rope[None, :], cos[:, gather], 1.0)
    sin_t = sin[:, gather] * jnp.asarray(_C_LAYOUT["sin_sign"])[None, :]
    return cos_t, sin_t


def _bias_expand_kernel(n_heads, head0, gain, table_ref, idx_ref, out_ref):
    idx = idx_ref[...]
    hit = [idx == b for b in range(1, REL_BUCKETS)]
    masked = idx == MASKED_BUCKET
    for h in range(n_heads):
        acc = jnp.full(idx.shape, table_ref[0, head0 + h] * gain, F32)
        for b in range(1, REL_BUCKETS):
            acc = jnp.where(hit[b - 1], table_ref[b, head0 + h] * gain, acc)
        out_ref[h] = jnp.where(masked, NEG, acc)


def _bias_expand(table, idx, n_heads, head0, block_rows, name, gain=1.0):
    rows, cols = idx.shape
    return pl.pallas_call(
        functools.partial(_bias_expand_kernel, n_heads, head0, gain),
        grid=(rows // block_rows,),
        in_specs=[
            pl.BlockSpec(memory_space=pltpu.SMEM),
            pl.BlockSpec((block_rows, cols), lambda i: (i, 0)),
        ],
        out_specs=pl.BlockSpec((n_heads, block_rows, cols), lambda i: (0, i, 0)),
        out_shape=jax.ShapeDtypeStruct((n_heads, rows, cols), F32),
        compiler_params=_params(1),
        name=name,
    )(table, jnp.asarray(idx))


def _input_kernel(x_ref, g_ref, wab_ref, wcq_ref, gq_ref, wuq_ref, wckv_ref, gkv_ref, wuk_ref,
                  wuv_ref, wkr_ref, cos_ref, sin_ref,
                  qkv_ref, qbt_ref, vbt_ref, qct_ref, kc_ref, vct_ref):
    h = _rms(x_ref[...], g_ref[...]).astype(BF16)
    for c in range(0, AB_COLS, 512):
        t = _dot(h, wab_ref[:, c:c + 512])
        qkv_ref[:, c:c + 512] = t.astype(BF16)
        if c == QB_COL0:
            qbt_ref[...] = (t * B_SCORE_SCALE).T.astype(BF16)
        if c == VB_COL0:
            vbt_ref[...] = t.T.astype(BF16)
    cos = cos_ref[...]
    sin = sin_ref[...]
    cqn = _rms(_dot(h, wcq_ref[...]), gq_ref[...]).astype(BF16)
    for j in range(C_PAIRS):
        cols = slice(j * C_PAIR_LANES, (j + 1) * C_PAIR_LANES)
        t = _dot(cqn, wuq_ref[:, cols])
        qct_ref[cols, :] = ((t * cos + _swap_halves(t) * sin) * C_SCORE_SCALE).T.astype(BF16)
    ckvn = _rms(_dot(h, wckv_ref[...]), gkv_ref[...]).astype(BF16)
    kr = _dot(h, wkr_ref[...])
    kr_rot = kr * cos + _swap_halves(kr) * sin
    for j in range(C_PAIRS):
        cols = slice(j * C_PAIR_LANES, (j + 1) * C_PAIR_LANES)
        kc_ref[:, cols] = (_dot(ckvn, wuk_ref[:, cols]) + kr_rot).astype(BF16)
    vct_ref[...] = _dot(ckvn, wuv_ref[...]).T.astype(BF16)


def _input_stage(x2, g, wab, wcq, gq, wuq, wckv, gkv, wuk, wuv, wkr, cos_t, sin_t, seq):
    n = x2.shape[0]
    tm = ROW_TILE
    pos_blocks = seq // tm
    row = lambda i: (i, 0)
    col = lambda i: (0, i)
    return pl.pallas_call(
        _input_kernel,
        grid=(n // tm,),
        in_specs=[
            pl.BlockSpec((tm, D_MODEL), row),
            _const_spec((1, D_MODEL)),
            _const_spec((D_MODEL, AB_COLS)),
            _const_spec((D_MODEL, C_Q_RANK)),
            _const_spec((1, C_Q_RANK)),
            _const_spec((C_Q_RANK, C_COLS)),
            _const_spec((D_MODEL, C_KV_RANK)),
            _const_spec((1, C_KV_RANK)),
            _const_spec((C_KV_RANK, C_COLS)),
            _const_spec((C_KV_RANK, C_WIDTH)),
            _const_spec((D_MODEL, C_PAIR_LANES)),
            pl.BlockSpec((tm, C_PAIR_LANES), lambda i: (i % pos_blocks, 0)),
            pl.BlockSpec((tm, C_PAIR_LANES), lambda i: (i % pos_blocks, 0)),
        ],
        out_specs=[
            pl.BlockSpec((tm, AB_COLS), row),
            pl.BlockSpec((B_WIDTH, tm), col),
            pl.BlockSpec((B_WIDTH, tm), col),
            pl.BlockSpec((C_COLS, tm), col),
            pl.BlockSpec((tm, C_COLS), row),
            pl.BlockSpec((C_WIDTH, tm), col),
        ],
        out_shape=[
            jax.ShapeDtypeStruct((n, AB_COLS), BF16),
            jax.ShapeDtypeStruct((B_WIDTH, n), BF16),
            jax.ShapeDtypeStruct((B_WIDTH, n), BF16),
            jax.ShapeDtypeStruct((C_COLS, n), BF16),
            jax.ShapeDtypeStruct((n, C_COLS), BF16),
            jax.ShapeDtypeStruct((C_WIDTH, n), BF16),
        ],
        compiler_params=_params(1),
        name="input_stage",
    )(x2, g, wab, wcq, gq, wuq, wckv, gkv, wuk, wuv, wkr, cos_t, sin_t)


def _dot_tn(a, b):
    return lax.dot_general(a, b, (((0,), (0,)), ((), ())), preferred_element_type=F32)


def _dilated_kernel(q_ref, k_ref, v_ref, bias_ref, y_ref, qf_ref, kf_ref, vf_ref, m_ref, w_ref, a_ref):
    seq = q_ref.shape[0]
    chunk = 512

    def widen(c, carry):
        rows = pl.ds(pl.multiple_of(c * chunk, chunk), chunk)
        qf_ref[rows, :] = q_ref[rows, :].astype(F32) * (HEAD_DIM ** -0.5)
        kf_ref[rows, :] = k_ref[rows, :].astype(F32)
        vf_ref[rows, :] = v_ref[rows, :].astype(F32)
        return carry

    lax.fori_loop(0, seq // chunk, widen, 0)

    tile = (A_BLOCK, A_BLOCK)
    low_lane = lax.broadcasted_iota(jnp.int32, tile, 1) < HEAD_DIM
    low_row = lax.broadcasted_iota(jnp.int32, tile, 0) < HEAD_DIM
    last = len(A_PATTERNS) - 1
    for g, (_, dil) in enumerate(A_PATTERNS):
        nb = seq // dil // A_BLOCK
        span = dil * A_BLOCK

        def attend(blk, g=g, dil=dil, nb=nb, span=span):
            r = blk // nb
            n = blk % nb
            base = r + n * span
            prev_base = jnp.maximum(base - span, r)
            rows = pl.ds(base, A_BLOCK, stride=dil)
            prev_rows = pl.ds(prev_base, A_BLOCK, stride=dil)
            q = qf_ref[rows, :]
            k2 = jnp.concatenate([kf_ref[prev_rows, :], kf_ref[rows, :]], axis=0).astype(BF16)
            v2 = jnp.concatenate([vf_ref[prev_rows, :], vf_ref[rows, :]], axis=0).astype(BF16)
            first_block = jnp.where(n == 0, 1, 0)
            outs, lses = [], []
            for hh in range(2):
                qm = jnp.where(low_lane if hh == 0 else ~low_lane, q, 0.0).astype(BF16)
                s = _dot_nt(k2, qm) + bias_ref[0, g, first_block, hh]
                m = jnp.max(s, axis=0, keepdims=True)
                p = jnp.exp(s - m)
                l = jnp.sum(p, axis=0, keepdims=True)
                outs.append(_dot_tn(v2, p.astype(BF16)) * (1.0 / l))
                lses.append(jnp.broadcast_to(m + jnp.log(l), tile))
            o_blk = jnp.where(low_row, outs[0], outs[1]).T
            lse_blk = jnp.where(low_row, lses[0], lses[1]).T
            return rows, o_blk, lse_blk

        def merge(rows, o_blk, lse_blk, g=g):
            if g == 0:
                m_ref[rows, :] = lse_blk
                a_ref[rows, :] = o_blk
                return
            m_old = m_ref[rows, :]
            m_new = jnp.maximum(m_old, lse_blk)
            c_old = jnp.exp(m_old - m_new)
            c_blk = jnp.exp(lse_blk - m_new)
            w_new = c_blk + (c_old if g == 1 else c_old * w_ref[rows, :])
            a_new = c_old * a_ref[rows, :] + c_blk * o_blk
            if g < last:
                m_ref[rows, :] = m_new
                w_ref[rows, :] = w_new
                a_ref[rows, :] = a_new
            else:
                a_ref[rows, :] = a_new / w_new

        def step(it, carry, attend=attend, merge=merge):
            done = [attend(it * A_BLOCKS_PER_STEP + u) for u in range(A_BLOCKS_PER_STEP)]
            for args in done:
                merge(*args)
            return carry

        lax.fori_loop(0, dil * nb // A_BLOCKS_PER_STEP, step, 0)

    def narrow(c, carry):
        rows = pl.ds(pl.multiple_of(c * chunk, chunk), chunk)
        y_ref[rows, :] = a_ref[rows, :].astype(BF16)
        return carry

    lax.fori_loop(0, seq // chunk, narrow, 0)


def _dilated_attention(qkv, bias_tiles, batch, seq):
    pairs = A_HEADS // 2
    col = lambda off: (lambda b, p: (b, off * pairs + p))
    state = pltpu.VMEM((seq, 128), F32)
    return pl.pallas_call(
        _dilated_kernel,
        grid=(batch, pairs),
        in_specs=[
            pl.BlockSpec((seq, 128), col(0)),
            pl.BlockSpec((seq, 128), col(1)),
            pl.BlockSpec((seq, 128), col(2)),
            pl.BlockSpec((1, len(A_PATTERNS), 2, 2, 2 * A_BLOCK, A_BLOCK), lambda b, p: (p, 0, 0, 0, 0, 0)),
        ],
        out_specs=pl.BlockSpec((seq, 128), lambda b, p: (b, p)),
        out_shape=jax.ShapeDtypeStruct((batch * seq, A_WIDTH), BF16),
        scratch_shapes=[state] * 6,
        compiler_params=_params(2),
        name="dilated_attention",
    )(qkv, qkv, qkv, bias_tiles)


def _flash_init(m_ref, acc_ref):
    m_ref[...] = jnp.full(m_ref.shape, NEG, F32)
    acc_ref[...] = jnp.zeros(acc_ref.shape, F32)


def _with_ones(v_t):
    return jnp.concatenate([v_t, jnp.ones((ONES_ROWS, v_t.shape[1]), v_t.dtype)], axis=0)


def _flash_update(s, v_ext, idx, m_ref, acc_ref):
    m_prev = m_ref[idx]
    m_new = jnp.maximum(m_prev, jnp.max(s, axis=0, keepdims=True))
    alpha = jnp.exp2(m_prev - m_new)
    p = jnp.exp2(s - m_new)
    acc_ref[idx] = alpha * acc_ref[idx] + _dot(v_ext, p.astype(BF16))
    m_ref[idx] = m_new


def _flash_result(idx, dv, acc_ref):
    acc = acc_ref[idx]
    return acc[:dv] * (1.0 / acc[dv:dv + 1])


def _flash_sweep(i, scores, absorb):
    scores(0, 0)

    def pair(j, carry):
        c = 2 * j
        scores(c + 1, 1)
        absorb(c, 0, False)
        scores(c + 2, 0)
        absorb(c + 1, 1, False)
        return carry

    lax.fori_loop(0, i // 2, pair, 0)

    @pl.when(i % 2 == 0)
    def _():
        absorb(i, 0, True)

    @pl.when(i % 2 == 1)
    def _():
        scores(i, 1)
        absorb(i - 1, 0, False)
        absorb(i, 1, True)


def _causal_mask_t(t):
    key = lax.broadcasted_iota(jnp.int32, (t, t), 0)
    query = lax.broadcasted_iota(jnp.int32, (t, t), 1)
    return key <= query


def _row_in_ranges(shape, ranges):
    row = lax.broadcasted_iota(jnp.int32, shape, 0)
    hit = None
    for lo, hi in ranges:
        r = (row >= lo) & (row < hi)
        hit = r if hit is None else hit | r
    return hit


def _diff_bias(bias_ref, d):
    sub = ATT_TILE // BIAS_TILE
    rows = []
    for ka in range(sub):
        tiles = []
        for qa in range(sub):
            e = sub * d + qa - ka
            e = min(max(e, 0), B_BIAS_TILES - 1) if isinstance(e, int) else jnp.clip(e, 0, B_BIAS_TILES - 1)
            tiles.append(bias_ref[0, e])
        rows.append(jnp.concatenate(tiles, axis=1))
    return jnp.concatenate(rows, axis=0)


def _diff_kernel(lam_init, qt_ref, k_ref, vt_ref, bias_ref, lam_ref, g_ref, y_ref,
                 m_ref, acc_ref, s_ref):
    i = pl.program_id(2)
    t = ATT_TILE
    qf = qt_ref[...].astype(F32)
    first = lax.broadcasted_iota(jnp.int32, qf.shape, 0) < B_QK_DIM
    q1 = jnp.where(first, qf, 0.0).astype(BF16)
    q2 = jnp.where(first, 0.0, qf).astype(BF16)
    _flash_init(m_ref, acc_ref)

    def scores(c, slot):
        k = k_ref[pl.ds(pl.multiple_of(c * t, t), t), :]
        bias = _diff_bias(bias_ref, i - c)
        for idx, qx in enumerate((q1, q2)):
            s_ref[slot, idx] = _dot(k, qx) + bias

    def absorb(c, slot, masked):
        v_ext = _with_ones(vt_ref[:, pl.ds(pl.multiple_of(c * t, t), t)])
        for idx in range(2):
            s = s_ref[slot, idx]
            if masked:
                s = jnp.where(_causal_mask_t(t), s, NEG)
            _flash_update(s, v_ext, idx, m_ref, acc_ref)

    _flash_sweep(i, scores, absorb)

    lv = lam_ref[...]
    lam = (jnp.exp(jnp.sum(lv[0:1] * lv[1:2], axis=-1, keepdims=True))
           - jnp.exp(jnp.sum(lv[2:3] * lv[3:4], axis=-1, keepdims=True)) + lam_init)
    o = _flash_result(0, B_V_DIM, acc_ref) - lam * _flash_result(1, B_V_DIM, acc_ref)
    y = o * lax.rsqrt(jnp.mean(o * o, axis=0, keepdims=True) + EPS) * g_ref[...] * (1.0 - lam_init)
    y_ref[...] = y.T.astype(BF16)


def _diff_attention(qb_t, qkv, vb_t, bias_tiles, lam_vecs, g_col, batch, seq, lam_init):
    t = ATT_TILE
    nq = seq // t
    k_col0 = KB_COL0 // 128
    return pl.pallas_call(
        functools.partial(_diff_kernel, lam_init),
        grid=(batch, B_HEADS, nq),
        in_specs=[
            pl.BlockSpec((128, t), lambda b, h, i: (h, b * nq + i)),
            pl.BlockSpec((seq, 128), lambda b, h, i: (b, k_col0 + h)),
            pl.BlockSpec((128, seq), lambda b, h, i: (h, b)),
            pl.BlockSpec((1, B_BIAS_TILES, BIAS_TILE, BIAS_TILE), lambda b, h, i: (h, 0, 0, 0)),
            _const_spec((4, B_QK_DIM)),
            _const_spec((B_V_DIM, 1)),
        ],
        out_specs=pl.BlockSpec((t, 128), lambda b, h, i: (b * nq + i, h)),
        out_shape=jax.ShapeDtypeStruct((batch * seq, B_WIDTH), BF16),
        scratch_shapes=[
            pltpu.VMEM((2, 1, t), F32),
            pltpu.VMEM((2, B_V_DIM + ONES_ROWS, t), F32),
            pltpu.VMEM((2, 2, t, t), F32),
        ],
        compiler_params=_params(3),
        name="diff_attention",
    )(qb_t, qkv, vb_t, bias_tiles, lam_vecs, g_col)


def _mla_kernel(qt_ref, k_ref, vt_ref, y_ref, m_ref, acc_ref, s_ref):
    i = pl.program_id(2)
    t = ATT_TILE
    qf = qt_ref[...].astype(F32)
    qa = jnp.where(_row_in_ranges(qf.shape, C_HEAD_A_RANGES), qf, 0.0).astype(BF16)
    qb = jnp.where(_row_in_ranges(qf.shape, C_HEAD_B_RANGES), qf, 0.0).astype(BF16)
    _flash_init(m_ref, acc_ref)

    def scores(c, slot):
        k = k_ref[pl.ds(pl.multiple_of(c * t, t), t), :]
        for idx, qx in enumerate((qa, qb)):
            s_ref[slot, idx] = _dot(k, qx)

    def absorb(c, slot, masked):
        v_t = vt_ref[:, pl.ds(pl.multiple_of(c * t, t), t)]
        for idx in range(2):
            s = s_ref[slot, idx]
            if masked:
                s = jnp.where(_causal_mask_t(t), s, NEG)
            _flash_update(s, _with_ones(v_t[idx * C_V:(idx + 1) * C_V]), idx, m_ref, acc_ref)

    _flash_sweep(i, scores, absorb)

    o = jnp.concatenate([_flash_result(0, C_V, acc_ref), _flash_result(1, C_V, acc_ref)], axis=0)
    y_ref[...] = o.T.astype(BF16)


def _mla_attention(qc_t, kc, vc_t, batch, seq):
    t = ATT_TILE
    nq = seq // t
    return pl.pallas_call(
        _mla_kernel,
        grid=(batch, C_PAIRS, nq),
        in_specs=[
            pl.BlockSpec((C_PAIR_LANES, t), lambda b, j, i: (j, b * nq + i)),
            pl.BlockSpec((seq, C_PAIR_LANES), lambda b, j, i: (b, j)),
            pl.BlockSpec((2 * C_V, seq), lambda b, j, i: (j, b)),
        ],
        out_specs=pl.BlockSpec((t, 2 * C_V), lambda b, j, i: (b * nq + i, j)),
        out_shape=jax.ShapeDtypeStruct((batch * seq, C_WIDTH), BF16),
        scratch_shapes=[
            pltpu.VMEM((2, 1, t), F32),
            pltpu.VMEM((2, C_V + ONES_ROWS, t), F32),
            pltpu.VMEM((2, 2, t, t), F32),
        ],
        compiler_params=_params(3),
        name="mla_attention",
    )(qc_t, kc, vc_t)


def _sigmoid(z):
    return 1.0 / (1.0 + jnp.exp(-z))


def _merge_kernel(x_ref, g_ref, ya_ref, yb_ref, yc_ref, wg_ref, bg_ref, wa_ref, wb_ref, wc_ref,
                  wo_ref, out_ref):
    x = x_ref[...]
    h = _rms(x, g_ref[...]).astype(BF16)
    merged = jnp.zeros(x.shape, F32)
    for k, (y_ref, w_ref) in enumerate(((ya_ref, wa_ref), (yb_ref, wb_ref), (yc_ref, wc_ref))):
        cols = slice(k * D_MODEL, (k + 1) * D_MODEL)
        gate = _sigmoid(_dot(h, wg_ref[:, cols]) + bg_ref[:, cols])
        merged = merged + gate * _dot(y_ref[...], w_ref[...])
    out_ref[...] = x + _dot(merged.astype(BF16), wo_ref[...])


def _merge_stage(x2, g, ya, yb, yc, wg, bg, wa, wb, wc, wo):
    n = x2.shape[0]
    tm = ROW_TILE
    row = lambda i: (i, 0)
    return pl.pallas_call(
        _merge_kernel,
        grid=(n // tm,),
        in_specs=[
            pl.BlockSpec((tm, D_MODEL), row),
            _const_spec((1, D_MODEL)),
            pl.BlockSpec((tm, A_WIDTH), row),
            pl.BlockSpec((tm, B_WIDTH), row),
            pl.BlockSpec((tm, C_WIDTH), row),
            _const_spec((D_MODEL, N_BRANCH * D_MODEL)),
            _const_spec((1, N_BRANCH * D_MODEL)),
            _const_spec((A_WIDTH, D_MODEL)),
            _const_spec((B_WIDTH, D_MODEL)),
            _const_spec((C_WIDTH, D_MODEL)),
            _const_spec((D_MODEL, D_MODEL)),
        ],
        out_specs=pl.BlockSpec((tm, D_MODEL), row),
        out_shape=jax.ShapeDtypeStruct((n, D_MODEL), F32),
        compiler_params=_params(1),
        name="gated_merge",
    )(x2, g, ya, yb, yc, wg, bg, wa, wb, wc, wo)


def _ffn_kernel(final_norm, x_ref, g_ref, wg_ref, wu_ref, wd_ref, gf_ref, out_ref):
    x = x_ref[...]
    h = _rms(x, g_ref[...]).astype(BF16)
    acc = jnp.zeros(x.shape, F32)
    for c in range(0, FFN_HIDDEN, FFN_CHUNK):
        cols = slice(c, c + FFN_CHUNK)
        gate = _dot(h, wg_ref[:, cols])
        act = gate * _sigmoid(gate) * _dot(h, wu_ref[:, cols])
        acc = acc + _dot(act.astype(BF16), wd_ref[cols, :])
    y = x + acc
    if final_norm:
        y = _rms(y, gf_ref[...])
    out_ref[...] = y


def _ffn_stage(x2, g, wg, wu, wd, gf, final_norm):
    n = x2.shape[0]
    tm = ROW_TILE
    row = lambda i: (i, 0)
    return pl.pallas_call(
        functools.partial(_ffn_kernel, final_norm),
        grid=(n // tm,),
        in_specs=[
            pl.BlockSpec((tm, D_MODEL), row),
            _const_spec((1, D_MODEL)),
            _const_spec((D_MODEL, FFN_HIDDEN)),
            _const_spec((D_MODEL, FFN_HIDDEN)),
            _const_spec((FFN_HIDDEN, D_MODEL)),
            _const_spec((1, D_MODEL)),
        ],
        out_specs=pl.BlockSpec((tm, D_MODEL), row),
        out_shape=jax.ShapeDtypeStruct((n, D_MODEL), F32),
        compiler_params=_params(1),
        name="swiglu_final" if final_norm else "swiglu",
    )(x2, g, wg, wu, wd, gf)


def _gather_cols(w, idx):
    padded = jnp.concatenate([w, jnp.zeros((w.shape[0], 1), w.dtype)], axis=1)
    return padded[:, idx]


def kernel(x, rel_bias_table, ln_mix_g, w_in, lambda_q1, lambda_k1, lambda_q2, lambda_k2, diff_subln_g, mla_q_norm_g, w_uq, mla_kv_norm_g, w_ukv, w_gate, b_gate, w_br_a, w_br_b, w_br_c, w_o, ln_ffn_g, w_ffn_gate, w_ffn_up, w_ffn_down, final_norm_g):
    batch, seq, _ = x.shape
    depth = w_in.shape[0]
    assert seq % (A_BLOCK * A_PATTERNS[-1][1]) == 0 and seq % ATT_TILE == 0 and seq % ROW_TILE == 0
    lay = _C_LAYOUT
    cos_t, sin_t = _rope_tables(seq)

    a_tiles = _bias_expand(rel_bias_table, _a_bucket_index(), A_HEADS, 0, 2 * A_BLOCK, "bias_expand_a")
    a_tiles = a_tiles.reshape(A_HEADS // 2, 2, len(A_PATTERNS), 2, 2 * A_BLOCK, A_BLOCK)
    a_tiles = jnp.transpose(a_tiles, (0, 2, 3, 1, 4, 5))
    b_tiles = _bias_expand(rel_bias_table, _b_bucket_index(), B_HEADS, A_HEADS, BIAS_TILE,
                           "bias_expand_b", gain=LOG2E).reshape(B_HEADS, B_BIAS_TILES, BIAS_TILE, BIAS_TILE)

    cq0 = AB_COLS
    ckv0 = cq0 + C_Q_RANK
    kr0 = ckv0 + C_KV_RANK
    x2 = x.reshape(batch * seq, D_MODEL)
    for l in range(depth):
        w_in_l = w_in[l]
        qkv, qb_t, vb_t, qc_t, kc, vc_t = _input_stage(
            x2, ln_mix_g[l][None, :],
            w_in_l[:, :AB_COLS].astype(BF16),
            w_in_l[:, cq0:ckv0].astype(BF16), mla_q_norm_g[l][None, :],
            _gather_cols(w_uq[l], lay["q"]).astype(BF16),
            w_in_l[:, ckv0:kr0].astype(BF16), mla_kv_norm_g[l][None, :],
            _gather_cols(w_ukv[l], lay["k"]).astype(BF16),
            w_ukv[l][:, lay["v"]].astype(BF16),
            _gather_cols(w_in_l[:, kr0:], lay["kr"]).astype(BF16),
            cos_t, sin_t, seq)

        ya = _dilated_attention(qkv, a_tiles, batch, seq)

        lam_init = 0.8 - 0.6 * math.exp(-0.3 * l)
        lam_vecs = jnp.stack([lambda_q1[l], lambda_k1[l], lambda_q2[l], lambda_k2[l]]).astype(F32)
        yb = _diff_attention(qb_t, qkv, vb_t, b_tiles, lam_vecs, diff_subln_g[l][:, None],
                             batch, seq, lam_init)
        yc = _mla_attention(qc_t, kc, vc_t, batch, seq)

        x2 = _merge_stage(x2, ln_mix_g[l][None, :], ya, yb, yc,
                          w_gate[l].astype(BF16), b_gate[l][None, :],
                          w_br_a[l].astype(BF16), w_br_b[l].astype(BF16), w_br_c[l].astype(BF16),
                          w_o[l].astype(BF16))
        x2 = _ffn_stage(x2, ln_ffn_g[l][None, :], w_ffn_gate[l].astype(BF16),
                        w_ffn_up[l].astype(BF16), w_ffn_down[l].astype(BF16),
                        final_norm_g[None, :], final_norm=(l == depth - 1))
    return x2.reshape(batch, seq, D_MODEL)
```

```python
import functools
import math

import numpy as np
import jax
import jax.numpy as jnp
from jax import lax
from jax.experimental import pallas as pl
from jax.experimental.pallas import tpu as pltpu

D_MODEL = 1024
HEAD_DIM = 64
A_HEADS = 8
A_PATTERNS = ((2048, 16), (512, 4), (128, 1))
A_WIDTH = A_HEADS * HEAD_DIM
A_BLOCK = 128
A_BLOCKS_PER_STEP = 8
B_HEADS = 4
B_QK_DIM = HEAD_DIM
B_V_DIM = 2 * HEAD_DIM
B_WIDTH = B_HEADS * B_V_DIM
C_HEADS = 8
C_PAIRS = C_HEADS // 2
C_Q_RANK = 768
C_KV_RANK = 256
C_NOPE = 64
C_ROPE = 32
C_V = 64
C_WIDTH = C_HEADS * C_V
C_PAIR_LANES = 256
C_COLS = C_PAIRS * C_PAIR_LANES
ROPE_THETA = 10000.0
REL_BUCKETS = 32
REL_MAX_DIST = 2048
AB_COLS = 3 * A_WIDTH + 4 * B_HEADS * B_QK_DIM + B_WIDTH
QB_COL0 = 3 * A_WIDTH
KB_COL0 = QB_COL0 + 2 * B_HEADS * B_QK_DIM
VB_COL0 = KB_COL0 + 2 * B_HEADS * B_QK_DIM
N_BRANCH = 3
FFN_HIDDEN = 2816
EPS = 1e-6
NEG = -1e30
LOG2E = math.log2(math.e)
B_SCORE_SCALE = B_QK_DIM ** -0.5 * LOG2E
C_SCORE_SCALE = (C_NOPE + C_ROPE) ** -0.5 * LOG2E
ONES_ROWS = 16

ROW_TILE = 512
ATT_TILE = 512
BIAS_TILE = 256
B_BIAS_TILES = 8
FFN_CHUNK = 256
VMEM_LIMIT = 56 * 1024 * 1024

F32 = jnp.float32
BF16 = jnp.bfloat16


def _dot(a, b):
    return jnp.dot(a, b, preferred_element_type=F32)


def _dot_nt(a, b):
    return lax.dot_general(a, b, (((1,), (1,)), ((), ())), preferred_element_type=F32)


def _rms(x, g):
    return x * lax.rsqrt(jnp.mean(x * x, axis=-1, keepdims=True) + EPS) * g


def _swap_halves(t):
    half = t.shape[1] // 2
    return jnp.concatenate([t[:, half:], t[:, :half]], axis=1)


def _const_spec(shape):
    nd = len(shape)
    return pl.BlockSpec(shape, lambda *_: (0,) * nd, pipeline_mode=pl.Buffered(1))


def _params(n_axes):
    return pltpu.CompilerParams(dimension_semantics=("parallel",) * n_axes,
                                vmem_limit_bytes=VMEM_LIMIT)


def _t5_bucket_np(dist):
    dist = np.maximum(dist, 0)
    exact = REL_BUCKETS // 2
    ratio = np.maximum(dist, 1).astype(np.float32) / np.float32(exact)
    log_ratio = np.log(ratio).astype(np.float32) / np.float32(math.log(REL_MAX_DIST / exact))
    large = np.minimum(exact + (log_ratio * np.float32(REL_BUCKETS - exact)).astype(np.int32),
                       REL_BUCKETS - 1)
    return np.where(dist < exact, dist, large).astype(np.int32)


MASKED_BUCKET = REL_BUCKETS


def _a_bucket_index():
    kj = np.arange(2 * A_BLOCK)[:, None]
    qi = np.arange(A_BLOCK)[None, :]
    step = qi + A_BLOCK - kj
    in_band = (step >= 0) & (step <= A_BLOCK)
    tiles = []
    for _, dil in A_PATTERNS:
        bucket = _t5_bucket_np(step * dil)
        for first_block in (False, True):
            valid = in_band & (kj >= A_BLOCK) if first_block else in_band
            tiles.append(np.where(valid, bucket, MASKED_BUCKET))
    return np.concatenate(tiles, axis=0).astype(np.int32)


def _b_bucket_index():
    kj = np.arange(BIAS_TILE)[:, None]
    qi = np.arange(BIAS_TILE)[None, :]
    e = np.arange(B_BIAS_TILES)[:, None, None]
    idx = _t5_bucket_np(e * BIAS_TILE + qi - kj)
    assert (idx[-1] == REL_BUCKETS - 1).all()
    return idx.reshape(B_BIAS_TILES * BIAS_TILE, BIAS_TILE)


def _c_layout():
    hq = C_NOPE + C_ROPE
    q_zero = C_HEADS * hq
    kv_zero = C_HEADS * (C_NOPE + C_V)
    half = C_ROPE // 2
    q_idx = np.full((C_PAIRS, C_PAIR_LANES), q_zero, np.int32)
    k_idx = np.full((C_PAIRS, C_PAIR_LANES), kv_zero, np.int32)
    for j in range(C_PAIRS):
        a, b = 2 * j, 2 * j + 1
        q_idx[j, 0:64] = a * hq + np.arange(64)
        q_idx[j, 64:80] = a * hq + C_NOPE + np.arange(half)
        q_idx[j, 80:96] = b * hq + C_NOPE + np.arange(half)
        q_idx[j, 128:192] = b * hq + np.arange(64)
        q_idx[j, 192:208] = a * hq + C_NOPE + half + np.arange(half)
        q_idx[j, 208:224] = b * hq + C_NOPE + half + np.arange(half)
        k_idx[j, 0:64] = a * (C_NOPE + C_V) + np.arange(64)
        k_idx[j, 128:192] = b * (C_NOPE + C_V) + np.arange(64)
    v_idx = (np.arange(C_HEADS)[:, None] * (C_NOPE + C_V) + C_NOPE + np.arange(C_V)[None, :])
    kr_idx = np.full((C_PAIR_LANES,), C_ROPE, np.int32)
    kr_idx[64:80] = np.arange(half)
    kr_idx[80:96] = np.arange(half)
    kr_idx[192:208] = half + np.arange(half)
    kr_idx[208:224] = half + np.arange(half)
    rope_lane = np.full((C_PAIR_LANES,), -1, np.int32)
    for start in (64, 80, 192, 208):
        rope_lane[start:start + half] = np.arange(half)
    sin_sign = np.zeros((C_PAIR_LANES,), np.float32)
    sin_sign[64:96] = -1.0
    sin_sign[192:224] = 1.0
    return dict(q=q_idx.reshape(-1), k=k_idx.reshape(-1), v=v_idx.reshape(-1), kr=kr_idx,
                rope_lane=rope_lane, sin_sign=sin_sign)


_C_LAYOUT = _c_layout()
C_HEAD_A_RANGES = ((0, 80), (192, 208))
C_HEAD_B_RANGES = ((80, 96), (128, 192), (208, 224))


def _rope_tables(seq):
    pos = jnp.arange(seq, dtype=F32)
    inv_freq = ROPE_THETA ** (-jnp.arange(0, C_ROPE, 2, dtype=F32) / C_ROPE)
    ang = pos[:, None] * inv_freq[None, :]
    cos, sin = jnp.cos(ang), jnp.sin(ang)
    lane = _C_LAYOUT["rope_lane"]
    is_rope = jnp.asarray(lane >= 0)
    gather = np.maximum(lane, 0)
    cos_t = jnp.where(is_rope[None, :], cos[:, gather], 1.0)
    sin_t = sin[:, gather] * jnp.asarray(_C_LAYOUT["sin_sign"])[None, :]
    return cos_t, sin_t


def _bias_expand_kernel(n_heads, head0, gain, table_ref, idx_ref, out_ref):
    idx = idx_ref[...]
    hit = [idx == b for b in range(1, REL_BUCKETS)]
    masked = idx == MASKED_BUCKET
    for h in range(n_heads):
        acc = jnp.full(idx.shape, table_ref[0, head0 + h] * gain, F32)
        for b in range(1, REL_BUCKETS):
            acc = jnp.where(hit[b - 1], table_ref[b, head0 + h] * gain, acc)
        out_ref[h] = jnp.where(masked, NEG, acc)


def _bias_expand(table, idx, n_heads, head0, block_rows, name, gain=1.0):
    rows, cols = idx.shape
    return pl.pallas_call(
        functools.partial(_bias_expand_kernel, n_heads, head0, gain),
        grid=(rows // block_rows,),
        in_specs=[
            pl.BlockSpec(memory_space=pltpu.SMEM),
            pl.BlockSpec((block_rows, cols), lambda i: (i, 0)),
        ],
        out_specs=pl.BlockSpec((n_heads, block_rows, cols), lambda i: (0, i, 0)),
        out_shape=jax.ShapeDtypeStruct((n_heads, rows, cols), F32),
        compiler_params=_params(1),
        name=name,
    )(table, jnp.asarray(idx))


def _input_kernel(x_ref, g_ref, wab_ref, wcq_ref, gq_ref, wuq_ref, wckv_ref, gkv_ref, wuk_ref,
                  wuv_ref, wkr_ref, cos_ref, sin_ref,
                  qkv_ref, qbt_ref, vbt_ref, qct_ref, kc_ref, vct_ref):
    h = _rms(x_ref[...], g_ref[...]).astype(BF16)
    for c in range(0, AB_COLS, 512):
        t = _dot(h, wab_ref[:, c:c + 512])
        qkv_ref[:, c:c + 512] = t.astype(BF16)
        if c == QB_COL0:
            qbt_ref[...] = (t * B_SCORE_SCALE).T.astype(BF16)
        if c == VB_COL0:
            vbt_ref[...] = t.T.astype(BF16)
    cos = cos_ref[...]
    sin = sin_ref[...]
    cqn = _rms(_dot(h, wcq_ref[...]), gq_ref[...]).astype(BF16)
    for j in range(C_PAIRS):
        cols = slice(j * C_PAIR_LANES, (j + 1) * C_PAIR_LANES)
        t = _dot(cqn, wuq_ref[:, cols])
        qct_ref[cols, :] = ((t * cos + _swap_halves(t) * sin) * C_SCORE_SCALE).T.astype(BF16)
    ckvn = _rms(_dot(h, wckv_ref[...]), gkv_ref[...]).astype(BF16)
    kr = _dot(h, wkr_ref[...])
    kr_rot = kr * cos + _swap_halves(kr) * sin
    for j in range(C_PAIRS):
        cols = slice(j * C_PAIR_LANES, (j + 1) * C_PAIR_LANES)
        kc_ref[:, cols] = (_dot(ckvn, wuk_ref[:, cols]) + kr_rot).astype(BF16)
    vct_ref[...] = _dot(ckvn, wuv_ref[...]).T.astype(BF16)


def _input_stage(x2, g, wab, wcq, gq, wuq, wckv, gkv, wuk, wuv, wkr, cos_t, sin_t, seq):
    n = x2.shape[0]
    tm = ROW_TILE
    pos_blocks = seq // tm
    row = lambda i: (i, 0)
    col = lambda i: (0, i)
    return pl.pallas_call(
        _input_kernel,
        grid=(n // tm,),
        in_specs=[
            pl.BlockSpec((tm, D_MODEL), row),
            _const_spec((1, D_MODEL)),
            _const_spec((D_MODEL, AB_COLS)),
            _const_spec((D_MODEL, C_Q_RANK)),
            _const_spec((1, C_Q_RANK)),
            _const_spec((C_Q_RANK, C_COLS)),
            _const_spec((D_MODEL, C_KV_RANK)),
            _const_spec((1, C_KV_RANK)),
            _const_spec((C_KV_RANK, C_COLS)),
            _const_spec((C_KV_RANK, C_WIDTH)),
            _const_spec((D_MODEL, C_PAIR_LANES)),
            pl.BlockSpec((tm, C_PAIR_LANES), lambda i: (i % pos_blocks, 0)),
            pl.BlockSpec((tm, C_PAIR_LANES), lambda i: (i % pos_blocks, 0)),
        ],
        out_specs=[
            pl.BlockSpec((tm, AB_COLS), row),
            pl.BlockSpec((B_WIDTH, tm), col),
            pl.BlockSpec((B_WIDTH, tm), col),
            pl.BlockSpec((C_COLS, tm), col),
            pl.BlockSpec((tm, C_COLS), row),
            pl.BlockSpec((C_WIDTH, tm), col),
        ],
        out_shape=[
            jax.ShapeDtypeStruct((n, AB_COLS), BF16),
            jax.ShapeDtypeStruct((B_WIDTH, n), BF16),
            jax.ShapeDtypeStruct((B_WIDTH, n), BF16),
            jax.ShapeDtypeStruct((C_COLS, n), BF16),
            jax.ShapeDtypeStruct((n, C_COLS), BF16),
            jax.ShapeDtypeStruct((C_WIDTH, n), BF16),
        ],
        compiler_params=_params(1),
        name="input_stage",
    )(x2, g, wab, wcq, gq, wuq, wckv, gkv, wuk, wuv, wkr, cos_t, sin_t)


def _dot_tn(a, b):
    return lax.dot_general(a, b, (((0,), (0,)), ((), ())), preferred_element_type=F32)


def _dilated_kernel(q_ref, k_ref, v_ref, bias_ref, y_ref, qf_ref, kf_ref, vf_ref, m_ref, w_ref, a_ref):
    seq = q_ref.shape[0]
    chunk = 512

    def widen(c, carry):
        rows = pl.ds(pl.multiple_of(c * chunk, chunk), chunk)
        qf_ref[rows, :] = q_ref[rows, :].astype(F32) * (HEAD_DIM ** -0.5)
        kf_ref[rows, :] = k_ref[rows, :].astype(F32)
        vf_ref[rows, :] = v_ref[rows, :].astype(F32)
        return carry

    lax.fori_loop(0, seq // chunk, widen, 0)

    tile = (A_BLOCK, A_BLOCK)
    low_lane = lax.broadcasted_iota(jnp.int32, tile, 1) < HEAD_DIM
    low_row = lax.broadcasted_iota(jnp.int32, tile, 0) < HEAD_DIM
    last = len(A_PATTERNS) - 1
    for g, (_, dil) in enumerate(A_PATTERNS):
        nb = seq // dil // A_BLOCK
        span = dil * A_BLOCK

        def attend(blk, g=g, dil=dil, nb=nb, span=span):
            r = blk // nb
            n = blk % nb
            base = r + n * span
            prev_base = jnp.maximum(base - span, r)
            rows = pl.ds(base, A_BLOCK, stride=dil)
            prev_rows = pl.ds(prev_base, A_BLOCK, stride=dil)
            q = qf_ref[rows, :]
            k2 = jnp.concatenate([kf_ref[prev_rows, :], kf_ref[rows, :]], axis=0).astype(BF16)
            v2 = jnp.concatenate([vf_ref[prev_rows, :], vf_ref[rows, :]], axis=0).astype(BF16)
            first_block = jnp.where(n == 0, 1, 0)
            outs, lses = [], []
            for hh in range(2):
                qm = jnp.where(low_lane if hh == 0 else ~low_lane, q, 0.0).astype(BF16)
                s = _dot_nt(k2, qm) + bias_ref[0, g, first_block, hh]
                m = jnp.max(s, axis=0, keepdims=True)
                p = jnp.exp(s - m)
                l = jnp.sum(p, axis=0, keepdims=True)
                outs.append(_dot_tn(v2, p.astype(BF16)) * (1.0 / l))
                lses.append(jnp.broadcast_to(m + jnp.log(l), tile))
            o_blk = jnp.where(low_row, outs[0], outs[1]).T
            lse_blk = jnp.where(low_row, lses[0], lses[1]).T
            return rows, o_blk, lse_blk

        def merge(rows, o_blk, lse_blk, g=g):
            if g == 0:
                m_ref[rows, :] = lse_blk
                a_ref[rows, :] = o_blk
                return
            m_old = m_ref[rows, :]
            m_new = jnp.maximum(m_old, lse_blk)
            c_old = jnp.exp(m_old - m_new)
            c_blk = jnp.exp(lse_blk - m_new)
            w_new = c_blk + (c_old if g == 1 else c_old * w_ref[rows, :])
            a_new = c_old * a_ref[rows, :] + c_blk * o_blk
            if g < last:
                m_ref[rows, :] = m_new
                w_ref[rows, :] = w_new
                a_ref[rows, :] = a_new
            else:
                a_ref[rows, :] = a_new / w_new

        def step(it, carry, attend=attend, merge=merge):
            done = [attend(it * A_BLOCKS_PER_STEP + u) for u in range(A_BLOCKS_PER_STEP)]
            for args in done:
                merge(*args)
            return carry

        lax.fori_loop(0, dil * nb // A_BLOCKS_PER_STEP, step, 0)

    def narrow(c, carry):
        rows = pl.ds(pl.multiple_of(c * chunk, chunk), chunk)
        y_ref[rows, :] = a_ref[rows, :].astype(BF16)
        return carry

    lax.fori_loop(0, seq // chunk, narrow, 0)


def _dilated_attention(qkv, bias_tiles, batch, seq):
    pairs = A_HEADS // 2
    col = lambda off: (lambda b, p: (b, off * pairs + p))
    state = pltpu.VMEM((seq, 128), F32)
    return pl.pallas_call(
        _dilated_kernel,
        grid=(batch, pairs),
        in_specs=[
            pl.BlockSpec((seq, 128), col(0)),
            pl.BlockSpec((seq, 128), col(1)),
            pl.BlockSpec((seq, 128), col(2)),
            pl.BlockSpec((1, len(A_PATTERNS), 2, 2, 2 * A_BLOCK, A_BLOCK), lambda b, p: (p, 0, 0, 0, 0, 0)),
        ],
        out_specs=pl.BlockSpec((seq, 128), lambda b, p: (b, p)),
        out_shape=jax.ShapeDtypeStruct((batch * seq, A_WIDTH), BF16),
        scratch_shapes=[state] * 6,
        compiler_params=_params(2),
        name="dilated_attention",
    )(qkv, qkv, qkv, bias_tiles)


def _flash_init(m_ref, acc_ref):
    m_ref[...] = jnp.full(m_ref.shape, NEG, F32)
    acc_ref[...] = jnp.zeros(acc_ref.shape, F32)


def _with_ones(v_t):
    return jnp.concatenate([v_t, jnp.ones((ONES_ROWS, v_t.shape[1]), v_t.dtype)], axis=0)


def _flash_update(s, v_ext, idx, m_ref, acc_ref):
    m_prev = m_ref[idx]
    m_new = jnp.maximum(m_prev, jnp.max(s, axis=0, keepdims=True))
    alpha = jnp.exp2(m_prev - m_new)
    p = jnp.exp2(s - m_new)
    acc_ref[idx] = alpha * acc_ref[idx] + _dot(v_ext, p.astype(BF16))
    m_ref[idx] = m_new


def _flash_result(idx, dv, acc_ref):
    acc = acc_ref[idx]
    return acc[:dv] * (1.0 / acc[dv:dv + 1])


def _flash_sweep(n_tiles, scores, absorb, finish):
    n_items = n_tiles * (n_tiles + 1) // 2
    assert n_items % 2 == 0

    def following(i, c):
        end = c == i
        return jnp.where(end, i + 1, i), jnp.where(end, 0, c + 1)

    scores(0, 0, 0)

    def two_items(u, carry):
        i0, c0 = carry
        i1, c1 = following(i0, c0)
        i2, c2 = following(i1, c1)
        i2s = jnp.minimum(i2, n_tiles - 1)

        def variant(diag0, diag1):
            def run():
                scores(i1, c1, 1)
                absorb(c0, 0, diag0)
                if diag0:
                    finish(i0)
                scores(i2s, c2, 0)
                absorb(c1, 1, diag1)
                if diag1:
                    finish(i1)
            return run

        kind = 2 * (c0 == i0).astype(jnp.int32) + (c1 == i1).astype(jnp.int32)
        lax.switch(kind, [variant(False, False), variant(False, True),
                          variant(True, False), variant(True, True)])
        return i2, c2

    lax.fori_loop(0, n_items // 2, two_items, (jnp.int32(0), jnp.int32(0)))


def _causal_mask_t(t):
    key = lax.broadcasted_iota(jnp.int32, (t, t), 0)
    query = lax.broadcasted_iota(jnp.int32, (t, t), 1)
    return key <= query


def _row_in_ranges(shape, ranges):
    row = lax.broadcasted_iota(jnp.int32, shape, 0)
    hit = None
    for lo, hi in ranges:
        r = (row >= lo) & (row < hi)
        hit = r if hit is None else hit | r
    return hit


def _diff_bias(bias_ref, d):
    sub = ATT_TILE // BIAS_TILE
    rows = []
    for ka in range(sub):
        tiles = []
        for qa in range(sub):
            e = sub * d + qa - ka
            e = min(max(e, 0), B_BIAS_TILES - 1) if isinstance(e, int) else jnp.clip(e, 0, B_BIAS_TILES - 1)
            tiles.append(bias_ref[0, e])
        rows.append(jnp.concatenate(tiles, axis=1))
    return jnp.concatenate(rows, axis=0)


def _tile_cols(i):
    return pl.ds(pl.multiple_of(i * ATT_TILE, ATT_TILE), ATT_TILE)


def _diff_kernel(lam_init, qt_ref, k_ref, vt_ref, bias_ref, lam_ref, g_ref, y_ref,
                 qm_ref, m_ref, acc_ref, s_ref):
    t = ATT_TILE
    n_tiles = qt_ref.shape[1] // t

    def split_queries(i, carry):
        qf = qt_ref[:, _tile_cols(i)].astype(F32)
        first = lax.broadcasted_iota(jnp.int32, qf.shape, 0) < B_QK_DIM
        qm_ref[0, :, _tile_cols(i)] = jnp.where(first, qf, 0.0).astype(BF16)
        qm_ref[1, :, _tile_cols(i)] = jnp.where(first, 0.0, qf).astype(BF16)
        return carry

    lax.fori_loop(0, n_tiles, split_queries, 0)
    lv = lam_ref[...]
    lam = (jnp.exp(jnp.sum(lv[0:1] * lv[1:2], axis=-1, keepdims=True))
           - jnp.exp(jnp.sum(lv[2:3] * lv[3:4], axis=-1, keepdims=True)) + lam_init)
    _flash_init(m_ref, acc_ref)

    def scores(i, c, slot):
        k = k_ref[_tile_cols(c), :]
        bias = _diff_bias(bias_ref, i - c)
        for idx in range(2):
            s_ref[slot, idx] = _dot(k, qm_ref[idx, :, _tile_cols(i)]) + bias

    def absorb(c, slot, masked):
        v_ext = _with_ones(vt_ref[:, _tile_cols(c)])
        for idx in range(2):
            s = s_ref[slot, idx]
            if masked:
                s = jnp.where(_causal_mask_t(t), s, NEG)
            _flash_update(s, v_ext, idx, m_ref, acc_ref)

    def finish(i):
        o = _flash_result(0, B_V_DIM, acc_ref) - lam * _flash_result(1, B_V_DIM, acc_ref)
        y = o * lax.rsqrt(jnp.mean(o * o, axis=0, keepdims=True) + EPS) * g_ref[...] * (1.0 - lam_init)
        y_ref[_tile_cols(i), :] = y.T.astype(BF16)
        _flash_init(m_ref, acc_ref)

    _flash_sweep(n_tiles, scores, absorb, finish)


def _diff_attention(qb_t, qkv, vb_t, bias_tiles, lam_vecs, g_col, batch, seq, lam_init):
    t = ATT_TILE
    k_col0 = KB_COL0 // 128
    return pl.pallas_call(
        functools.partial(_diff_kernel, lam_init),
        grid=(batch, B_HEADS),
        in_specs=[
            pl.BlockSpec((128, seq), lambda b, h: (h, b)),
            pl.BlockSpec((seq, 128), lambda b, h: (b, k_col0 + h)),
            pl.BlockSpec((128, seq), lambda b, h: (h, b)),
            pl.BlockSpec((1, B_BIAS_TILES, BIAS_TILE, BIAS_TILE), lambda b, h: (h, 0, 0, 0)),
            _const_spec((4, B_QK_DIM)),
            _const_spec((B_V_DIM, 1)),
        ],
        out_specs=pl.BlockSpec((seq, 128), lambda b, h: (b, h)),
        out_shape=jax.ShapeDtypeStruct((batch * seq, B_WIDTH), BF16),
        scratch_shapes=[
            pltpu.VMEM((2, 128, seq), BF16),
            pltpu.VMEM((2, 1, t), F32),
            pltpu.VMEM((2, B_V_DIM + ONES_ROWS, t), F32),
            pltpu.VMEM((2, 2, t, t), F32),
        ],
        compiler_params=_params(2),
        name="diff_attention",
    )(qb_t, qkv, vb_t, bias_tiles, lam_vecs, g_col)


def _mla_kernel(qt_ref, k_ref, vt_ref, y_ref, qm_ref, m_ref, acc_ref, s_ref):
    t = ATT_TILE
    n_tiles = qt_ref.shape[1] // t

    def split_heads(i, carry):
        qf = qt_ref[:, _tile_cols(i)].astype(F32)
        qm_ref[0, :, _tile_cols(i)] = jnp.where(_row_in_ranges(qf.shape, C_HEAD_A_RANGES), qf, 0.0).astype(BF16)
        qm_ref[1, :, _tile_cols(i)] = jnp.where(_row_in_ranges(qf.shape, C_HEAD_B_RANGES), qf, 0.0).astype(BF16)
        return carry

    lax.fori_loop(0, n_tiles, split_heads, 0)
    _flash_init(m_ref, acc_ref)

    def scores(i, c, slot):
        k = k_ref[_tile_cols(c), :]
        for idx in range(2):
            s_ref[slot, idx] = _dot(k, qm_ref[idx, :, _tile_cols(i)])

    def absorb(c, slot, masked):
        v_t = vt_ref[:, _tile_cols(c)]
        for idx in range(2):
            s = s_ref[slot, idx]
            if masked:
                s = jnp.where(_causal_mask_t(t), s, NEG)
            _flash_update(s, _with_ones(v_t[idx * C_V:(idx + 1) * C_V]), idx, m_ref, acc_ref)

    def finish(i):
        o = jnp.concatenate([_flash_result(0, C_V, acc_ref), _flash_result(1, C_V, acc_ref)], axis=0)
        y_ref[_tile_cols(i), :] = o.T.astype(BF16)
        _flash_init(m_ref, acc_ref)

    _flash_sweep(n_tiles, scores, absorb, finish)


def _mla_attention(qc_t, kc, vc_t, batch, seq):
    t = ATT_TILE
    return pl.pallas_call(
        _mla_kernel,
        grid=(batch, C_PAIRS),
        in_specs=[
            pl.BlockSpec((C_PAIR_LANES, seq), lambda b, j: (j, b)),
            pl.BlockSpec((seq, C_PAIR_LANES), lambda b, j: (b, j)),
            pl.BlockSpec((2 * C_V, seq), lambda b, j: (j, b)),
        ],
        out_specs=pl.BlockSpec((seq, 2 * C_V), lambda b, j: (b, j)),
        out_shape=jax.ShapeDtypeStruct((batch * seq, C_WIDTH), BF16),
        scratch_shapes=[
            pltpu.VMEM((2, C_PAIR_LANES, seq), BF16),
            pltpu.VMEM((2, 1, t), F32),
            pltpu.VMEM((2, C_V + ONES_ROWS, t), F32),
            pltpu.VMEM((2, 2, t, t), F32),
        ],
        compiler_params=_params(2),
        name="mla_attention",
    )(qc_t, kc, vc_t)


def _sigmoid(z):
    return 1.0 / (1.0 + jnp.exp(-z))


def _merge_kernel(x_ref, g_ref, ya_ref, yb_ref, yc_ref, wg_ref, bg_ref, wa_ref, wb_ref, wc_ref,
                  wo_ref, out_ref):
    x = x_ref[...]
    h = _rms(x, g_ref[...]).astype(BF16)
    merged = jnp.zeros(x.shape, F32)
    for k, (y_ref, w_ref) in enumerate(((ya_ref, wa_ref), (yb_ref, wb_ref), (yc_ref, wc_ref))):
        cols = slice(k * D_MODEL, (k + 1) * D_MODEL)
        gate = _sigmoid(_dot(h, wg_ref[:, cols]) + bg_ref[:, cols])
        merged = merged + gate * _dot(y_ref[...], w_ref[...])
    out_ref[...] = x + _dot(merged.astype(BF16), wo_ref[...])


def _merge_stage(x2, g, ya, yb, yc, wg, bg, wa, wb, wc, wo):
    n = x2.shape[0]
    tm = ROW_TILE
    row = lambda i: (i, 0)
    return pl.pallas_call(
        _merge_kernel,
        grid=(n // tm,),
        in_specs=[
            pl.BlockSpec((tm, D_MODEL), row),
            _const_spec((1, D_MODEL)),
            pl.BlockSpec((tm, A_WIDTH), row),
            pl.BlockSpec((tm, B_WIDTH), row),
            pl.BlockSpec((tm, C_WIDTH), row),
            _const_spec((D_MODEL, N_BRANCH * D_MODEL)),
            _const_spec((1, N_BRANCH * D_MODEL)),
            _const_spec((A_WIDTH, D_MODEL)),
            _const_spec((B_WIDTH, D_MODEL)),
            _const_spec((C_WIDTH, D_MODEL)),
            _const_spec((D_MODEL, D_MODEL)),
        ],
        out_specs=pl.BlockSpec((tm, D_MODEL), row),
        out_shape=jax.ShapeDtypeStruct((n, D_MODEL), F32),
        compiler_params=_params(1),
        name="gated_merge",
    )(x2, g, ya, yb, yc, wg, bg, wa, wb, wc, wo)


def _ffn_kernel(final_norm, x_ref, g_ref, wg_ref, wu_ref, wd_ref, gf_ref, out_ref):
    x = x_ref[...]
    h = _rms(x, g_ref[...]).astype(BF16)
    acc = jnp.zeros(x.shape, F32)
    for c in range(0, FFN_HIDDEN, FFN_CHUNK):
        cols = slice(c, c + FFN_CHUNK)
        gate = _dot(h, wg_ref[:, cols])
        act = gate * _sigmoid(gate) * _dot(h, wu_ref[:, cols])
        acc = acc + _dot(act.astype(BF16), wd_ref[cols, :])
    y = x + acc
    if final_norm:
        y = _rms(y, gf_ref[...])
    out_ref[...] = y


def _ffn_stage(x2, g, wg, wu, wd, gf, final_norm):
    n = x2.shape[0]
    tm = ROW_TILE
    row = lambda i: (i, 0)
    return pl.pallas_call(
        functools.partial(_ffn_kernel, final_norm),
        grid=(n // tm,),
        in_specs=[
            pl.BlockSpec((tm, D_MODEL), row),
            _const_spec((1, D_MODEL)),
            _const_spec((D_MODEL, FFN_HIDDEN)),
            _const_spec((D_MODEL, FFN_HIDDEN)),
            _const_spec((FFN_HIDDEN, D_MODEL)),
            _const_spec((1, D_MODEL)),
        ],
        out_specs=pl.BlockSpec((tm, D_MODEL), row),
        out_shape=jax.ShapeDtypeStruct((n, D_MODEL), F32),
        compiler_params=_params(1),
        name="swiglu_final" if final_norm else "swiglu",
    )(x2, g, wg, wu, wd, gf)


def _gather_cols(w, idx):
    padded = jnp.concatenate([w, jnp.zeros((w.shape[0], 1), w.dtype)], axis=1)
    return padded[:, idx]


def kernel(x, rel_bias_table, ln_mix_g, w_in, lambda_q1, lambda_k1, lambda_q2, lambda_k2, diff_subln_g, mla_q_norm_g, w_uq, mla_kv_norm_g, w_ukv, w_gate, b_gate, w_br_a, w_br_b, w_br_c, w_o, ln_ffn_g, w_ffn_gate, w_ffn_up, w_ffn_down, final_norm_g):
    batch, seq, _ = x.shape
    depth = w_in.shape[0]
    assert all(seq % (A_BLOCK * dil) == 0 for _, dil in A_PATTERNS)
    assert (seq // A_BLOCK) % A_BLOCKS_PER_STEP == 0
    assert seq % ATT_TILE == 0 and seq % ROW_TILE == 0
    lay = _C_LAYOUT
    cos_t, sin_t = _rope_tables(seq)

    a_tiles = _bias_expand(rel_bias_table, _a_bucket_index(), A_HEADS, 0, 2 * A_BLOCK, "bias_expand_a")
    a_tiles = a_tiles.reshape(A_HEADS // 2, 2, len(A_PATTERNS), 2, 2 * A_BLOCK, A_BLOCK)
    a_tiles = jnp.transpose(a_tiles, (0, 2, 3, 1, 4, 5))
    b_tiles = _bias_expand(rel_bias_table, _b_bucket_index(), B_HEADS, A_HEADS, BIAS_TILE,
                           "bias_expand_b", gain=LOG2E).reshape(B_HEADS, B_BIAS_TILES, BIAS_TILE, BIAS_TILE)

    cq0 = AB_COLS
    ckv0 = cq0 + C_Q_RANK
    kr0 = ckv0 + C_KV_RANK
    x2 = x.reshape(batch * seq, D_MODEL)
    for l in range(depth):
        w_in_l = w_in[l]
        qkv, qb_t, vb_t, qc_t, kc, vc_t = _input_stage(
            x2, ln_mix_g[l][None, :],
            w_in_l[:, :AB_COLS].astype(BF16),
            w_in_l[:, cq0:ckv0].astype(BF16), mla_q_norm_g[l][None, :],
            _gather_cols(w_uq[l], lay["q"]).astype(BF16),
            w_in_l[:, ckv0:kr0].astype(BF16), mla_kv_norm_g[l][None, :],
            _gather_cols(w_ukv[l], lay["k"]).astype(BF16),
            w_ukv[l][:, lay["v"]].astype(BF16),
            _gather_cols(w_in_l[:, kr0:], lay["kr"]).astype(BF16),
            cos_t, sin_t, seq)

        ya = _dilated_attention(qkv, a_tiles, batch, seq)

        lam_init = 0.8 - 0.6 * math.exp(-0.3 * l)
        lam_vecs = jnp.stack([lambda_q1[l], lambda_k1[l], lambda_q2[l], lambda_k2[l]]).astype(F32)
        yb = _diff_attention(qb_t, qkv, vb_t, b_tiles, lam_vecs, diff_subln_g[l][:, None],
                             batch, seq, lam_init)
        yc = _mla_attention(qc_t, kc, vc_t, batch, seq)

        x2 = _merge_stage(x2, ln_mix_g[l][None, :], ya, yb, yc,
                          w_gate[l].astype(BF16), b_gate[l][None, :],
                          w_br_a[l].astype(BF16), w_br_b[l].astype(BF16), w_br_c[l].astype(BF16),
                          w_o[l].astype(BF16))
        x2 = _ffn_stage(x2, ln_ffn_g[l][None, :], w_ffn_gate[l].astype(BF16),
                        w_ffn_up[l].astype(BF16), w_ffn_down[l].astype(BF16),
                        final_norm_g[None, :], final_norm=(l == depth - 1))
    return x2.reshape(batch, seq, D_MODEL)
```

```python
import functools
import math

import numpy as np
import jax
import jax.numpy as jnp
from jax import lax
from jax.experimental import pallas as pl
from jax.experimental.pallas import tpu as pltpu

D_MODEL = 1024
HEAD_DIM = 64
A_HEADS = 8
A_PATTERNS = ((2048, 16), (512, 4), (128, 1))
A_WIDTH = A_HEADS * HEAD_DIM
A_BLOCK = 128
A_BLOCKS_PER_STEP = 8
B_HEADS = 4
B_QK_DIM = HEAD_DIM
B_V_DIM = 2 * HEAD_DIM
B_WIDTH = B_HEADS * B_V_DIM
C_HEADS = 8
C_PAIRS = C_HEADS // 2
C_Q_RANK = 768
C_KV_RANK = 256
C_NOPE = 64
C_ROPE = 32
C_V = 64
C_WIDTH = C_HEADS * C_V
C_PAIR_LANES = 256
C_COLS = C_PAIRS * C_PAIR_LANES
ROPE_THETA = 10000.0
REL_BUCKETS = 32
REL_MAX_DIST = 2048
AB_COLS = 3 * A_WIDTH + 4 * B_HEADS * B_QK_DIM + B_WIDTH
QB_COL0 = 3 * A_WIDTH
KB_COL0 = QB_COL0 + 2 * B_HEADS * B_QK_DIM
VB_COL0 = KB_COL0 + 2 * B_HEADS * B_QK_DIM
N_BRANCH = 3
FFN_HIDDEN = 2816
EPS = 1e-6
NEG = -1e30
LOG2E = math.log2(math.e)
A_SCORE_SCALE = HEAD_DIM ** -0.5 * LOG2E
B_SCORE_SCALE = B_QK_DIM ** -0.5 * LOG2E
C_SCORE_SCALE = (C_NOPE + C_ROPE) ** -0.5 * LOG2E
ONES_ROWS = 16

ROW_TILE = 512
ATT_TILE = 512
BIAS_TILE = 256
B_BIAS_TILES = 8
FFN_CHUNK = 256
VMEM_LIMIT = 56 * 1024 * 1024

F32 = jnp.float32
BF16 = jnp.bfloat16


def _dot(a, b):
    return jnp.dot(a, b, preferred_element_type=F32)


def _dot_nt(a, b):
    return lax.dot_general(a, b, (((1,), (1,)), ((), ())), preferred_element_type=F32)


def _rms(x, g):
    return x * lax.rsqrt(jnp.mean(x * x, axis=-1, keepdims=True) + EPS) * g


def _swap_halves(t):
    half = t.shape[1] // 2
    return jnp.concatenate([t[:, half:], t[:, :half]], axis=1)


def _const_spec(shape):
    nd = len(shape)
    return pl.BlockSpec(shape, lambda *_: (0,) * nd, pipeline_mode=pl.Buffered(1))


def _params(n_axes):
    return pltpu.CompilerParams(dimension_semantics=("parallel",) * n_axes,
                                vmem_limit_bytes=VMEM_LIMIT)


def _t5_bucket_np(dist):
    dist = np.maximum(dist, 0)
    exact = REL_BUCKETS // 2
    ratio = np.maximum(dist, 1).astype(np.float32) / np.float32(exact)
    log_ratio = np.log(ratio).astype(np.float32) / np.float32(math.log(REL_MAX_DIST / exact))
    large = np.minimum(exact + (log_ratio * np.float32(REL_BUCKETS - exact)).astype(np.int32),
                       REL_BUCKETS - 1)
    return np.where(dist < exact, dist, large).astype(np.int32)


MASKED_BUCKET = REL_BUCKETS


def _a_bucket_index():
    kj = np.arange(2 * A_BLOCK)[:, None]
    qi = np.arange(A_BLOCK)[None, :]
    step = qi + A_BLOCK - kj
    in_band = (step >= 0) & (step <= A_BLOCK)
    tiles = []
    for _, dil in A_PATTERNS:
        bucket = _t5_bucket_np(step * dil)
        for first_block in (False, True):
            valid = in_band & (kj >= A_BLOCK) if first_block else in_band
            tiles.append(np.where(valid, bucket, MASKED_BUCKET))
    return np.concatenate(tiles, axis=0).astype(np.int32)


def _b_bucket_index():
    kj = np.arange(BIAS_TILE)[:, None]
    qi = np.arange(BIAS_TILE)[None, :]
    e = np.arange(B_BIAS_TILES)[:, None, None]
    idx = _t5_bucket_np(e * BIAS_TILE + qi - kj)
    assert (idx[-1] == REL_BUCKETS - 1).all()
    return idx.reshape(B_BIAS_TILES * BIAS_TILE, BIAS_TILE)


def _c_layout():
    hq = C_NOPE + C_ROPE
    q_zero = C_HEADS * hq
    kv_zero = C_HEADS * (C_NOPE + C_V)
    half = C_ROPE // 2
    q_idx = np.full((C_PAIRS, C_PAIR_LANES), q_zero, np.int32)
    k_idx = np.full((C_PAIRS, C_PAIR_LANES), kv_zero, np.int32)
    for j in range(C_PAIRS):
        a, b = 2 * j, 2 * j + 1
        q_idx[j, 0:64] = a * hq + np.arange(64)
        q_idx[j, 64:80] = a * hq + C_NOPE + np.arange(half)
        q_idx[j, 80:96] = b * hq + C_NOPE + np.arange(half)
        q_idx[j, 128:192] = b * hq + np.arange(64)
        q_idx[j, 192:208] = a * hq + C_NOPE + half + np.arange(half)
        q_idx[j, 208:224] = b * hq + C_NOPE + half + np.arange(half)
        k_idx[j, 0:64] = a * (C_NOPE + C_V) + np.arange(64)
        k_idx[j, 128:192] = b * (C_NOPE + C_V) + np.arange(64)
    v_idx = (np.arange(C_HEADS)[:, None] * (C_NOPE + C_V) + C_NOPE + np.arange(C_V)[None, :])
    kr_idx = np.full((C_PAIR_LANES,), C_ROPE, np.int32)
    kr_idx[64:80] = np.arange(half)
    kr_idx[80:96] = np.arange(half)
    kr_idx[192:208] = half + np.arange(half)
    kr_idx[208:224] = half + np.arange(half)
    rope_lane = np.full((C_PAIR_LANES,), -1, np.int32)
    for start in (64, 80, 192, 208):
        rope_lane[start:start + half] = np.arange(half)
    sin_sign = np.zeros((C_PAIR_LANES,), np.float32)
    sin_sign[64:96] = -1.0
    sin_sign[192:224] = 1.0
    return dict(q=q_idx.reshape(-1), k=k_idx.reshape(-1), v=v_idx.reshape(-1), kr=kr_idx,
                rope_lane=rope_lane, sin_sign=sin_sign)


_C_LAYOUT = _c_layout()
C_HEAD_A_RANGES = ((0, 80), (192, 208))
C_HEAD_B_RANGES = ((80, 96), (128, 192), (208, 224))


def _rope_tables(seq):
    pos = jnp.arange(seq, dtype=F32)
    inv_freq = ROPE_THETA ** (-jnp.arange(0, C_ROPE, 2, dtype=F32) / C_ROPE)
    ang = pos[:, None] * inv_freq[None, :]
    cos, sin = jnp.cos(ang), jnp.sin(ang)
    lane = _C_LAYOUT["rope_lane"]
    is_rope = jnp.asarray(lane >= 0)
    gather = np.maximum(lane, 0)
    cos_t = jnp.where(is_rope[None, :], cos[:, gather], 1.0)
    sin_t = sin[:, gather] * jnp.asarray(_C_LAYOUT["sin_sign"])[None, :]
    return cos_t, sin_t


def _bias_expand_kernel(n_heads, head0, gain, table_ref, idx_ref, out_ref):
    idx = idx_ref[...]
    hit = [idx == b for b in range(1, REL_BUCKETS)]
    masked = idx == MASKED_BUCKET
    for h in range(n_heads):
        acc = jnp.full(idx.shape, table_ref[0, head0 + h] * gain, F32)
        for b in range(1, REL_BUCKETS):
            acc = jnp.where(hit[b - 1], table_ref[b, head0 + h] * gain, acc)
        out_ref[h] = jnp.where(masked, NEG, acc)


def _bias_expand(table, idx, n_heads, head0, block_rows, name, gain=1.0):
    rows, cols = idx.shape
    return pl.pallas_call(
        functools.partial(_bias_expand_kernel, n_heads, head0, gain),
        grid=(rows // block_rows,),
        in_specs=[
            pl.BlockSpec(memory_space=pltpu.SMEM),
            pl.BlockSpec((block_rows, cols), lambda i: (i, 0)),
        ],
        out_specs=pl.BlockSpec((n_heads, block_rows, cols), lambda i: (0, i, 0)),
        out_shape=jax.ShapeDtypeStruct((n_heads, rows, cols), F32),
        compiler_params=_params(1),
        name=name,
    )(table, jnp.asarray(idx))


def _input_kernel(x_ref, g_ref, wab_ref, wcq_ref, gq_ref, wuq_ref, wckv_ref, gkv_ref, wuk_ref,
                  wuv_ref, wkr_ref, cos_ref, sin_ref,
                  qkv_ref, qbt_ref, vbt_ref, qct_ref, kc_ref, vct_ref):
    h = _rms(x_ref[...], g_ref[...]).astype(BF16)
    for c in range(0, AB_COLS, 512):
        t = _dot(h, wab_ref[:, c:c + 512])
        qkv_ref[:, c:c + 512] = (t * A_SCORE_SCALE if c == 0 else t).astype(BF16)
        if c == QB_COL0:
            qbt_ref[...] = (t * B_SCORE_SCALE).T.astype(BF16)
        if c == VB_COL0:
            vbt_ref[...] = t.T.astype(BF16)
    cos = cos_ref[...]
    sin = sin_ref[...]
    cqn = _rms(_dot(h, wcq_ref[...]), gq_ref[...]).astype(BF16)
    for j in range(C_PAIRS):
        cols = slice(j * C_PAIR_LANES, (j + 1) * C_PAIR_LANES)
        t = _dot(cqn, wuq_ref[:, cols])
        qct_ref[cols, :] = ((t * cos + _swap_halves(t) * sin) * C_SCORE_SCALE).T.astype(BF16)
    ckvn = _rms(_dot(h, wckv_ref[...]), gkv_ref[...]).astype(BF16)
    kr = _dot(h, wkr_ref[...])
    kr_rot = kr * cos + _swap_halves(kr) * sin
    for j in range(C_PAIRS):
        cols = slice(j * C_PAIR_LANES, (j + 1) * C_PAIR_LANES)
        kc_ref[:, cols] = (_dot(ckvn, wuk_ref[:, cols]) + kr_rot).astype(BF16)
    vct_ref[...] = _dot(ckvn, wuv_ref[...]).T.astype(BF16)


def _input_stage(x2, g, wab, wcq, gq, wuq, wckv, gkv, wuk, wuv, wkr, cos_t, sin_t, seq):
    n = x2.shape[0]
    tm = ROW_TILE
    pos_blocks = seq // tm
    row = lambda i: (i, 0)
    col = lambda i: (0, i)
    return pl.pallas_call(
        _input_kernel,
        grid=(n // tm,),
        in_specs=[
            pl.BlockSpec((tm, D_MODEL), row),
            _const_spec((1, D_MODEL)),
            _const_spec((D_MODEL, AB_COLS)),
            _const_spec((D_MODEL, C_Q_RANK)),
            _const_spec((1, C_Q_RANK)),
            _const_spec((C_Q_RANK, C_COLS)),
            _const_spec((D_MODEL, C_KV_RANK)),
            _const_spec((1, C_KV_RANK)),
            _const_spec((C_KV_RANK, C_COLS)),
            _const_spec((C_KV_RANK, C_WIDTH)),
            _const_spec((D_MODEL, C_PAIR_LANES)),
            pl.BlockSpec((tm, C_PAIR_LANES), lambda i: (i % pos_blocks, 0)),
            pl.BlockSpec((tm, C_PAIR_LANES), lambda i: (i % pos_blocks, 0)),
        ],
        out_specs=[
            pl.BlockSpec((tm, AB_COLS), row),
            pl.BlockSpec((B_WIDTH, tm), col),
            pl.BlockSpec((B_WIDTH, tm), col),
            pl.BlockSpec((C_COLS, tm), col),
            pl.BlockSpec((tm, C_COLS), row),
            pl.BlockSpec((C_WIDTH, tm), col),
        ],
        out_shape=[
            jax.ShapeDtypeStruct((n, AB_COLS), BF16),
            jax.ShapeDtypeStruct((B_WIDTH, n), BF16),
            jax.ShapeDtypeStruct((B_WIDTH, n), BF16),
            jax.ShapeDtypeStruct((C_COLS, n), BF16),
            jax.ShapeDtypeStruct((n, C_COLS), BF16),
            jax.ShapeDtypeStruct((C_WIDTH, n), BF16),
        ],
        compiler_params=_params(1),
        name="input_stage",
    )(x2, g, wab, wcq, gq, wuq, wckv, gkv, wuk, wuv, wkr, cos_t, sin_t)


def _dot_tn(a, b):
    return lax.dot_general(a, b, (((0,), (0,)), ((), ())), preferred_element_type=F32)


def _dilated_kernel(q_ref, k_ref, v_ref, bias_ref, y_ref, qf_ref, kf_ref, vf_ref, m_ref, w_ref, a_ref):
    seq = q_ref.shape[0]
    chunk = 512

    def widen(c, carry):
        rows = pl.ds(pl.multiple_of(c * chunk, chunk), chunk)
        qf_ref[rows, :] = q_ref[rows, :].astype(F32)
        kf_ref[rows, :] = k_ref[rows, :].astype(F32)
        vf_ref[rows, :] = v_ref[rows, :].astype(F32)
        return carry

    lax.fori_loop(0, seq // chunk, widen, 0)

    tile = (A_BLOCK, A_BLOCK)
    low_lane = lax.broadcasted_iota(jnp.int32, tile, 1) < HEAD_DIM
    low_row = lax.broadcasted_iota(jnp.int32, tile, 0) < HEAD_DIM
    last = len(A_PATTERNS) - 1
    for g, (_, dil) in enumerate(A_PATTERNS):
        nb = seq // dil // A_BLOCK
        span = dil * A_BLOCK

        def attend(blk, g=g, dil=dil, nb=nb, span=span):
            r = blk // nb
            n = blk % nb
            base = r + n * span
            prev_base = jnp.maximum(base - span, r)
            rows = pl.ds(base, A_BLOCK, stride=dil)
            prev_rows = pl.ds(prev_base, A_BLOCK, stride=dil)
            q = qf_ref[rows, :]
            k2 = jnp.concatenate([kf_ref[prev_rows, :], kf_ref[rows, :]], axis=0).astype(BF16)
            v2 = jnp.concatenate([vf_ref[prev_rows, :], vf_ref[rows, :]], axis=0).astype(BF16)
            first_block = jnp.where(n == 0, 1, 0)
            outs, lses = [], []
            for hh in range(2):
                qm = jnp.where(low_lane if hh == 0 else ~low_lane, q, 0.0).astype(BF16)
                s = _dot_nt(k2, qm) + bias_ref[0, g, first_block, hh]
                m = jnp.max(s, axis=0, keepdims=True)
                p = jnp.exp2(s - m)
                l = jnp.sum(p, axis=0, keepdims=True)
                outs.append(_dot_tn(v2, p.astype(BF16)) * (1.0 / l))
                lses.append(jnp.broadcast_to(m + jnp.log2(l), tile))
            o_blk = jnp.where(low_row, outs[0], outs[1]).T
            lse_blk = jnp.where(low_row, lses[0], lses[1]).T
            return rows, o_blk, lse_blk

        def merge(rows, o_blk, lse_blk, g=g):
            if g == 0:
                m_ref[rows, :] = lse_blk
                a_ref[rows, :] = o_blk
                return
            m_old = m_ref[rows, :]
            m_new = jnp.maximum(m_old, lse_blk)
            c_old = jnp.exp2(m_old - m_new)
            c_blk = jnp.exp2(lse_blk - m_new)
            w_new = c_blk + (c_old if g == 1 else c_old * w_ref[rows, :])
            a_new = c_old * a_ref[rows, :] + c_blk * o_blk
            if g < last:
                m_ref[rows, :] = m_new
                w_ref[rows, :] = w_new
                a_ref[rows, :] = a_new
            else:
                a_ref[rows, :] = a_new / w_new

        def step(it, carry, attend=attend, merge=merge):
            done = [attend(it * A_BLOCKS_PER_STEP + u) for u in range(A_BLOCKS_PER_STEP)]
            for args in done:
                merge(*args)
            return carry

        lax.fori_loop(0, dil * nb // A_BLOCKS_PER_STEP, step, 0)

    def narrow(c, carry):
        rows = pl.ds(pl.multiple_of(c * chunk, chunk), chunk)
        y_ref[rows, :] = a_ref[rows, :].astype(BF16)
        return carry

    lax.fori_loop(0, seq // chunk, narrow, 0)


def _dilated_attention(qkv, bias_tiles, batch, seq):
    pairs = A_HEADS // 2
    col = lambda off: (lambda b, p: (b, off * pairs + p))
    state = pltpu.VMEM((seq, 128), F32)
    return pl.pallas_call(
        _dilated_kernel,
        grid=(batch, pairs),
        in_specs=[
            pl.BlockSpec((seq, 128), col(0)),
            pl.BlockSpec((seq, 128), col(1)),
            pl.BlockSpec((seq, 128), col(2)),
            pl.BlockSpec((1, len(A_PATTERNS), 2, 2, 2 * A_BLOCK, A_BLOCK), lambda b, p: (p, 0, 0, 0, 0, 0)),
        ],
        out_specs=pl.BlockSpec((seq, 128), lambda b, p: (b, p)),
        out_shape=jax.ShapeDtypeStruct((batch * seq, A_WIDTH), BF16),
        scratch_shapes=[state] * 6,
        compiler_params=_params(2),
        name="dilated_attention",
    )(qkv, qkv, qkv, bias_tiles)


def _flash_init(m_ref, acc_ref):
    m_ref[...] = jnp.full(m_ref.shape, NEG, F32)
    acc_ref[...] = jnp.zeros(acc_ref.shape, F32)


def _with_ones(v_t):
    return jnp.concatenate([v_t, jnp.ones((ONES_ROWS, v_t.shape[1]), v_t.dtype)], axis=0)


def _store_scores(s, slot, idx, s_ref, cmax_ref):
    s_ref[slot, idx] = s
    cmax_ref[slot, idx] = jnp.max(s, axis=0, keepdims=True)


def _load_scores(slot, idx, masked, s_ref, cmax_ref):
    s = s_ref[slot, idx]
    if masked:
        s = jnp.where(_causal_mask_t(s.shape[0]), s, NEG)
        return s, jnp.max(s, axis=0, keepdims=True)
    return s, cmax_ref[slot, idx]


def _flash_update(s, col_max, v_ext, idx, m_ref, acc_ref):
    m_prev = m_ref[idx]
    m_new = jnp.maximum(m_prev, col_max)
    alpha = jnp.exp2(m_prev - m_new)
    p = jnp.exp2(s - m_new)
    acc_ref[idx] = alpha * acc_ref[idx] + _dot(v_ext, p.astype(BF16))
    m_ref[idx] = m_new


def _flash_result(idx, dv, acc_ref):
    acc = acc_ref[idx]
    return acc[:dv] * (1.0 / acc[dv:dv + 1])


def _flash_sweep(n_tiles, scores, absorb, finish):
    n_items = n_tiles * (n_tiles + 1) // 2
    assert n_items % 2 == 0

    def following(i, c):
        end = c == i
        return jnp.where(end, i + 1, i), jnp.where(end, 0, c + 1)

    scores(0, 0, 0)

    def two_items(u, carry):
        i0, c0 = carry
        i1, c1 = following(i0, c0)
        i2, c2 = following(i1, c1)
        i2s = jnp.minimum(i2, n_tiles - 1)

        def variant(diag0, diag1):
            def run():
                scores(i1, c1, 1)
                absorb(c0, 0, diag0)
                if diag0:
                    finish(i0)
                scores(i2s, c2, 0)
                absorb(c1, 1, diag1)
                if diag1:
                    finish(i1)
            return run

        kind = 2 * (c0 == i0).astype(jnp.int32) + (c1 == i1).astype(jnp.int32)
        lax.switch(kind, [variant(False, False), variant(False, True),
                          variant(True, False), variant(True, True)])
        return i2, c2

    lax.fori_loop(0, n_items // 2, two_items, (jnp.int32(0), jnp.int32(0)))


def _causal_mask_t(t):
    key = lax.broadcasted_iota(jnp.int32, (t, t), 0)
    query = lax.broadcasted_iota(jnp.int32, (t, t), 1)
    return key <= query


def _row_in_ranges(shape, ranges):
    row = lax.broadcasted_iota(jnp.int32, shape, 0)
    hit = None
    for lo, hi in ranges:
        r = (row >= lo) & (row < hi)
        hit = r if hit is None else hit | r
    return hit


def _diff_bias(bias_ref, d):
    sub = ATT_TILE // BIAS_TILE
    rows = []
    for ka in range(sub):
        tiles = []
        for qa in range(sub):
            e = sub * d + qa - ka
            e = min(max(e, 0), B_BIAS_TILES - 1) if isinstance(e, int) else jnp.clip(e, 0, B_BIAS_TILES - 1)
            tiles.append(bias_ref[0, e])
        rows.append(jnp.concatenate(tiles, axis=1))
    return jnp.concatenate(rows, axis=0)


def _tile_cols(i):
    return pl.ds(pl.multiple_of(i * ATT_TILE, ATT_TILE), ATT_TILE)


def _diff_kernel(lam_init, qt_ref, k_ref, vt_ref, bias_ref, lam_ref, g_ref, y_ref,
                 qm_ref, m_ref, acc_ref, s_ref, cmax_ref):
    t = ATT_TILE
    n_tiles = qt_ref.shape[1] // t

    def split_queries(i, carry):
        qf = qt_ref[:, _tile_cols(i)].astype(F32)
        first = lax.broadcasted_iota(jnp.int32, qf.shape, 0) < B_QK_DIM
        qm_ref[0, :, _tile_cols(i)] = jnp.where(first, qf, 0.0).astype(BF16)
        qm_ref[1, :, _tile_cols(i)] = jnp.where(first, 0.0, qf).astype(BF16)
        return carry

    lax.fori_loop(0, n_tiles, split_queries, 0)
    lv = lam_ref[...]
    lam = (jnp.exp(jnp.sum(lv[0:1] * lv[1:2], axis=-1, keepdims=True))
           - jnp.exp(jnp.sum(lv[2:3] * lv[3:4], axis=-1, keepdims=True)) + lam_init)
    _flash_init(m_ref, acc_ref)

    def scores(i, c, slot):
        k = k_ref[_tile_cols(c), :]
        bias = _diff_bias(bias_ref, i - c)
        for idx in range(2):
            _store_scores(_dot(k, qm_ref[idx, :, _tile_cols(i)]) + bias, slot, idx, s_ref, cmax_ref)

    def absorb(c, slot, masked):
        v_ext = _with_ones(vt_ref[:, _tile_cols(c)])
        for idx in range(2):
            s, col_max = _load_scores(slot, idx, masked, s_ref, cmax_ref)
            _flash_update(s, col_max, v_ext, idx, m_ref, acc_ref)

    def finish(i):
        o = _flash_result(0, B_V_DIM, acc_ref) - lam * _flash_result(1, B_V_DIM, acc_ref)
        y = o * lax.rsqrt(jnp.mean(o * o, axis=0, keepdims=True) + EPS) * g_ref[...] * (1.0 - lam_init)
        y_ref[_tile_cols(i), :] = y.T.astype(BF16)
        _flash_init(m_ref, acc_ref)

    _flash_sweep(n_tiles, scores, absorb, finish)


def _diff_attention(qb_t, qkv, vb_t, bias_tiles, lam_vecs, g_col, batch, seq, lam_init):
    t = ATT_TILE
    k_col0 = KB_COL0 // 128
    return pl.pallas_call(
        functools.partial(_diff_kernel, lam_init),
        grid=(batch, B_HEADS),
        in_specs=[
            pl.BlockSpec((128, seq), lambda b, h: (h, b)),
            pl.BlockSpec((seq, 128), lambda b, h: (b, k_col0 + h)),
            pl.BlockSpec((128, seq), lambda b, h: (h, b)),
            pl.BlockSpec((1, B_BIAS_TILES, BIAS_TILE, BIAS_TILE), lambda b, h: (h, 0, 0, 0)),
            _const_spec((4, B_QK_DIM)),
            _const_spec((B_V_DIM, 1)),
        ],
        out_specs=pl.BlockSpec((seq, 128), lambda b, h: (b, h)),
        out_shape=jax.ShapeDtypeStruct((batch * seq, B_WIDTH), BF16),
        scratch_shapes=[
            pltpu.VMEM((2, 128, seq), BF16),
            pltpu.VMEM((2, 1, t), F32),
            pltpu.VMEM((2, B_V_DIM + ONES_ROWS, t), F32),
            pltpu.VMEM((2, 2, t, t), F32),
            pltpu.VMEM((2, 2, 1, t), F32),
        ],
        compiler_params=_params(2),
        name="diff_attention",
    )(qb_t, qkv, vb_t, bias_tiles, lam_vecs, g_col)


def _mla_kernel(qt_ref, k_ref, vt_ref, y_ref, qm_ref, m_ref, acc_ref, s_ref, cmax_ref):
    t = ATT_TILE
    n_tiles = qt_ref.shape[1] // t

    def split_heads(i, carry):
        qf = qt_ref[:, _tile_cols(i)].astype(F32)
        qm_ref[0, :, _tile_cols(i)] = jnp.where(_row_in_ranges(qf.shape, C_HEAD_A_RANGES), qf, 0.0).astype(BF16)
        qm_ref[1, :, _tile_cols(i)] = jnp.where(_row_in_ranges(qf.shape, C_HEAD_B_RANGES), qf, 0.0).astype(BF16)
        return carry

    lax.fori_loop(0, n_tiles, split_heads, 0)
    _flash_init(m_ref, acc_ref)

    def scores(i, c, slot):
        k = k_ref[_tile_cols(c), :]
        for idx in range(2):
            _store_scores(_dot(k, qm_ref[idx, :, _tile_cols(i)]), slot, idx, s_ref, cmax_ref)

    def absorb(c, slot, masked):
        v_t = vt_ref[:, _tile_cols(c)]
        for idx in range(2):
            s, col_max = _load_scores(slot, idx, masked, s_ref, cmax_ref)
            _flash_update(s, col_max, _with_ones(v_t[idx * C_V:(idx + 1) * C_V]), idx, m_ref, acc_ref)

    def finish(i):
        o = jnp.concatenate([_flash_result(0, C_V, acc_ref), _flash_result(1, C_V, acc_ref)], axis=0)
        y_ref[_tile_cols(i), :] = o.T.astype(BF16)
        _flash_init(m_ref, acc_ref)

    _flash_sweep(n_tiles, scores, absorb, finish)


def _mla_attention(qc_t, kc, vc_t, batch, seq):
    t = ATT_TILE
    return pl.pallas_call(
        _mla_kernel,
        grid=(batch, C_PAIRS),
        in_specs=[
            pl.BlockSpec((C_PAIR_LANES, seq), lambda b, j: (j, b)),
            pl.BlockSpec((seq, C_PAIR_LANES), lambda b, j: (b, j)),
            pl.BlockSpec((2 * C_V, seq), lambda b, j: (j, b)),
        ],
        out_specs=pl.BlockSpec((seq, 2 * C_V), lambda b, j: (b, j)),
        out_shape=jax.ShapeDtypeStruct((batch * seq, C_WIDTH), BF16),
        scratch_shapes=[
            pltpu.VMEM((2, C_PAIR_LANES, seq), BF16),
            pltpu.VMEM((2, 1, t), F32),
            pltpu.VMEM((2, C_V + ONES_ROWS, t), F32),
            pltpu.VMEM((2, 2, t, t), F32),
            pltpu.VMEM((2, 2, 1, t), F32),
        ],
        compiler_params=_params(2),
        name="mla_attention",
    )(qc_t, kc, vc_t)


def _sigmoid(z):
    return 1.0 / (1.0 + jnp.exp(-z))


def _merge_kernel(x_ref, g_ref, ya_ref, yb_ref, yc_ref, wg_ref, bg_ref, wa_ref, wb_ref, wc_ref,
                  wo_ref, out_ref):
    x = x_ref[...]
    h = _rms(x, g_ref[...]).astype(BF16)
    merged = jnp.zeros(x.shape, F32)
    for k, (y_ref, w_ref) in enumerate(((ya_ref, wa_ref), (yb_ref, wb_ref), (yc_ref, wc_ref))):
        cols = slice(k * D_MODEL, (k + 1) * D_MODEL)
        gate = _sigmoid(_dot(h, wg_ref[:, cols]) + bg_ref[:, cols])
        merged = merged + gate * _dot(y_ref[...], w_ref[...])
    out_ref[...] = x + _dot(merged.astype(BF16), wo_ref[...])


def _merge_stage(x2, g, ya, yb, yc, wg, bg, wa, wb, wc, wo):
    n = x2.shape[0]
    tm = ROW_TILE
    row = lambda i: (i, 0)
    return pl.pallas_call(
        _merge_kernel,
        grid=(n // tm,),
        in_specs=[
            pl.BlockSpec((tm, D_MODEL), row),
            _const_spec((1, D_MODEL)),
            pl.BlockSpec((tm, A_WIDTH), row),
            pl.BlockSpec((tm, B_WIDTH), row),
            pl.BlockSpec((tm, C_WIDTH), row),
            _const_spec((D_MODEL, N_BRANCH * D_MODEL)),
            _const_spec((1, N_BRANCH * D_MODEL)),
            _const_spec((A_WIDTH, D_MODEL)),
            _const_spec((B_WIDTH, D_MODEL)),
            _const_spec((C_WIDTH, D_MODEL)),
            _const_spec((D_MODEL, D_MODEL)),
        ],
        out_specs=pl.BlockSpec((tm, D_MODEL), row),
        out_shape=jax.ShapeDtypeStruct((n, D_MODEL), F32),
        compiler_params=_params(1),
        name="gated_merge",
    )(x2, g, ya, yb, yc, wg, bg, wa, wb, wc, wo)


def _ffn_kernel(final_norm, x_ref, g_ref, wg_ref, wu_ref, wd_ref, gf_ref, out_ref):
    x = x_ref[...]
    h = _rms(x, g_ref[...]).astype(BF16)
    acc = jnp.zeros(x.shape, F32)
    for c in range(0, FFN_HIDDEN, FFN_CHUNK):
        cols = slice(c, c + FFN_CHUNK)
        gate = _dot(h, wg_ref[:, cols])
        act = gate * _sigmoid(gate) * _dot(h, wu_ref[:, cols])
        acc = acc + _dot(act.astype(BF16), wd_ref[cols, :])
    y = x + acc
    if final_norm:
        y = _rms(y, gf_ref[...])
    out_ref[...] = y


def _ffn_stage(x2, g, wg, wu, wd, gf, final_norm):
    n = x2.shape[0]
    tm = ROW_TILE
    row = lambda i: (i, 0)
    return pl.pallas_call(
        functools.partial(_ffn_kernel, final_norm),
        grid=(n // tm,),
        in_specs=[
            pl.BlockSpec((tm, D_MODEL), row),
            _const_spec((1, D_MODEL)),
            _const_spec((D_MODEL, FFN_HIDDEN)),
            _const_spec((D_MODEL, FFN_HIDDEN)),
            _const_spec((FFN_HIDDEN, D_MODEL)),
            _const_spec((1, D_MODEL)),
        ],
        out_specs=pl.BlockSpec((tm, D_MODEL), row),
        out_shape=jax.ShapeDtypeStruct((n, D_MODEL), F32),
        compiler_params=_params(1),
        name="swiglu_final" if final_norm else "swiglu",
    )(x2, g, wg, wu, wd, gf)


def _gather_cols(w, idx):
    padded = jnp.concatenate([w, jnp.zeros((w.shape[0], 1), w.dtype)], axis=1)
    return padded[:, idx]


def kernel(x, rel_bias_table, ln_mix_g, w_in, lambda_q1, lambda_k1, lambda_q2, lambda_k2, diff_subln_g, mla_q_norm_g, w_uq, mla_kv_norm_g, w_ukv, w_gate, b_gate, w_br_a, w_br_b, w_br_c, w_o, ln_ffn_g, w_ffn_gate, w_ffn_up, w_ffn_down, final_norm_g):
    batch, seq, _ = x.shape
    depth = w_in.shape[0]
    assert all(seq % (A_BLOCK * dil) == 0 for _, dil in A_PATTERNS)
    assert (seq // A_BLOCK) % A_BLOCKS_PER_STEP == 0
    assert seq % ATT_TILE == 0 and seq % ROW_TILE == 0
    lay = _C_LAYOUT
    cos_t, sin_t = _rope_tables(seq)

    a_tiles = _bias_expand(rel_bias_table, _a_bucket_index(), A_HEADS, 0, 2 * A_BLOCK, "bias_expand_a",
                           gain=LOG2E)
    a_tiles = a_tiles.reshape(A_HEADS // 2, 2, len(A_PATTERNS), 2, 2 * A_BLOCK, A_BLOCK)
    a_tiles = jnp.transpose(a_tiles, (0, 2, 3, 1, 4, 5))
    b_tiles = _bias_expand(rel_bias_table, _b_bucket_index(), B_HEADS, A_HEADS, BIAS_TILE,
                           "bias_expand_b", gain=LOG2E).reshape(B_HEADS, B_BIAS_TILES, BIAS_TILE, BIAS_TILE)

    cq0 = AB_COLS
    ckv0 = cq0 + C_Q_RANK
    kr0 = ckv0 + C_KV_RANK
    x2 = x.reshape(batch * seq, D_MODEL)
    for l in range(depth):
        w_in_l = w_in[l]
        qkv, qb_t, vb_t, qc_t, kc, vc_t = _input_stage(
            x2, ln_mix_g[l][None, :],
            w_in_l[:, :AB_COLS].astype(BF16),
            w_in_l[:, cq0:ckv0].astype(BF16), mla_q_norm_g[l][None, :],
            _gather_cols(w_uq[l], lay["q"]).astype(BF16),
            w_in_l[:, ckv0:kr0].astype(BF16), mla_kv_norm_g[l][None, :],
            _gather_cols(w_ukv[l], lay["k"]).astype(BF16),
            w_ukv[l][:, lay["v"]].astype(BF16),
            _gather_cols(w_in_l[:, kr0:], lay["kr"]).astype(BF16),
            cos_t, sin_t, seq)

        ya = _dilated_attention(qkv, a_tiles, batch, seq)

        lam_init = 0.8 - 0.6 * math.exp(-0.3 * l)
        lam_vecs = jnp.stack([lambda_q1[l], lambda_k1[l], lambda_q2[l], lambda_k2[l]]).astype(F32)
        yb = _diff_attention(qb_t, qkv, vb_t, b_tiles, lam_vecs, diff_subln_g[l][:, None],
                             batch, seq, lam_init)
        yc = _mla_attention(qc_t, kc, vc_t, batch, seq)

        x2 = _merge_stage(x2, ln_mix_g[l][None, :], ya, yb, yc,
                          w_gate[l].astype(BF16), b_gate[l][None, :],
                          w_br_a[l].astype(BF16), w_br_b[l].astype(BF16), w_br_c[l].astype(BF16),
                          w_o[l].astype(BF16))
        x2 = _ffn_stage(x2, ln_ffn_g[l][None, :], w_ffn_gate[l].astype(BF16),
                        w_ffn_up[l].astype(BF16), w_ffn_down[l].astype(BF16),
                        final_norm_g[None, :], final_norm=(l == depth - 1))
    return x2.reshape(batch, seq, D_MODEL)
```

```python
import functools
import math

import numpy as np
import jax
import jax.numpy as jnp
from jax import lax
from jax.experimental import pallas as pl
from jax.experimental.pallas import tpu as pltpu

D_MODEL = 1024
HEAD_DIM = 64
A_HEADS = 8
A_PATTERNS = ((2048, 16), (512, 4), (128, 1))
A_WIDTH = A_HEADS * HEAD_DIM
A_BLOCK = 128
A_BLOCKS_PER_STEP = 16
B_HEADS = 4
B_QK_DIM = HEAD_DIM
B_V_DIM = 2 * HEAD_DIM
B_WIDTH = B_HEADS * B_V_DIM
C_HEADS = 8
C_PAIRS = C_HEADS // 2
C_Q_RANK = 768
C_KV_RANK = 256
C_NOPE = 64
C_ROPE = 32
C_V = 64
C_WIDTH = C_HEADS * C_V
C_PAIR_LANES = 256
C_COLS = C_PAIRS * C_PAIR_LANES
ROPE_THETA = 10000.0
REL_BUCKETS = 32
REL_MAX_DIST = 2048
AB_COLS = 3 * A_WIDTH + 4 * B_HEADS * B_QK_DIM + B_WIDTH
QB_COL0 = 3 * A_WIDTH
KB_COL0 = QB_COL0 + 2 * B_HEADS * B_QK_DIM
VB_COL0 = KB_COL0 + 2 * B_HEADS * B_QK_DIM
N_BRANCH = 3
FFN_HIDDEN = 2816
EPS = 1e-6
NEG = -1e30
LOG2E = math.log2(math.e)
A_SCORE_SCALE = HEAD_DIM ** -0.5 * LOG2E
B_SCORE_SCALE = B_QK_DIM ** -0.5 * LOG2E
C_SCORE_SCALE = (C_NOPE + C_ROPE) ** -0.5 * LOG2E
ONES_ROWS = 16

ROW_TILE = 512
ATT_TILE = 512
KEY_PART = 256
BIAS_TILE = 256
B_BIAS_TILES = 8
FFN_CHUNK = 256
VMEM_LIMIT = 56 * 1024 * 1024

F32 = jnp.float32
BF16 = jnp.bfloat16


def _dot(a, b):
    return jnp.dot(a, b, preferred_element_type=F32)


def _dot_nt(a, b):
    return lax.dot_general(a, b, (((1,), (1,)), ((), ())), preferred_element_type=F32)


def _rms(x, g):
    return x * lax.rsqrt(jnp.mean(x * x, axis=-1, keepdims=True) + EPS) * g


def _swap_halves(t):
    half = t.shape[1] // 2
    return jnp.concatenate([t[:, half:], t[:, :half]], axis=1)


def _const_spec(shape):
    nd = len(shape)
    return pl.BlockSpec(shape, lambda *_: (0,) * nd, pipeline_mode=pl.Buffered(1))


def _params(n_axes):
    return pltpu.CompilerParams(dimension_semantics=("parallel",) * n_axes,
                                vmem_limit_bytes=VMEM_LIMIT)


def _t5_bucket_np(dist):
    dist = np.maximum(dist, 0)
    exact = REL_BUCKETS // 2
    ratio = np.maximum(dist, 1).astype(np.float32) / np.float32(exact)
    log_ratio = np.log(ratio).astype(np.float32) / np.float32(math.log(REL_MAX_DIST / exact))
    large = np.minimum(exact + (log_ratio * np.float32(REL_BUCKETS - exact)).astype(np.int32),
                       REL_BUCKETS - 1)
    return np.where(dist < exact, dist, large).astype(np.int32)


MASKED_BUCKET = REL_BUCKETS


def _a_bucket_index():
    kj = np.arange(2 * A_BLOCK)[:, None]
    qi = np.arange(A_BLOCK)[None, :]
    step = qi + A_BLOCK - kj
    in_band = (step >= 0) & (step <= A_BLOCK)
    tiles = []
    for _, dil in A_PATTERNS:
        bucket = _t5_bucket_np(step * dil)
        for first_block in (False, True):
            valid = in_band & (kj >= A_BLOCK) if first_block else in_band
            tiles.append(np.where(valid, bucket, MASKED_BUCKET))
    return np.concatenate(tiles, axis=0).astype(np.int32)


def _b_bucket_index():
    kj = np.arange(BIAS_TILE)[:, None]
    qi = np.arange(BIAS_TILE)[None, :]
    e = np.arange(B_BIAS_TILES)[:, None, None]
    idx = _t5_bucket_np(e * BIAS_TILE + qi - kj)
    assert (idx[-1] == REL_BUCKETS - 1).all()
    return idx.reshape(B_BIAS_TILES * BIAS_TILE, BIAS_TILE)


def _c_layout():
    hq = C_NOPE + C_ROPE
    q_zero = C_HEADS * hq
    kv_zero = C_HEADS * (C_NOPE + C_V)
    half = C_ROPE // 2
    q_idx = np.full((C_PAIRS, C_PAIR_LANES), q_zero, np.int32)
    k_idx = np.full((C_PAIRS, C_PAIR_LANES), kv_zero, np.int32)
    for j in range(C_PAIRS):
        a, b = 2 * j, 2 * j + 1
        q_idx[j, 0:64] = a * hq + np.arange(64)
        q_idx[j, 64:80] = a * hq + C_NOPE + np.arange(half)
        q_idx[j, 80:96] = b * hq + C_NOPE + np.arange(half)
        q_idx[j, 128:192] = b * hq + np.arange(64)
        q_idx[j, 192:208] = a * hq + C_NOPE + half + np.arange(half)
        q_idx[j, 208:224] = b * hq + C_NOPE + half + np.arange(half)
        k_idx[j, 0:64] = a * (C_NOPE + C_V) + np.arange(64)
        k_idx[j, 128:192] = b * (C_NOPE + C_V) + np.arange(64)
    v_idx = (np.arange(C_HEADS)[:, None] * (C_NOPE + C_V) + C_NOPE + np.arange(C_V)[None, :])
    kr_idx = np.full((C_PAIR_LANES,), C_ROPE, np.int32)
    kr_idx[64:80] = np.arange(half)
    kr_idx[80:96] = np.arange(half)
    kr_idx[192:208] = half + np.arange(half)
    kr_idx[208:224] = half + np.arange(half)
    rope_lane = np.full((C_PAIR_LANES,), -1, np.int32)
    for start in (64, 80, 192, 208):
        rope_lane[start:start + half] = np.arange(half)
    sin_sign = np.zeros((C_PAIR_LANES,), np.float32)
    sin_sign[64:96] = -1.0
    sin_sign[192:224] = 1.0
    return dict(q=q_idx.reshape(-1), k=k_idx.reshape(-1), v=v_idx.reshape(-1), kr=kr_idx,
                rope_lane=rope_lane, sin_sign=sin_sign)


_C_LAYOUT = _c_layout()
C_HEAD_A_RANGES = ((0, 80), (192, 208))
C_HEAD_B_RANGES = ((80, 96), (128, 192), (208, 224))


def _rope_tables(seq):
    pos = jnp.arange(seq, dtype=F32)
    inv_freq = ROPE_THETA ** (-jnp.arange(0, C_ROPE, 2, dtype=F32) / C_ROPE)
    ang = pos[:, None] * inv_freq[None, :]
    cos, sin = jnp.cos(ang), jnp.sin(ang)
    lane = _C_LAYOUT["rope_lane"]
    is_rope = jnp.asarray(lane >= 0)
    gather = np.maximum(lane, 0)
    cos_t = jnp.where(is_rope[None, :], cos[:, gather], 1.0)
    sin_t = sin[:, gather] * jnp.asarray(_C_LAYOUT["sin_sign"])[None, :]
    return cos_t, sin_t


def _bias_expand_kernel(n_heads, head0, gain, table_ref, idx_ref, out_ref):
    idx = idx_ref[...]
    hit = [idx == b for b in range(1, REL_BUCKETS)]
    masked = idx == MASKED_BUCKET
    for h in range(n_heads):
        acc = jnp.full(idx.shape, table_ref[0, head0 + h] * gain, F32)
        for b in range(1, REL_BUCKETS):
            acc = jnp.where(hit[b - 1], table_ref[b, head0 + h] * gain, acc)
        out_ref[h] = jnp.where(masked, NEG, acc)


def _bias_expand(table, idx, n_heads, head0, block_rows, name, gain=1.0):
    rows, cols = idx.shape
    return pl.pallas_call(
        functools.partial(_bias_expand_kernel, n_heads, head0, gain),
        grid=(rows // block_rows,),
        in_specs=[
            pl.BlockSpec(memory_space=pltpu.SMEM),
            pl.BlockSpec((block_rows, cols), lambda i: (i, 0)),
        ],
        out_specs=pl.BlockSpec((n_heads, block_rows, cols), lambda i: (0, i, 0)),
        out_shape=jax.ShapeDtypeStruct((n_heads, rows, cols), F32),
        compiler_params=_params(1),
        name=name,
    )(table, jnp.asarray(idx))


def _input_kernel(x_ref, g_ref, wab_ref, wcq_ref, gq_ref, wuq_ref, wckv_ref, gkv_ref, wuk_ref,
                  wuv_ref, wkr_ref, cos_ref, sin_ref,
                  qkv_ref, qbt_ref, vbt_ref, qct_ref, kc_ref, vct_ref):
    h = _rms(x_ref[...], g_ref[...]).astype(BF16)
    for c in range(0, AB_COLS, 512):
        t = _dot(h, wab_ref[:, c:c + 512])
        qkv_ref[:, c:c + 512] = (t * A_SCORE_SCALE if c == 0 else t).astype(BF16)
        if c == QB_COL0:
            qbt_ref[...] = (t * B_SCORE_SCALE).T.astype(BF16)
        if c == VB_COL0:
            vbt_ref[...] = t.T.astype(BF16)
    cos = cos_ref[...]
    sin = sin_ref[...]
    cqn = _rms(_dot(h, wcq_ref[...]), gq_ref[...]).astype(BF16)
    for j in range(C_PAIRS):
        cols = slice(j * C_PAIR_LANES, (j + 1) * C_PAIR_LANES)
        t = _dot(cqn, wuq_ref[:, cols])
        qct_ref[cols, :] = ((t * cos + _swap_halves(t) * sin) * C_SCORE_SCALE).T.astype(BF16)
    ckvn = _rms(_dot(h, wckv_ref[...]), gkv_ref[...]).astype(BF16)
    kr = _dot(h, wkr_ref[...])
    kr_rot = kr * cos + _swap_halves(kr) * sin
    for j in range(C_PAIRS):
        cols = slice(j * C_PAIR_LANES, (j + 1) * C_PAIR_LANES)
        kc_ref[:, cols] = (_dot(ckvn, wuk_ref[:, cols]) + kr_rot).astype(BF16)
    vct_ref[...] = _dot(ckvn, wuv_ref[...]).T.astype(BF16)


def _input_stage(x2, g, wab, wcq, gq, wuq, wckv, gkv, wuk, wuv, wkr, cos_t, sin_t, seq):
    n = x2.shape[0]
    tm = ROW_TILE
    pos_blocks = seq // tm
    row = lambda i: (i, 0)
    col = lambda i: (0, i)
    return pl.pallas_call(
        _input_kernel,
        grid=(n // tm,),
        in_specs=[
            pl.BlockSpec((tm, D_MODEL), row),
            _const_spec((1, D_MODEL)),
            _const_spec((D_MODEL, AB_COLS)),
            _const_spec((D_MODEL, C_Q_RANK)),
            _const_spec((1, C_Q_RANK)),
            _const_spec((C_Q_RANK, C_COLS)),
            _const_spec((D_MODEL, C_KV_RANK)),
            _const_spec((1, C_KV_RANK)),
            _const_spec((C_KV_RANK, C_COLS)),
            _const_spec((C_KV_RANK, C_WIDTH)),
            _const_spec((D_MODEL, C_PAIR_LANES)),
            pl.BlockSpec((tm, C_PAIR_LANES), lambda i: (i % pos_blocks, 0)),
            pl.BlockSpec((tm, C_PAIR_LANES), lambda i: (i % pos_blocks, 0)),
        ],
        out_specs=[
            pl.BlockSpec((tm, AB_COLS), row),
            pl.BlockSpec((B_WIDTH, tm), col),
            pl.BlockSpec((B_WIDTH, tm), col),
            pl.BlockSpec((C_COLS, tm), col),
            pl.BlockSpec((tm, C_COLS), row),
            pl.BlockSpec((C_WIDTH, tm), col),
        ],
        out_shape=[
            jax.ShapeDtypeStruct((n, AB_COLS), BF16),
            jax.ShapeDtypeStruct((B_WIDTH, n), BF16),
            jax.ShapeDtypeStruct((B_WIDTH, n), BF16),
            jax.ShapeDtypeStruct((C_COLS, n), BF16),
            jax.ShapeDtypeStruct((n, C_COLS), BF16),
            jax.ShapeDtypeStruct((C_WIDTH, n), BF16),
        ],
        compiler_params=_params(1),
        name="input_stage",
    )(x2, g, wab, wcq, gq, wuq, wckv, gkv, wuk, wuv, wkr, cos_t, sin_t)


def _dot_tn(a, b):
    return lax.dot_general(a, b, (((0,), (0,)), ((), ())), preferred_element_type=F32)


def _dilated_kernel(q_ref, k_ref, v_ref, bias_ref, y_ref, qf_ref, kf_ref, vf_ref, m_ref, w_ref, a_ref):
    seq = q_ref.shape[0]
    chunk = 512

    def widen(c, carry):
        rows = pl.ds(pl.multiple_of(c * chunk, chunk), chunk)
        qf_ref[rows, :] = q_ref[rows, :].astype(F32)
        kf_ref[rows, :] = k_ref[rows, :].astype(F32)
        vf_ref[rows, :] = v_ref[rows, :].astype(F32)
        return carry

    lax.fori_loop(0, seq // chunk, widen, 0)

    tile = (A_BLOCK, A_BLOCK)
    low_lane = lax.broadcasted_iota(jnp.int32, tile, 1) < HEAD_DIM
    low_row = lax.broadcasted_iota(jnp.int32, tile, 0) < HEAD_DIM
    last = len(A_PATTERNS) - 1
    for g, (_, dil) in enumerate(A_PATTERNS):
        nb = seq // dil // A_BLOCK
        span = dil * A_BLOCK

        def attend(blk, g=g, dil=dil, nb=nb, span=span):
            r = blk // nb
            n = blk % nb
            base = r + n * span
            prev_base = jnp.maximum(base - span, r)
            rows = pl.ds(base, A_BLOCK, stride=dil)
            prev_rows = pl.ds(prev_base, A_BLOCK, stride=dil)
            q = qf_ref[rows, :]
            k2 = jnp.concatenate([kf_ref[prev_rows, :], kf_ref[rows, :]], axis=0).astype(BF16)
            v2 = jnp.concatenate([vf_ref[prev_rows, :], vf_ref[rows, :]], axis=0).astype(BF16)
            first_block = jnp.where(n == 0, 1, 0)
            outs, lses = [], []
            for hh in range(2):
                qm = jnp.where(low_lane if hh == 0 else ~low_lane, q, 0.0).astype(BF16)
                s = _dot_nt(k2, qm) + bias_ref[0, g, first_block, hh]
                m = jnp.max(s, axis=0, keepdims=True)
                p = jnp.exp2(s - m)
                l = jnp.sum(p, axis=0, keepdims=True)
                outs.append(_dot_tn(v2, p.astype(BF16)) * (1.0 / l))
                lses.append(jnp.broadcast_to(m + jnp.log2(l), tile))
            o_blk = jnp.where(low_row, outs[0], outs[1]).T
            lse_blk = jnp.where(low_row, lses[0], lses[1]).T
            return rows, o_blk, lse_blk

        def merge(rows, o_blk, lse_blk, g=g):
            if g == 0:
                m_ref[rows, :] = lse_blk
                a_ref[rows, :] = o_blk
                return
            m_old = m_ref[rows, :]
            m_new = jnp.maximum(m_old, lse_blk)
            c_old = jnp.exp2(m_old - m_new)
            c_blk = jnp.exp2(lse_blk - m_new)
            w_new = c_blk + (c_old if g == 1 else c_old * w_ref[rows, :])
            a_new = c_old * a_ref[rows, :] + c_blk * o_blk
            if g < last:
                m_ref[rows, :] = m_new
                w_ref[rows, :] = w_new
                a_ref[rows, :] = a_new
            else:
                a_ref[rows, :] = a_new / w_new

        def step(it, carry, attend=attend, merge=merge):
            done = [attend(it * A_BLOCKS_PER_STEP + u) for u in range(A_BLOCKS_PER_STEP)]
            for args in done:
                merge(*args)
            return carry

        lax.fori_loop(0, dil * nb // A_BLOCKS_PER_STEP, step, 0)

    def narrow(c, carry):
        rows = pl.ds(pl.multiple_of(c * chunk, chunk), chunk)
        y_ref[rows, :] = a_ref[rows, :].astype(BF16)
        return carry

    lax.fori_loop(0, seq // chunk, narrow, 0)


def _dilated_attention(qkv, bias_tiles, batch, seq):
    pairs = A_HEADS // 2
    col = lambda off: (lambda b, p: (b, off * pairs + p))
    state = pltpu.VMEM((seq, 128), F32)
    return pl.pallas_call(
        _dilated_kernel,
        grid=(batch, pairs),
        in_specs=[
            pl.BlockSpec((seq, 128), col(0)),
            pl.BlockSpec((seq, 128), col(1)),
            pl.BlockSpec((seq, 128), col(2)),
            pl.BlockSpec((1, len(A_PATTERNS), 2, 2, 2 * A_BLOCK, A_BLOCK), lambda b, p: (p, 0, 0, 0, 0, 0)),
        ],
        out_specs=pl.BlockSpec((seq, 128), lambda b, p: (b, p)),
        out_shape=jax.ShapeDtypeStruct((batch * seq, A_WIDTH), BF16),
        scratch_shapes=[state] * 6,
        compiler_params=_params(2),
        name="dilated_attention",
    )(qkv, qkv, qkv, bias_tiles)


def _flash_init(m_ref, acc_ref):
    m_ref[...] = jnp.full(m_ref.shape, NEG, F32)
    acc_ref[...] = jnp.zeros(acc_ref.shape, F32)


def _with_ones(v_t):
    return jnp.concatenate([v_t, jnp.ones((ONES_ROWS, v_t.shape[1]), v_t.dtype)], axis=0)


def _key_rows(part):
    return slice(part * KEY_PART, (part + 1) * KEY_PART)


def _store_scores(s, slot, idx, part, s_ref, cmax_ref):
    s_ref[slot, idx, _key_rows(part), :] = s
    part_max = jnp.max(s, axis=0, keepdims=True)
    cmax_ref[slot, idx] = part_max if part == 0 else jnp.maximum(cmax_ref[slot, idx], part_max)


def _masked_part(slot, idx, part, masked, s_ref):
    s = s_ref[slot, idx, _key_rows(part), :]
    if masked:
        key = lax.broadcasted_iota(jnp.int32, s.shape, 0) + part * KEY_PART
        s = jnp.where(key <= lax.broadcasted_iota(jnp.int32, s.shape, 1), s, NEG)
    return s


def _flash_begin(slot, idx, masked, m_ref, s_ref, cmax_ref):
    if masked:
        col_max = functools.reduce(jnp.maximum, [
            jnp.max(_masked_part(slot, idx, part, True, s_ref), axis=0, keepdims=True)
            for part in range(ATT_TILE // KEY_PART)])
    else:
        col_max = cmax_ref[slot, idx]
    m_prev = m_ref[idx]
    m_new = jnp.maximum(m_prev, col_max)
    m_ref[idx] = m_new
    return m_new, jnp.exp2(m_prev - m_new)


def _flash_part(slot, idx, part, masked, m_new, v_ext, s_ref):
    p = jnp.exp2(_masked_part(slot, idx, part, masked, s_ref) - m_new)
    return _dot(v_ext[:, _key_rows(part)], p.astype(BF16))


def _flash_result(idx, dv, acc_ref):
    acc = acc_ref[idx]
    return acc[:dv] * (1.0 / acc[dv:dv + 1])


def _flash_sweep(n_tiles, scores, absorb, finish, m_ref, acc_ref, s_ref, cmax_ref):
    n_items = n_tiles * (n_tiles + 1) // 2
    assert n_items % 2 == 0

    def following(i, c):
        end = c == i
        return jnp.where(end, i + 1, i), jnp.where(end, 0, c + 1)

    parts = ATT_TILE // KEY_PART

    def item(nxt, cur, diag, slot):
        for idx in range(2):
            m_new, alpha = _flash_begin(slot, idx, diag, m_ref, s_ref, cmax_ref)
            pv = None
            for part in range(parts):
                scores(*nxt, 1 - slot, idx, part)
                piece = absorb(cur, slot, idx, part, diag, m_new)
                pv = piece if pv is None else pv + piece
            acc_ref[idx] = alpha * acc_ref[idx] + pv

    for idx in range(2):
        for part in range(parts):
            scores(0, 0, 0, idx, part)

    def two_items(u, carry):
        i0, c0 = carry
        i1, c1 = following(i0, c0)
        i2, c2 = following(i1, c1)
        i2s = jnp.minimum(i2, n_tiles - 1)

        def variant(diag0, diag1):
            def run():
                item((i1, c1), c0, diag0, 0)
                if diag0:
                    finish(i0)
                item((i2s, c2), c1, diag1, 1)
                if diag1:
                    finish(i1)
            return run

        kind = 2 * (c0 == i0).astype(jnp.int32) + (c1 == i1).astype(jnp.int32)
        lax.switch(kind, [variant(False, False), variant(False, True),
                          variant(True, False), variant(True, True)])
        return i2, c2

    lax.fori_loop(0, n_items // 2, two_items, (jnp.int32(0), jnp.int32(0)))


def _row_in_ranges(shape, ranges):
    row = lax.broadcasted_iota(jnp.int32, shape, 0)
    hit = None
    for lo, hi in ranges:
        r = (row >= lo) & (row < hi)
        hit = r if hit is None else hit | r
    return hit


def _diff_bias(bias_ref, d, part):
    assert KEY_PART == BIAS_TILE
    sub = ATT_TILE // BIAS_TILE
    tiles = [bias_ref[0, jnp.clip(sub * d + qa - part, 0, B_BIAS_TILES - 1)] for qa in range(sub)]
    return jnp.concatenate(tiles, axis=1)


def _tile_cols(i):
    return pl.ds(pl.multiple_of(i * ATT_TILE, ATT_TILE), ATT_TILE)


def _diff_kernel(lam_init, qt_ref, k_ref, vt_ref, bias_ref, lam_ref, g_ref, y_ref,
                 qm_ref, m_ref, acc_ref, s_ref, cmax_ref):
    t = ATT_TILE
    n_tiles = qt_ref.shape[1] // t

    def split_queries(i, carry):
        qf = qt_ref[:, _tile_cols(i)].astype(F32)
        first = lax.broadcasted_iota(jnp.int32, qf.shape, 0) < B_QK_DIM
        qm_ref[0, :, _tile_cols(i)] = jnp.where(first, qf, 0.0).astype(BF16)
        qm_ref[1, :, _tile_cols(i)] = jnp.where(first, 0.0, qf).astype(BF16)
        return carry

    lax.fori_loop(0, n_tiles, split_queries, 0)
    lv = lam_ref[...]
    lam = (jnp.exp(jnp.sum(lv[0:1] * lv[1:2], axis=-1, keepdims=True))
           - jnp.exp(jnp.sum(lv[2:3] * lv[3:4], axis=-1, keepdims=True)) + lam_init)
    _flash_init(m_ref, acc_ref)

    def scores(i, c, slot, idx, part):
        k = k_ref[pl.ds(pl.multiple_of(c * t + part * KEY_PART, KEY_PART), KEY_PART), :]
        s = _dot(k, qm_ref[idx, :, _tile_cols(i)]) + _diff_bias(bias_ref, i - c, part)
        _store_scores(s, slot, idx, part, s_ref, cmax_ref)

    def absorb(c, slot, idx, part, masked, m_new):
        v_ext = _with_ones(vt_ref[:, _tile_cols(c)])
        return _flash_part(slot, idx, part, masked, m_new, v_ext, s_ref)

    def finish(i):
        o = _flash_result(0, B_V_DIM, acc_ref) - lam * _flash_result(1, B_V_DIM, acc_ref)
        y = o * lax.rsqrt(jnp.mean(o * o, axis=0, keepdims=True) + EPS) * g_ref[...] * (1.0 - lam_init)
        y_ref[_tile_cols(i), :] = y.T.astype(BF16)
        _flash_init(m_ref, acc_ref)

    _flash_sweep(n_tiles, scores, absorb, finish, m_ref, acc_ref, s_ref, cmax_ref)


def _diff_attention(qb_t, qkv, vb_t, bias_tiles, lam_vecs, g_col, batch, seq, lam_init):
    t = ATT_TILE
    k_col0 = KB_COL0 // 128
    return pl.pallas_call(
        functools.partial(_diff_kernel, lam_init),
        grid=(batch, B_HEADS),
        in_specs=[
            pl.BlockSpec((128, seq), lambda b, h: (h, b)),
            pl.BlockSpec((seq, 128), lambda b, h: (b, k_col0 + h)),
            pl.BlockSpec((128, seq), lambda b, h: (h, b)),
            pl.BlockSpec((1, B_BIAS_TILES, BIAS_TILE, BIAS_TILE), lambda b, h: (h, 0, 0, 0)),
            _const_spec((4, B_QK_DIM)),
            _const_spec((B_V_DIM, 1)),
        ],
        out_specs=pl.BlockSpec((seq, 128), lambda b, h: (b, h)),
        out_shape=jax.ShapeDtypeStruct((batch * seq, B_WIDTH), BF16),
        scratch_shapes=[
            pltpu.VMEM((2, 128, seq), BF16),
            pltpu.VMEM((2, 1, t), F32),
            pltpu.VMEM((2, B_V_DIM + ONES_ROWS, t), F32),
            pltpu.VMEM((2, 2, t, t), F32),
            pltpu.VMEM((2, 2, 1, t), F32),
        ],
        compiler_params=_params(2),
        name="diff_attention",
    )(qb_t, qkv, vb_t, bias_tiles, lam_vecs, g_col)


def _mla_kernel(qt_ref, k_ref, vt_ref, y_ref, qm_ref, m_ref, acc_ref, s_ref, cmax_ref):
    t = ATT_TILE
    n_tiles = qt_ref.shape[1] // t

    def split_heads(i, carry):
        qf = qt_ref[:, _tile_cols(i)].astype(F32)
        qm_ref[0, :, _tile_cols(i)] = jnp.where(_row_in_ranges(qf.shape, C_HEAD_A_RANGES), qf, 0.0).astype(BF16)
        qm_ref[1, :, _tile_cols(i)] = jnp.where(_row_in_ranges(qf.shape, C_HEAD_B_RANGES), qf, 0.0).astype(BF16)
        return carry

    lax.fori_loop(0, n_tiles, split_heads, 0)
    _flash_init(m_ref, acc_ref)

    def scores(i, c, slot, idx, part):
        k = k_ref[pl.ds(pl.multiple_of(c * t + part * KEY_PART, KEY_PART), KEY_PART), :]
        _store_scores(_dot(k, qm_ref[idx, :, _tile_cols(i)]), slot, idx, part, s_ref, cmax_ref)

    def absorb(c, slot, idx, part, masked, m_new):
        v_t = vt_ref[idx * C_V:(idx + 1) * C_V, _tile_cols(c)]
        return _flash_part(slot, idx, part, masked, m_new, _with_ones(v_t), s_ref)

    def finish(i):
        o = jnp.concatenate([_flash_result(0, C_V, acc_ref), _flash_result(1, C_V, acc_ref)], axis=0)
        y_ref[_tile_cols(i), :] = o.T.astype(BF16)
        _flash_init(m_ref, acc_ref)

    _flash_sweep(n_tiles, scores, absorb, finish, m_ref, acc_ref, s_ref, cmax_ref)


def _mla_attention(qc_t, kc, vc_t, batch, seq):
    t = ATT_TILE
    return pl.pallas_call(
        _mla_kernel,
        grid=(batch, C_PAIRS),
        in_specs=[
            pl.BlockSpec((C_PAIR_LANES, seq), lambda b, j: (j, b)),
            pl.BlockSpec((seq, C_PAIR_LANES), lambda b, j: (b, j)),
            pl.BlockSpec((2 * C_V, seq), lambda b, j: (j, b)),
        ],
        out_specs=pl.BlockSpec((seq, 2 * C_V), lambda b, j: (b, j)),
        out_shape=jax.ShapeDtypeStruct((batch * seq, C_WIDTH), BF16),
        scratch_shapes=[
            pltpu.VMEM((2, C_PAIR_LANES, seq), BF16),
            pltpu.VMEM((2, 1, t), F32),
            pltpu.VMEM((2, C_V + ONES_ROWS, t), F32),
            pltpu.VMEM((2, 2, t, t), F32),
            pltpu.VMEM((2, 2, 1, t), F32),
        ],
        compiler_params=_params(2),
        name="mla_attention",
    )(qc_t, kc, vc_t)


def _sigmoid(z):
    return 1.0 / (1.0 + jnp.exp(-z))


def _merge_kernel(x_ref, g_ref, ya_ref, yb_ref, yc_ref, wg_ref, bg_ref, wa_ref, wb_ref, wc_ref,
                  wo_ref, out_ref):
    x = x_ref[...]
    h = _rms(x, g_ref[...]).astype(BF16)
    merged = jnp.zeros(x.shape, F32)
    for k, (y_ref, w_ref) in enumerate(((ya_ref, wa_ref), (yb_ref, wb_ref), (yc_ref, wc_ref))):
        cols = slice(k * D_MODEL, (k + 1) * D_MODEL)
        gate = _sigmoid(_dot(h, wg_ref[:, cols]) + bg_ref[:, cols])
        merged = merged + gate * _dot(y_ref[...], w_ref[...])
    out_ref[...] = x + _dot(merged.astype(BF16), wo_ref[...])


def _merge_stage(x2, g, ya, yb, yc, wg, bg, wa, wb, wc, wo):
    n = x2.shape[0]
    tm = ROW_TILE
    row = lambda i: (i, 0)
    return pl.pallas_call(
        _merge_kernel,
        grid=(n // tm,),
        in_specs=[
            pl.BlockSpec((tm, D_MODEL), row),
            _const_spec((1, D_MODEL)),
            pl.BlockSpec((tm, A_WIDTH), row),
            pl.BlockSpec((tm, B_WIDTH), row),
            pl.BlockSpec((tm, C_WIDTH), row),
            _const_spec((D_MODEL, N_BRANCH * D_MODEL)),
            _const_spec((1, N_BRANCH * D_MODEL)),
            _const_spec((A_WIDTH, D_MODEL)),
            _const_spec((B_WIDTH, D_MODEL)),
            _const_spec((C_WIDTH, D_MODEL)),
            _const_spec((D_MODEL, D_MODEL)),
        ],
        out_specs=pl.BlockSpec((tm, D_MODEL), row),
        out_shape=jax.ShapeDtypeStruct((n, D_MODEL), F32),
        compiler_params=_params(1),
        name="gated_merge",
    )(x2, g, ya, yb, yc, wg, bg, wa, wb, wc, wo)


def _ffn_kernel(final_norm, x_ref, g_ref, wg_ref, wu_ref, wd_ref, gf_ref, out_ref):
    x = x_ref[...]
    h = _rms(x, g_ref[...]).astype(BF16)
    acc = jnp.zeros(x.shape, F32)
    for c in range(0, FFN_HIDDEN, FFN_CHUNK):
        cols = slice(c, c + FFN_CHUNK)
        gate = _dot(h, wg_ref[:, cols])
        act = gate * _sigmoid(gate) * _dot(h, wu_ref[:, cols])
        acc = acc + _dot(act.astype(BF16), wd_ref[cols, :])
    y = x + acc
    if final_norm:
        y = _rms(y, gf_ref[...])
    out_ref[...] = y


def _ffn_stage(x2, g, wg, wu, wd, gf, final_norm):
    n = x2.shape[0]
    tm = ROW_TILE
    row = lambda i: (i, 0)
    return pl.pallas_call(
        functools.partial(_ffn_kernel, final_norm),
        grid=(n // tm,),
        in_specs=[
            pl.BlockSpec((tm, D_MODEL), row),
            _const_spec((1, D_MODEL)),
            _const_spec((D_MODEL, FFN_HIDDEN)),
            _const_spec((D_MODEL, FFN_HIDDEN)),
            _const_spec((FFN_HIDDEN, D_MODEL)),
            _const_spec((1, D_MODEL)),
        ],
        out_specs=pl.BlockSpec((tm, D_MODEL), row),
        out_shape=jax.ShapeDtypeStruct((n, D_MODEL), F32),
        compiler_params=_params(1),
        name="swiglu_final" if final_norm else "swiglu",
    )(x2, g, wg, wu, wd, gf)


def _gather_cols(w, idx):
    padded = jnp.concatenate([w, jnp.zeros((w.shape[0], 1), w.dtype)], axis=1)
    return padded[:, idx]


def kernel(x, rel_bias_table, ln_mix_g, w_in, lambda_q1, lambda_k1, lambda_q2, lambda_k2, diff_subln_g, mla_q_norm_g, w_uq, mla_kv_norm_g, w_ukv, w_gate, b_gate, w_br_a, w_br_b, w_br_c, w_o, ln_ffn_g, w_ffn_gate, w_ffn_up, w_ffn_down, final_norm_g):
    batch, seq, _ = x.shape
    depth = w_in.shape[0]
    assert all(seq % (A_BLOCK * dil) == 0 for _, dil in A_PATTERNS)
    assert (seq // A_BLOCK) % A_BLOCKS_PER_STEP == 0
    assert seq % ATT_TILE == 0 and seq % ROW_TILE == 0
    lay = _C_LAYOUT
    cos_t, sin_t = _rope_tables(seq)

    a_tiles = _bias_expand(rel_bias_table, _a_bucket_index(), A_HEADS, 0, 2 * A_BLOCK, "bias_expand_a",
                           gain=LOG2E)
    a_tiles = a_tiles.reshape(A_HEADS // 2, 2, len(A_PATTERNS), 2, 2 * A_BLOCK, A_BLOCK)
    a_tiles = jnp.transpose(a_tiles, (0, 2, 3, 1, 4, 5))
    b_tiles = _bias_expand(rel_bias_table, _b_bucket_index(), B_HEADS, A_HEADS, BIAS_TILE,
                           "bias_expand_b", gain=LOG2E).reshape(B_HEADS, B_BIAS_TILES, BIAS_TILE, BIAS_TILE)

    cq0 = AB_COLS
    ckv0 = cq0 + C_Q_RANK
    kr0 = ckv0 + C_KV_RANK
    x2 = x.reshape(batch * seq, D_MODEL)
    for l in range(depth):
        w_in_l = w_in[l]
        qkv, qb_t, vb_t, qc_t, kc, vc_t = _input_stage(
            x2, ln_mix_g[l][None, :],
            w_in_l[:, :AB_COLS].astype(BF16),
            w_in_l[:, cq0:ckv0].astype(BF16), mla_q_norm_g[l][None, :],
            _gather_cols(w_uq[l], lay["q"]).astype(BF16),
            w_in_l[:, ckv0:kr0].astype(BF16), mla_kv_norm_g[l][None, :],
            _gather_cols(w_ukv[l], lay["k"]).astype(BF16),
            w_ukv[l][:, lay["v"]].astype(BF16),
            _gather_cols(w_in_l[:, kr0:], lay["kr"]).astype(BF16),
            cos_t, sin_t, seq)

        ya = _dilated_attention(qkv, a_tiles, batch, seq)

        lam_init = 0.8 - 0.6 * math.exp(-0.3 * l)
        lam_vecs = jnp.stack([lambda_q1[l], lambda_k1[l], lambda_q2[l], lambda_k2[l]]).astype(F32)
        yb = _diff_attention(qb_t, qkv, vb_t, b_tiles, lam_vecs, diff_subln_g[l][:, None],
                             batch, seq, lam_init)
        yc = _mla_attention(qc_t, kc, vc_t, batch, seq)

        x2 = _merge_stage(x2, ln_mix_g[l][None, :], ya, yb, yc,
                          w_gate[l].astype(BF16), b_gate[l][None, :],
                          w_br_a[l].astype(BF16), w_br_b[l].astype(BF16), w_br_c[l].astype(BF16),
                          w_o[l].astype(BF16))
        x2 = _ffn_stage(x2, ln_ffn_g[l][None, :], w_ffn_gate[l].astype(BF16),
                        w_ffn_up[l].astype(BF16), w_ffn_down[l].astype(BF16),
                        final_norm_g[None, :], final_norm=(l == depth - 1))
    return x2.reshape(batch, seq, D_MODEL)
```

```python
import functools
import math

import numpy as np
import jax
import jax.numpy as jnp
from jax import lax
from jax.experimental import pallas as pl
from jax.experimental.pallas import tpu as pltpu

D_MODEL = 1024
HEAD_DIM = 64
A_HEADS = 8
A_PATTERNS = ((2048, 16), (512, 4), (128, 1))
A_WIDTH = A_HEADS * HEAD_DIM
A_BLOCK = 128
A_BLOCKS_PER_STEP = 16
B_HEADS = 4
B_QK_DIM = HEAD_DIM
B_V_DIM = 2 * HEAD_DIM
B_WIDTH = B_HEADS * B_V_DIM
C_HEADS = 8
C_PAIRS = C_HEADS // 2
C_Q_RANK = 768
C_KV_RANK = 256
C_NOPE = 64
C_ROPE = 32
C_V = 64
C_WIDTH = C_HEADS * C_V
C_PAIR_LANES = 256
C_COLS = C_PAIRS * C_PAIR_LANES
ROPE_THETA = 10000.0
REL_BUCKETS = 32
REL_MAX_DIST = 2048
AB_COLS = 3 * A_WIDTH + 4 * B_HEADS * B_QK_DIM + B_WIDTH
QB_COL0 = 3 * A_WIDTH
KB_COL0 = QB_COL0 + 2 * B_HEADS * B_QK_DIM
VB_COL0 = KB_COL0 + 2 * B_HEADS * B_QK_DIM
N_BRANCH = 3
FFN_HIDDEN = 2816
EPS = 1e-6
NEG = -1e30
LOG2E = math.log2(math.e)
A_SCORE_SCALE = HEAD_DIM ** -0.5 * LOG2E
B_SCORE_SCALE = B_QK_DIM ** -0.5 * LOG2E
C_SCORE_SCALE = (C_NOPE + C_ROPE) ** -0.5 * LOG2E
ONES_ROWS = 16

ROW_TILE = 512
ATT_TILE = 512
KEY_PART = 256
BIAS_TILE = 256
B_BIAS_TILES = 8
FFN_CHUNK = 256
VMEM_LIMIT = 56 * 1024 * 1024

F32 = jnp.float32
BF16 = jnp.bfloat16


def _dot(a, b):
    return jnp.dot(a, b, preferred_element_type=F32)


def _dot_nt(a, b):
    return lax.dot_general(a, b, (((1,), (1,)), ((), ())), preferred_element_type=F32)


def _rms(x, g):
    return x * lax.rsqrt(jnp.mean(x * x, axis=-1, keepdims=True) + EPS) * g


def _swap_halves(t):
    half = t.shape[1] // 2
    return jnp.concatenate([t[:, half:], t[:, :half]], axis=1)


def _const_spec(shape):
    nd = len(shape)
    return pl.BlockSpec(shape, lambda *_: (0,) * nd, pipeline_mode=pl.Buffered(1))


def _params(n_axes):
    return pltpu.CompilerParams(dimension_semantics=("parallel",) * n_axes,
                                vmem_limit_bytes=VMEM_LIMIT)


def _t5_bucket_np(dist):
    dist = np.maximum(dist, 0)
    exact = REL_BUCKETS // 2
    ratio = np.maximum(dist, 1).astype(np.float32) / np.float32(exact)
    log_ratio = np.log(ratio).astype(np.float32) / np.float32(math.log(REL_MAX_DIST / exact))
    large = np.minimum(exact + (log_ratio * np.float32(REL_BUCKETS - exact)).astype(np.int32),
                       REL_BUCKETS - 1)
    return np.where(dist < exact, dist, large).astype(np.int32)


MASKED_BUCKET = REL_BUCKETS


def _a_bucket_index():
    kj = np.arange(2 * A_BLOCK)[:, None]
    qi = np.arange(A_BLOCK)[None, :]
    step = qi + A_BLOCK - kj
    in_band = (step >= 0) & (step <= A_BLOCK)
    tiles = []
    for _, dil in A_PATTERNS:
        bucket = _t5_bucket_np(step * dil)
        for first_block in (False, True):
            valid = in_band & (kj >= A_BLOCK) if first_block else in_band
            tiles.append(np.where(valid, bucket, MASKED_BUCKET))
    return np.concatenate(tiles, axis=0).astype(np.int32)


def _b_bucket_index():
    kj = np.arange(BIAS_TILE)[:, None]
    qi = np.arange(BIAS_TILE)[None, :]
    e = np.arange(B_BIAS_TILES)[:, None, None]
    idx = _t5_bucket_np(e * BIAS_TILE + qi - kj)
    assert (idx[-1] == REL_BUCKETS - 1).all()
    return idx.reshape(B_BIAS_TILES * BIAS_TILE, BIAS_TILE)


def _c_layout():
    hq = C_NOPE + C_ROPE
    q_zero = C_HEADS * hq
    kv_zero = C_HEADS * (C_NOPE + C_V)
    half = C_ROPE // 2
    q_idx = np.full((C_PAIRS, C_PAIR_LANES), q_zero, np.int32)
    k_idx = np.full((C_PAIRS, C_PAIR_LANES), kv_zero, np.int32)
    for j in range(C_PAIRS):
        a, b = 2 * j, 2 * j + 1
        q_idx[j, 0:64] = a * hq + np.arange(64)
        q_idx[j, 64:80] = a * hq + C_NOPE + np.arange(half)
        q_idx[j, 80:96] = b * hq + C_NOPE + np.arange(half)
        q_idx[j, 128:192] = b * hq + np.arange(64)
        q_idx[j, 192:208] = a * hq + C_NOPE + half + np.arange(half)
        q_idx[j, 208:224] = b * hq + C_NOPE + half + np.arange(half)
        k_idx[j, 0:64] = a * (C_NOPE + C_V) + np.arange(64)
        k_idx[j, 128:192] = b * (C_NOPE + C_V) + np.arange(64)
    v_idx = (np.arange(C_HEADS)[:, None] * (C_NOPE + C_V) + C_NOPE + np.arange(C_V)[None, :])
    kr_idx = np.full((C_PAIR_LANES,), C_ROPE, np.int32)
    kr_idx[64:80] = np.arange(half)
    kr_idx[80:96] = np.arange(half)
    kr_idx[192:208] = half + np.arange(half)
    kr_idx[208:224] = half + np.arange(half)
    rope_lane = np.full((C_PAIR_LANES,), -1, np.int32)
    for start in (64, 80, 192, 208):
        rope_lane[start:start + half] = np.arange(half)
    sin_sign = np.zeros((C_PAIR_LANES,), np.float32)
    sin_sign[64:96] = -1.0
    sin_sign[192:224] = 1.0
    return dict(q=q_idx.reshape(-1), k=k_idx.reshape(-1), v=v_idx.reshape(-1), kr=kr_idx,
                rope_lane=rope_lane, sin_sign=sin_sign)


_C_LAYOUT = _c_layout()
C_HEAD_A_RANGES = ((0, 80), (192, 208))
C_HEAD_B_RANGES = ((80, 96), (128, 192), (208, 224))


def _rope_tables(seq):
    pos = jnp.arange(seq, dtype=F32)
    inv_freq = ROPE_THETA ** (-jnp.arange(0, C_ROPE, 2, dtype=F32) / C_ROPE)
    ang = pos[:, None] * inv_freq[None, :]
    cos, sin = jnp.cos(ang), jnp.sin(ang)
    lane = _C_LAYOUT["rope_lane"]
    is_rope = jnp.asarray(lane >= 0)
    gather = np.maximum(lane, 0)
    cos_t = jnp.where(is_rope[None, :], cos[:, gather], 1.0)
    sin_t = sin[:, gather] * jnp.asarray(_C_LAYOUT["sin_sign"])[None, :]
    return cos_t, sin_t


def _bias_expand_kernel(n_heads, head0, gain, table_ref, idx_ref, out_ref):
    idx = idx_ref[...]
    hit = [idx == b for b in range(1, REL_BUCKETS)]
    masked = idx == MASKED_BUCKET
    for h in range(n_heads):
        acc = jnp.full(idx.shape, table_ref[0, head0 + h] * gain, F32)
        for b in range(1, REL_BUCKETS):
            acc = jnp.where(hit[b - 1], table_ref[b, head0 + h] * gain, acc)
        out_ref[h] = jnp.where(masked, NEG, acc)


def _bias_expand(table, idx, n_heads, head0, block_rows, name, gain=1.0):
    rows, cols = idx.shape
    return pl.pallas_call(
        functools.partial(_bias_expand_kernel, n_heads, head0, gain),
        grid=(rows // block_rows,),
        in_specs=[
            pl.BlockSpec(memory_space=pltpu.SMEM),
            pl.BlockSpec((block_rows, cols), lambda i: (i, 0)),
        ],
        out_specs=pl.BlockSpec((n_heads, block_rows, cols), lambda i: (0, i, 0)),
        out_shape=jax.ShapeDtypeStruct((n_heads, rows, cols), F32),
        compiler_params=_params(1),
        name=name,
    )(table, jnp.asarray(idx))


def _input_kernel(x_ref, g_ref, wab_ref, wcq_ref, gq_ref, wuq_ref, wckv_ref, gkv_ref, wuk_ref,
                  wuv_ref, wkr_ref, cos_ref, sin_ref,
                  qkva_ref, kb_ref, qbt_ref, vbt_ref, qct_ref, kc_ref, vct_ref):
    h = _rms(x_ref[...], g_ref[...]).astype(BF16)
    cos = cos_ref[...]
    sin = sin_ref[...]

    def project(c):
        t = _dot(h, wab_ref[:, c:c + 512])
        if c < QB_COL0:
            qkva_ref[:, c:c + 512] = (t * A_SCORE_SCALE if c == 0 else t).astype(BF16)
        elif c == QB_COL0:
            qbt_ref[...] = (t * B_SCORE_SCALE).T.astype(BF16)
        elif c == KB_COL0:
            kb_ref[...] = t.astype(BF16)
        else:
            vbt_ref[...] = t.T.astype(BF16)

    chunks = iter(range(0, AB_COLS, 512))
    cq = _dot(h, wcq_ref[...])
    ckv = _dot(h, wckv_ref[...])
    kr = _dot(h, wkr_ref[...])
    project(next(chunks))
    cqn = _rms(cq, gq_ref[...]).astype(BF16)
    project(next(chunks))
    ckvn = _rms(ckv, gkv_ref[...]).astype(BF16)
    kr_rot = kr * cos + _swap_halves(kr) * sin
    for j in range(C_PAIRS):
        cols = slice(j * C_PAIR_LANES, (j + 1) * C_PAIR_LANES)
        t = _dot(cqn, wuq_ref[:, cols])
        qct_ref[cols, :] = ((t * cos + _swap_halves(t) * sin) * C_SCORE_SCALE).T.astype(BF16)
        project(next(chunks))
    for j in range(C_PAIRS):
        cols = slice(j * C_PAIR_LANES, (j + 1) * C_PAIR_LANES)
        kc_ref[:, cols] = (_dot(ckvn, wuk_ref[:, cols]) + kr_rot).astype(BF16)
    vct_ref[...] = _dot(ckvn, wuv_ref[...]).T.astype(BF16)
    assert next(chunks, None) is None


def _input_stage(x2, g, wab, wcq, gq, wuq, wckv, gkv, wuk, wuv, wkr, cos_t, sin_t, seq):
    n = x2.shape[0]
    tm = ROW_TILE
    pos_blocks = seq // tm
    row = lambda i: (i, 0)
    col = lambda i: (0, i)
    return pl.pallas_call(
        _input_kernel,
        grid=(n // tm,),
        in_specs=[
            pl.BlockSpec((tm, D_MODEL), row),
            _const_spec((1, D_MODEL)),
            _const_spec((D_MODEL, AB_COLS)),
            _const_spec((D_MODEL, C_Q_RANK)),
            _const_spec((1, C_Q_RANK)),
            _const_spec((C_Q_RANK, C_COLS)),
            _const_spec((D_MODEL, C_KV_RANK)),
            _const_spec((1, C_KV_RANK)),
            _const_spec((C_KV_RANK, C_COLS)),
            _const_spec((C_KV_RANK, C_WIDTH)),
            _const_spec((D_MODEL, C_PAIR_LANES)),
            pl.BlockSpec((tm, C_PAIR_LANES), lambda i: (i % pos_blocks, 0)),
            pl.BlockSpec((tm, C_PAIR_LANES), lambda i: (i % pos_blocks, 0)),
        ],
        out_specs=[
            pl.BlockSpec((tm, QB_COL0), row),
            pl.BlockSpec((tm, VB_COL0 - KB_COL0), row),
            pl.BlockSpec((B_WIDTH, tm), col),
            pl.BlockSpec((B_WIDTH, tm), col),
            pl.BlockSpec((C_COLS, tm), col),
            pl.BlockSpec((tm, C_COLS), row),
            pl.BlockSpec((C_WIDTH, tm), col),
        ],
        out_shape=[
            jax.ShapeDtypeStruct((n, QB_COL0), BF16),
            jax.ShapeDtypeStruct((n, VB_COL0 - KB_COL0), BF16),
            jax.ShapeDtypeStruct((B_WIDTH, n), BF16),
            jax.ShapeDtypeStruct((B_WIDTH, n), BF16),
            jax.ShapeDtypeStruct((C_COLS, n), BF16),
            jax.ShapeDtypeStruct((n, C_COLS), BF16),
            jax.ShapeDtypeStruct((C_WIDTH, n), BF16),
        ],
        compiler_params=_params(1),
        name="input_stage",
    )(x2, g, wab, wcq, gq, wuq, wckv, gkv, wuk, wuv, wkr, cos_t, sin_t)


def _dot_tn(a, b):
    return lax.dot_general(a, b, (((0,), (0,)), ((), ())), preferred_element_type=F32)


def _dilated_kernel(q_ref, k_ref, v_ref, bias_ref, y_ref, qf_ref, kf_ref, vf_ref, m_ref, w_ref, a_ref):
    seq = q_ref.shape[0]
    chunk = 512

    def widen(c, carry):
        rows = pl.ds(pl.multiple_of(c * chunk, chunk), chunk)
        qf_ref[rows, :] = q_ref[rows, :].astype(F32)
        kf_ref[rows, :] = k_ref[rows, :].astype(F32)
        vf_ref[rows, :] = v_ref[rows, :].astype(F32)
        return carry

    lax.fori_loop(0, seq // chunk, widen, 0)

    tile = (A_BLOCK, A_BLOCK)
    low_lane = lax.broadcasted_iota(jnp.int32, tile, 1) < HEAD_DIM
    low_row = lax.broadcasted_iota(jnp.int32, tile, 0) < HEAD_DIM
    last = len(A_PATTERNS) - 1
    for g, (_, dil) in enumerate(A_PATTERNS):
        nb = seq // dil // A_BLOCK
        span = dil * A_BLOCK

        def attend(blk, g=g, dil=dil, nb=nb, span=span):
            r = blk // nb
            n = blk % nb
            base = r + n * span
            prev_base = jnp.maximum(base - span, r)
            rows = pl.ds(base, A_BLOCK, stride=dil)
            prev_rows = pl.ds(prev_base, A_BLOCK, stride=dil)
            q = qf_ref[rows, :]
            k2 = jnp.concatenate([kf_ref[prev_rows, :], kf_ref[rows, :]], axis=0).astype(BF16)
            v2 = jnp.concatenate([vf_ref[prev_rows, :], vf_ref[rows, :]], axis=0).astype(BF16)
            first_block = jnp.where(n == 0, 1, 0)
            outs, lses = [], []
            for hh in range(2):
                qm = jnp.where(low_lane if hh == 0 else ~low_lane, q, 0.0).astype(BF16)
                s = _dot_nt(k2, qm) + bias_ref[0, g, first_block, hh]
                m = jnp.max(s, axis=0, keepdims=True)
                p = jnp.exp2(s - m)
                l = jnp.sum(p, axis=0, keepdims=True)
                outs.append(_dot_tn(v2, p.astype(BF16)) * (1.0 / l))
                lses.append(jnp.broadcast_to(m + jnp.log2(l), tile))
            o_blk = jnp.where(low_row, outs[0], outs[1]).T
            lse_blk = jnp.where(low_row, lses[0], lses[1]).T
            return rows, o_blk, lse_blk

        def merge(rows, o_blk, lse_blk, g=g):
            if g == 0:
                m_ref[rows, :] = lse_blk
                a_ref[rows, :] = o_blk
                return
            m_old = m_ref[rows, :]
            m_new = jnp.maximum(m_old, lse_blk)
            c_old = jnp.exp2(m_old - m_new)
            c_blk = jnp.exp2(lse_blk - m_new)
            w_new = c_blk + (c_old if g == 1 else c_old * w_ref[rows, :])
            a_new = c_old * a_ref[rows, :] + c_blk * o_blk
            if g < last:
                m_ref[rows, :] = m_new
                w_ref[rows, :] = w_new
                a_ref[rows, :] = a_new
            else:
                a_ref[rows, :] = a_new / w_new

        def step(it, carry, attend=attend, merge=merge):
            done = [attend(it * A_BLOCKS_PER_STEP + u) for u in range(A_BLOCKS_PER_STEP)]
            for args in done:
                merge(*args)
            return carry

        lax.fori_loop(0, dil * nb // A_BLOCKS_PER_STEP, step, 0)

    def narrow(c, carry):
        rows = pl.ds(pl.multiple_of(c * chunk, chunk), chunk)
        y_ref[rows, :] = a_ref[rows, :].astype(BF16)
        return carry

    lax.fori_loop(0, seq // chunk, narrow, 0)


def _dilated_attention(qkv, bias_tiles, batch, seq):
    pairs = A_HEADS // 2
    col = lambda off: (lambda b, p: (b, off * pairs + p))
    state = pltpu.VMEM((seq, 128), F32)
    return pl.pallas_call(
        _dilated_kernel,
        grid=(batch, pairs),
        in_specs=[
            pl.BlockSpec((seq, 128), col(0)),
            pl.BlockSpec((seq, 128), col(1)),
            pl.BlockSpec((seq, 128), col(2)),
            pl.BlockSpec((1, len(A_PATTERNS), 2, 2, 2 * A_BLOCK, A_BLOCK), lambda b, p: (p, 0, 0, 0, 0, 0)),
        ],
        out_specs=pl.BlockSpec((seq, 128), lambda b, p: (b, p)),
        out_shape=jax.ShapeDtypeStruct((batch * seq, A_WIDTH), BF16),
        scratch_shapes=[state] * 6,
        compiler_params=_params(2),
        name="dilated_attention",
    )(qkv, qkv, qkv, bias_tiles)


def _flash_init(m_ref, acc_ref):
    m_ref[...] = jnp.full(m_ref.shape, NEG, F32)
    acc_ref[...] = jnp.zeros(acc_ref.shape, F32)


def _with_ones(v_t):
    return jnp.concatenate([v_t, jnp.ones((ONES_ROWS, v_t.shape[1]), v_t.dtype)], axis=0)


def _key_rows(part):
    return slice(part * KEY_PART, (part + 1) * KEY_PART)


def _store_scores(s, slot, idx, s_ref, cmax_ref):
    s_ref[slot, idx] = s
    cmax_ref[slot, idx] = jnp.max(s, axis=0, keepdims=True)


def _masked_part(slot, idx, part, masked, s_ref):
    s = s_ref[slot, idx, _key_rows(part), :]
    if masked:
        key = lax.broadcasted_iota(jnp.int32, s.shape, 0) + part * KEY_PART
        s = jnp.where(key <= lax.broadcasted_iota(jnp.int32, s.shape, 1), s, NEG)
    return s


def _flash_begin(slot, idx, masked, m_ref, s_ref, cmax_ref):
    if masked:
        col_max = functools.reduce(jnp.maximum, [
            jnp.max(_masked_part(slot, idx, part, True, s_ref), axis=0, keepdims=True)
            for part in range(ATT_TILE // KEY_PART)])
    else:
        col_max = cmax_ref[slot, idx]
    m_prev = m_ref[idx]
    m_new = jnp.maximum(m_prev, col_max)
    m_ref[idx] = m_new
    return m_new, jnp.exp2(m_prev - m_new)


def _flash_part(slot, idx, part, masked, m_new, v_ext, s_ref):
    p = jnp.exp2(_masked_part(slot, idx, part, masked, s_ref) - m_new)
    return _dot(v_ext[:, _key_rows(part)], p.astype(BF16))


def _flash_result(idx, dv, acc_ref):
    acc = acc_ref[idx]
    return acc[:dv] * (1.0 / acc[dv:dv + 1])


def _flash_sweep(n_tiles, scores, absorb, finish, m_ref, acc_ref, s_ref, cmax_ref):
    n_items = n_tiles * (n_tiles + 1) // 2
    assert n_items % 2 == 0

    def following(i, c):
        end = c == i
        return jnp.where(end, i + 1, i), jnp.where(end, 0, c + 1)

    parts = ATT_TILE // KEY_PART

    def item(nxt, cur, diag, slot):
        for idx in range(2):
            scores(*nxt, 1 - slot, idx)
            m_new, alpha = _flash_begin(slot, idx, diag, m_ref, s_ref, cmax_ref)
            pv = None
            for part in range(parts):
                piece = absorb(cur, slot, idx, part, diag, m_new)
                pv = piece if pv is None else pv + piece
            acc_ref[idx] = alpha * acc_ref[idx] + pv

    for idx in range(2):
        scores(0, 0, 0, idx)

    def two_items(u, carry):
        i0, c0 = carry
        i1, c1 = following(i0, c0)
        i2, c2 = following(i1, c1)
        i2s = jnp.minimum(i2, n_tiles - 1)

        def variant(diag0, diag1):
            def run():
                item((i1, c1), c0, diag0, 0)
                if diag0:
                    finish(i0)
                item((i2s, c2), c1, diag1, 1)
                if diag1:
                    finish(i1)
            return run

        kind = 2 * (c0 == i0).astype(jnp.int32) + (c1 == i1).astype(jnp.int32)
        lax.switch(kind, [variant(False, False), variant(False, True),
                          variant(True, False), variant(True, True)])
        return i2, c2

    lax.fori_loop(0, n_items // 2, two_items, (jnp.int32(0), jnp.int32(0)))


def _row_in_ranges(shape, ranges):
    row = lax.broadcasted_iota(jnp.int32, shape, 0)
    hit = None
    for lo, hi in ranges:
        r = (row >= lo) & (row < hi)
        hit = r if hit is None else hit | r
    return hit


def _diff_bias(bias_ref, d):
    sub = ATT_TILE // BIAS_TILE
    rows = [jnp.concatenate([bias_ref[0, jnp.clip(sub * d + qa - ka, 0, B_BIAS_TILES - 1)]
                             for qa in range(sub)], axis=1) for ka in range(sub)]
    return jnp.concatenate(rows, axis=0)


def _tile_cols(i):
    return pl.ds(pl.multiple_of(i * ATT_TILE, ATT_TILE), ATT_TILE)


def _diff_kernel(lam_init, qt_ref, k_ref, vt_ref, bias_ref, lam_ref, g_ref, y_ref,
                 qm_ref, m_ref, acc_ref, s_ref, cmax_ref):
    t = ATT_TILE
    n_tiles = qt_ref.shape[1] // t

    def split_queries(i, carry):
        qf = qt_ref[:, _tile_cols(i)].astype(F32)
        first = lax.broadcasted_iota(jnp.int32, qf.shape, 0) < B_QK_DIM
        qm_ref[0, :, _tile_cols(i)] = jnp.where(first, qf, 0.0).astype(BF16)
        qm_ref[1, :, _tile_cols(i)] = jnp.where(first, 0.0, qf).astype(BF16)
        return carry

    lax.fori_loop(0, n_tiles, split_queries, 0)
    lv = lam_ref[...]
    lam = (jnp.exp(jnp.sum(lv[0:1] * lv[1:2], axis=-1, keepdims=True))
           - jnp.exp(jnp.sum(lv[2:3] * lv[3:4], axis=-1, keepdims=True)) + lam_init)
    _flash_init(m_ref, acc_ref)

    def scores(i, c, slot, idx):
        k = k_ref[_tile_cols(c), :]
        s = _dot(k, qm_ref[idx, :, _tile_cols(i)]) + _diff_bias(bias_ref, i - c)
        _store_scores(s, slot, idx, s_ref, cmax_ref)

    def absorb(c, slot, idx, part, masked, m_new):
        v_ext = _with_ones(vt_ref[:, _tile_cols(c)])
        return _flash_part(slot, idx, part, masked, m_new, v_ext, s_ref)

    def finish(i):
        o = _flash_result(0, B_V_DIM, acc_ref) - lam * _flash_result(1, B_V_DIM, acc_ref)
        y = o * lax.rsqrt(jnp.mean(o * o, axis=0, keepdims=True) + EPS) * g_ref[...] * (1.0 - lam_init)
        y_ref[_tile_cols(i), :] = y.T.astype(BF16)
        _flash_init(m_ref, acc_ref)

    _flash_sweep(n_tiles, scores, absorb, finish, m_ref, acc_ref, s_ref, cmax_ref)


def _diff_attention(qb_t, kb, vb_t, bias_tiles, lam_vecs, g_col, batch, seq, lam_init):
    t = ATT_TILE
    return pl.pallas_call(
        functools.partial(_diff_kernel, lam_init),
        grid=(batch, B_HEADS),
        in_specs=[
            pl.BlockSpec((128, seq), lambda b, h: (h, b)),
            pl.BlockSpec((seq, 128), lambda b, h: (b, h)),
            pl.BlockSpec((128, seq), lambda b, h: (h, b)),
            pl.BlockSpec((1, B_BIAS_TILES, BIAS_TILE, BIAS_TILE), lambda b, h: (h, 0, 0, 0)),
            _const_spec((4, B_QK_DIM)),
            _const_spec((B_V_DIM, 1)),
        ],
        out_specs=pl.BlockSpec((seq, 128), lambda b, h: (b, h)),
        out_shape=jax.ShapeDtypeStruct((batch * seq, B_WIDTH), BF16),
        scratch_shapes=[
            pltpu.VMEM((2, 128, seq), BF16),
            pltpu.VMEM((2, 1, t), F32),
            pltpu.VMEM((2, B_V_DIM + ONES_ROWS, t), F32),
            pltpu.VMEM((2, 2, t, t), F32),
            pltpu.VMEM((2, 2, 1, t), F32),
        ],
        compiler_params=_params(2),
        name="diff_attention",
    )(qb_t, kb, vb_t, bias_tiles, lam_vecs, g_col)


def _mla_kernel(qt_ref, k_ref, vt_ref, y_ref, qm_ref, m_ref, acc_ref, s_ref, cmax_ref):
    t = ATT_TILE
    n_tiles = qt_ref.shape[1] // t

    def split_heads(i, carry):
        qf = qt_ref[:, _tile_cols(i)].astype(F32)
        qm_ref[0, :, _tile_cols(i)] = jnp.where(_row_in_ranges(qf.shape, C_HEAD_A_RANGES), qf, 0.0).astype(BF16)
        qm_ref[1, :, _tile_cols(i)] = jnp.where(_row_in_ranges(qf.shape, C_HEAD_B_RANGES), qf, 0.0).astype(BF16)
        return carry

    lax.fori_loop(0, n_tiles, split_heads, 0)
    _flash_init(m_ref, acc_ref)

    def scores(i, c, slot, idx):
        k = k_ref[_tile_cols(c), :]
        _store_scores(_dot(k, qm_ref[idx, :, _tile_cols(i)]), slot, idx, s_ref, cmax_ref)

    def absorb(c, slot, idx, part, masked, m_new):
        v_t = vt_ref[idx * C_V:(idx + 1) * C_V, _tile_cols(c)]
        return _flash_part(slot, idx, part, masked, m_new, _with_ones(v_t), s_ref)

    def finish(i):
        o = jnp.concatenate([_flash_result(0, C_V, acc_ref), _flash_result(1, C_V, acc_ref)], axis=0)
        y_ref[_tile_cols(i), :] = o.T.astype(BF16)
        _flash_init(m_ref, acc_ref)

    _flash_sweep(n_tiles, scores, absorb, finish, m_ref, acc_ref, s_ref, cmax_ref)


def _mla_attention(qc_t, kc, vc_t, batch, seq):
    t = ATT_TILE
    return pl.pallas_call(
        _mla_kernel,
        grid=(batch, C_PAIRS),
        in_specs=[
            pl.BlockSpec((C_PAIR_LANES, seq), lambda b, j: (j, b)),
            pl.BlockSpec((seq, C_PAIR_LANES), lambda b, j: (b, j)),
            pl.BlockSpec((2 * C_V, seq), lambda b, j: (j, b)),
        ],
        out_specs=pl.BlockSpec((seq, 2 * C_V), lambda b, j: (b, j)),
        out_shape=jax.ShapeDtypeStruct((batch * seq, C_WIDTH), BF16),
        scratch_shapes=[
            pltpu.VMEM((2, C_PAIR_LANES, seq), BF16),
            pltpu.VMEM((2, 1, t), F32),
            pltpu.VMEM((2, C_V + ONES_ROWS, t), F32),
            pltpu.VMEM((2, 2, t, t), F32),
            pltpu.VMEM((2, 2, 1, t), F32),
        ],
        compiler_params=_params(2),
        name="mla_attention",
    )(qc_t, kc, vc_t)


def _sigmoid(z):
    return 1.0 / (1.0 + jnp.exp(-z))


def _merge_kernel(x_ref, g_ref, ya_ref, yb_ref, yc_ref, wg_ref, bg_ref, wa_ref, wb_ref, wc_ref,
                  wo_ref, out_ref):
    x = x_ref[...]
    h = _rms(x, g_ref[...]).astype(BF16)
    merged = jnp.zeros(x.shape, F32)
    for k, (y_ref, w_ref) in enumerate(((ya_ref, wa_ref), (yb_ref, wb_ref), (yc_ref, wc_ref))):
        cols = slice(k * D_MODEL, (k + 1) * D_MODEL)
        gate = _sigmoid(_dot(h, wg_ref[:, cols]) + bg_ref[:, cols])
        merged = merged + gate * _dot(y_ref[...], w_ref[...])
    out_ref[...] = x + _dot(merged.astype(BF16), wo_ref[...])


def _merge_stage(x2, g, ya, yb, yc, wg, bg, wa, wb, wc, wo):
    n = x2.shape[0]
    tm = ROW_TILE
    row = lambda i: (i, 0)
    return pl.pallas_call(
        _merge_kernel,
        grid=(n // tm,),
        in_specs=[
            pl.BlockSpec((tm, D_MODEL), row),
            _const_spec((1, D_MODEL)),
            pl.BlockSpec((tm, A_WIDTH), row),
            pl.BlockSpec((tm, B_WIDTH), row),
            pl.BlockSpec((tm, C_WIDTH), row),
            _const_spec((D_MODEL, N_BRANCH * D_MODEL)),
            _const_spec((1, N_BRANCH * D_MODEL)),
            _const_spec((A_WIDTH, D_MODEL)),
            _const_spec((B_WIDTH, D_MODEL)),
            _const_spec((C_WIDTH, D_MODEL)),
            _const_spec((D_MODEL, D_MODEL)),
        ],
        out_specs=pl.BlockSpec((tm, D_MODEL), row),
        out_shape=jax.ShapeDtypeStruct((n, D_MODEL), F32),
        compiler_params=_params(1),
        name="gated_merge",
    )(x2, g, ya, yb, yc, wg, bg, wa, wb, wc, wo)


def _ffn_kernel(final_norm, x_ref, g_ref, wg_ref, wu_ref, wd_ref, gf_ref, out_ref):
    x = x_ref[...]
    h = _rms(x, g_ref[...]).astype(BF16)
    acc = jnp.zeros(x.shape, F32)
    for c in range(0, FFN_HIDDEN, FFN_CHUNK):
        cols = slice(c, c + FFN_CHUNK)
        gate = _dot(h, wg_ref[:, cols])
        act = gate * _sigmoid(gate) * _dot(h, wu_ref[:, cols])
        acc = acc + _dot(act.astype(BF16), wd_ref[cols, :])
    y = x + acc
    if final_norm:
        y = _rms(y, gf_ref[...])
    out_ref[...] = y


def _ffn_stage(x2, g, wg, wu, wd, gf, final_norm):
    n = x2.shape[0]
    tm = ROW_TILE
    row = lambda i: (i, 0)
    return pl.pallas_call(
        functools.partial(_ffn_kernel, final_norm),
        grid=(n // tm,),
        in_specs=[
            pl.BlockSpec((tm, D_MODEL), row),
            _const_spec((1, D_MODEL)),
            _const_spec((D_MODEL, FFN_HIDDEN)),
            _const_spec((D_MODEL, FFN_HIDDEN)),
            _const_spec((FFN_HIDDEN, D_MODEL)),
            _const_spec((1, D_MODEL)),
        ],
        out_specs=pl.BlockSpec((tm, D_MODEL), row),
        out_shape=jax.ShapeDtypeStruct((n, D_MODEL), F32),
        compiler_params=_params(1),
        name="swiglu_final" if final_norm else "swiglu",
    )(x2, g, wg, wu, wd, gf)


def _gather_cols(w, idx):
    padded = jnp.concatenate([w, jnp.zeros((w.shape[0], 1), w.dtype)], axis=1)
    return padded[:, idx]


def kernel(x, rel_bias_table, ln_mix_g, w_in, lambda_q1, lambda_k1, lambda_q2, lambda_k2, diff_subln_g, mla_q_norm_g, w_uq, mla_kv_norm_g, w_ukv, w_gate, b_gate, w_br_a, w_br_b, w_br_c, w_o, ln_ffn_g, w_ffn_gate, w_ffn_up, w_ffn_down, final_norm_g):
    batch, seq, _ = x.shape
    depth = w_in.shape[0]
    assert all(seq % (A_BLOCK * dil) == 0 for _, dil in A_PATTERNS)
    assert (seq // A_BLOCK) % A_BLOCKS_PER_STEP == 0
    assert seq % ATT_TILE == 0 and seq % ROW_TILE == 0
    lay = _C_LAYOUT
    cos_t, sin_t = _rope_tables(seq)

    a_tiles = _bias_expand(rel_bias_table, _a_bucket_index(), A_HEADS, 0, 2 * A_BLOCK, "bias_expand_a",
                           gain=LOG2E)
    a_tiles = a_tiles.reshape(A_HEADS // 2, 2, len(A_PATTERNS), 2, 2 * A_BLOCK, A_BLOCK)
    a_tiles = jnp.transpose(a_tiles, (0, 2, 3, 1, 4, 5))
    b_tiles = _bias_expand(rel_bias_table, _b_bucket_index(), B_HEADS, A_HEADS, BIAS_TILE,
                           "bias_expand_b", gain=LOG2E).reshape(B_HEADS, B_BIAS_TILES, BIAS_TILE, BIAS_TILE)

    cq0 = AB_COLS
    ckv0 = cq0 + C_Q_RANK
    kr0 = ckv0 + C_KV_RANK
    x2 = x.reshape(batch * seq, D_MODEL)
    for l in range(depth):
        w_in_l = w_in[l]
        qkv_a, kb, qb_t, vb_t, qc_t, kc, vc_t = _input_stage(
            x2, ln_mix_g[l][None, :],
            w_in_l[:, :AB_COLS].astype(BF16),
            w_in_l[:, cq0:ckv0].astype(BF16), mla_q_norm_g[l][None, :],
            _gather_cols(w_uq[l], lay["q"]).astype(BF16),
            w_in_l[:, ckv0:kr0].astype(BF16), mla_kv_norm_g[l][None, :],
            _gather_cols(w_ukv[l], lay["k"]).astype(BF16),
            w_ukv[l][:, lay["v"]].astype(BF16),
            _gather_cols(w_in_l[:, kr0:], lay["kr"]).astype(BF16),
            cos_t, sin_t, seq)

        ya = _dilated_attention(qkv_a, a_tiles, batch, seq)

        lam_init = 0.8 - 0.6 * math.exp(-0.3 * l)
        lam_vecs = jnp.stack([lambda_q1[l], lambda_k1[l], lambda_q2[l], lambda_k2[l]]).astype(F32)
        yb = _diff_attention(qb_t, kb, vb_t, b_tiles, lam_vecs, diff_subln_g[l][:, None],
                             batch, seq, lam_init)
        yc = _mla_attention(qc_t, kc, vc_t, batch, seq)

        x2 = _merge_stage(x2, ln_mix_g[l][None, :], ya, yb, yc,
                          w_gate[l].astype(BF16), b_gate[l][None, :],
                          w_br_a[l].astype(BF16), w_br_b[l].astype(BF16), w_br_c[l].astype(BF16),
                          w_o[l].astype(BF16))
        x2 = _ffn_stage(x2, ln_ffn_g[l][None, :], w_ffn_gate[l].astype(BF16),
                        w_ffn_up[l].astype(BF16), w_ffn_down[l].astype(BF16),
                        final_norm_g[None, :], final_norm=(l == depth - 1))
    return x2.reshape(batch, seq, D_MODEL)
```

```python
import functools
import math

import numpy as np
import jax
import jax.numpy as jnp
from jax import lax
from jax.experimental import pallas as pl
from jax.experimental.pallas import tpu as pltpu

D_MODEL = 1024
HEAD_DIM = 64
A_HEADS = 8
A_PATTERNS = ((2048, 16), (512, 4), (128, 1))
A_WIDTH = A_HEADS * HEAD_DIM
A_BLOCK = 128
A_BLOCKS_PER_STEP = 16
B_HEADS = 4
B_QK_DIM = HEAD_DIM
B_V_DIM = 2 * HEAD_DIM
B_WIDTH = B_HEADS * B_V_DIM
C_HEADS = 8
C_PAIRS = C_HEADS // 2
C_Q_RANK = 768
C_KV_RANK = 256
C_NOPE = 64
C_ROPE = 32
C_V = 64
C_WIDTH = C_HEADS * C_V
C_PAIR_LANES = 256
C_COLS = C_PAIRS * C_PAIR_LANES
ROPE_THETA = 10000.0
REL_BUCKETS = 32
REL_MAX_DIST = 2048
AB_COLS = 3 * A_WIDTH + 4 * B_HEADS * B_QK_DIM + B_WIDTH
QB_COL0 = 3 * A_WIDTH
KB_COL0 = QB_COL0 + 2 * B_HEADS * B_QK_DIM
VB_COL0 = KB_COL0 + 2 * B_HEADS * B_QK_DIM
N_BRANCH = 3
FFN_HIDDEN = 2816
EPS = 1e-6
NEG = -1e30
LOG2E = math.log2(math.e)
A_SCORE_SCALE = HEAD_DIM ** -0.5 * LOG2E
B_SCORE_SCALE = B_QK_DIM ** -0.5 * LOG2E
C_SCORE_SCALE = (C_NOPE + C_ROPE) ** -0.5 * LOG2E
ONES_ROWS = 16

ROW_TILE = 512
ATT_TILE = 512
KEY_PART = 256
BIAS_TILE = 256
B_BIAS_TILES = 8
FFN_CHUNK = 256
VMEM_LIMIT = 56 * 1024 * 1024

F32 = jnp.float32
BF16 = jnp.bfloat16


def _dot(a, b):
    return jnp.dot(a, b, preferred_element_type=F32)


def _dot_nt(a, b):
    return lax.dot_general(a, b, (((1,), (1,)), ((), ())), preferred_element_type=F32)


def _rms(x, g):
    return x * lax.rsqrt(jnp.mean(x * x, axis=-1, keepdims=True) + EPS) * g


def _swap_halves(t):
    half = t.shape[1] // 2
    return jnp.concatenate([t[:, half:], t[:, :half]], axis=1)


def _const_spec(shape):
    nd = len(shape)
    return pl.BlockSpec(shape, lambda *_: (0,) * nd, pipeline_mode=pl.Buffered(1))


def _params(n_axes):
    return pltpu.CompilerParams(dimension_semantics=("parallel",) * n_axes,
                                vmem_limit_bytes=VMEM_LIMIT)


def _t5_bucket_np(dist):
    dist = np.maximum(dist, 0)
    exact = REL_BUCKETS // 2
    ratio = np.maximum(dist, 1).astype(np.float32) / np.float32(exact)
    log_ratio = np.log(ratio).astype(np.float32) / np.float32(math.log(REL_MAX_DIST / exact))
    large = np.minimum(exact + (log_ratio * np.float32(REL_BUCKETS - exact)).astype(np.int32),
                       REL_BUCKETS - 1)
    return np.where(dist < exact, dist, large).astype(np.int32)


MASKED_BUCKET = REL_BUCKETS


def _a_bucket_index():
    kj = np.arange(2 * A_BLOCK)[:, None]
    qi = np.arange(A_BLOCK)[None, :]
    step = qi + A_BLOCK - kj
    in_band = (step >= 0) & (step <= A_BLOCK)
    tiles = []
    for _, dil in A_PATTERNS:
        bucket = _t5_bucket_np(step * dil)
        for first_block in (False, True):
            valid = in_band & (kj >= A_BLOCK) if first_block else in_band
            tiles.append(np.where(valid, bucket, MASKED_BUCKET))
    return np.concatenate(tiles, axis=0).astype(np.int32)


def _b_bucket_index():
    kj = np.arange(BIAS_TILE)[:, None]
    qi = np.arange(BIAS_TILE)[None, :]
    e = np.arange(B_BIAS_TILES)[:, None, None]
    idx = _t5_bucket_np(e * BIAS_TILE + qi - kj)
    assert (idx[-1] == REL_BUCKETS - 1).all()
    return idx.reshape(B_BIAS_TILES * BIAS_TILE, BIAS_TILE)


def _c_layout():
    hq = C_NOPE + C_ROPE
    q_zero = C_HEADS * hq
    kv_zero = C_HEADS * (C_NOPE + C_V)
    half = C_ROPE // 2
    q_idx = np.full((C_PAIRS, C_PAIR_LANES), q_zero, np.int32)
    k_idx = np.full((C_PAIRS, C_PAIR_LANES), kv_zero, np.int32)
    for j in range(C_PAIRS):
        a, b = 2 * j, 2 * j + 1
        q_idx[j, 0:64] = a * hq + np.arange(64)
        q_idx[j, 64:80] = a * hq + C_NOPE + np.arange(half)
        q_idx[j, 80:96] = b * hq + C_NOPE + np.arange(half)
        q_idx[j, 128:192] = b * hq + np.arange(64)
        q_idx[j, 192:208] = a * hq + C_NOPE + half + np.arange(half)
        q_idx[j, 208:224] = b * hq + C_NOPE + half + np.arange(half)
        k_idx[j, 0:64] = a * (C_NOPE + C_V) + np.arange(64)
        k_idx[j, 128:192] = b * (C_NOPE + C_V) + np.arange(64)
    v_idx = (np.arange(C_HEADS)[:, None] * (C_NOPE + C_V) + C_NOPE + np.arange(C_V)[None, :])
    kr_idx = np.full((C_PAIR_LANES,), C_ROPE, np.int32)
    kr_idx[64:80] = np.arange(half)
    kr_idx[80:96] = np.arange(half)
    kr_idx[192:208] = half + np.arange(half)
    kr_idx[208:224] = half + np.arange(half)
    rope_lane = np.full((C_PAIR_LANES,), -1, np.int32)
    for start in (64, 80, 192, 208):
        rope_lane[start:start + half] = np.arange(half)
    sin_sign = np.zeros((C_PAIR_LANES,), np.float32)
    sin_sign[64:96] = -1.0
    sin_sign[192:224] = 1.0
    return dict(q=q_idx.reshape(-1), k=k_idx.reshape(-1), v=v_idx.reshape(-1), kr=kr_idx,
                rope_lane=rope_lane, sin_sign=sin_sign)


_C_LAYOUT = _c_layout()
C_HEAD_A_RANGES = ((0, 80), (192, 208))
C_HEAD_B_RANGES = ((80, 96), (128, 192), (208, 224))


def _rope_tables(seq):
    pos = np.arange(seq, dtype=np.float64)
    inv_freq = ROPE_THETA ** (-np.arange(0, C_ROPE, 2, dtype=np.float64) / C_ROPE)
    ang = pos[:, None] * inv_freq[None, :]
    cos, sin = np.cos(ang), np.sin(ang)
    lane = _C_LAYOUT["rope_lane"]
    gather = np.maximum(lane, 0)
    cos_t = np.where((lane >= 0)[None, :], cos[:, gather], 1.0)
    sin_t = sin[:, gather] * _C_LAYOUT["sin_sign"][None, :]
    return jnp.asarray(cos_t, F32), jnp.asarray(sin_t, F32)


def _bias_expand_kernel(n_heads, head0, gain, table_ref, idx_ref, out_ref):
    idx = idx_ref[...]
    hit = [idx == b for b in range(1, REL_BUCKETS)]
    masked = idx == MASKED_BUCKET
    for h in range(n_heads):
        acc = jnp.full(idx.shape, table_ref[0, head0 + h] * gain, F32)
        for b in range(1, REL_BUCKETS):
            acc = jnp.where(hit[b - 1], table_ref[b, head0 + h] * gain, acc)
        out_ref[h] = jnp.where(masked, NEG, acc)


def _bias_expand(table, idx, n_heads, head0, block_rows, name, gain=1.0):
    rows, cols = idx.shape
    return pl.pallas_call(
        functools.partial(_bias_expand_kernel, n_heads, head0, gain),
        grid=(rows // block_rows,),
        in_specs=[
            pl.BlockSpec(memory_space=pltpu.SMEM),
            pl.BlockSpec((block_rows, cols), lambda i: (i, 0)),
        ],
        out_specs=pl.BlockSpec((n_heads, block_rows, cols), lambda i: (0, i, 0)),
        out_shape=jax.ShapeDtypeStruct((n_heads, rows, cols), F32),
        compiler_params=_params(1),
        name=name,
    )(table, jnp.asarray(idx))


def _input_kernel(x_ref, g_ref, wab_ref, wcq_ref, gq_ref, wuq_ref, wckv_ref, gkv_ref, wuk_ref,
                  wuv_ref, wkr_ref, cos_ref, sin_ref,
                  qkva_ref, kb_ref, qbt_ref, vbt_ref, qct_ref, kc_ref, vct_ref):
    h = _rms(x_ref[...], g_ref[...]).astype(BF16)
    cos = cos_ref[...]
    sin = sin_ref[...]

    def project(c):
        t = _dot(h, wab_ref[:, c:c + 512])
        if c < QB_COL0:
            qkva_ref[:, c:c + 512] = (t * A_SCORE_SCALE if c == 0 else t).astype(BF16)
        elif c == QB_COL0:
            qbt_ref[...] = (t * B_SCORE_SCALE).T.astype(BF16)
        elif c == KB_COL0:
            kb_ref[...] = t.astype(BF16)
        else:
            vbt_ref[...] = t.T.astype(BF16)

    chunks = iter(range(0, AB_COLS, 512))
    cq = _dot(h, wcq_ref[...])
    ckv = _dot(h, wckv_ref[...])
    kr = _dot(h, wkr_ref[...])
    project(next(chunks))
    cqn = _rms(cq, gq_ref[...]).astype(BF16)
    project(next(chunks))
    ckvn = _rms(ckv, gkv_ref[...]).astype(BF16)
    kr_rot = kr * cos + _swap_halves(kr) * sin
    for j in range(C_PAIRS):
        cols = slice(j * C_PAIR_LANES, (j + 1) * C_PAIR_LANES)
        t = _dot(cqn, wuq_ref[:, cols])
        qct_ref[cols, :] = ((t * cos + _swap_halves(t) * sin) * C_SCORE_SCALE).T.astype(BF16)
        project(next(chunks))
    for j in range(C_PAIRS):
        cols = slice(j * C_PAIR_LANES, (j + 1) * C_PAIR_LANES)
        kc_ref[:, cols] = (_dot(ckvn, wuk_ref[:, cols]) + kr_rot).astype(BF16)
    vct_ref[...] = _dot(ckvn, wuv_ref[...]).T.astype(BF16)
    assert next(chunks, None) is None


def _input_stage(x2, g, wab, wcq, gq, wuq, wckv, gkv, wuk, wuv, wkr, cos_t, sin_t, seq):
    n = x2.shape[0]
    tm = ROW_TILE
    pos_blocks = seq // tm
    row = lambda i: (i, 0)
    col = lambda i: (0, i)
    return pl.pallas_call(
        _input_kernel,
        grid=(n // tm,),
        in_specs=[
            pl.BlockSpec((tm, D_MODEL), row),
            _const_spec((1, D_MODEL)),
            _const_spec((D_MODEL, AB_COLS)),
            _const_spec((D_MODEL, C_Q_RANK)),
            _const_spec((1, C_Q_RANK)),
            _const_spec((C_Q_RANK, C_COLS)),
            _const_spec((D_MODEL, C_KV_RANK)),
            _const_spec((1, C_KV_RANK)),
            _const_spec((C_KV_RANK, C_COLS)),
            _const_spec((C_KV_RANK, C_WIDTH)),
            _const_spec((D_MODEL, C_PAIR_LANES)),
            pl.BlockSpec((tm, C_PAIR_LANES), lambda i: (i % pos_blocks, 0)),
            pl.BlockSpec((tm, C_PAIR_LANES), lambda i: (i % pos_blocks, 0)),
        ],
        out_specs=[
            pl.BlockSpec((tm, QB_COL0), row),
            pl.BlockSpec((tm, VB_COL0 - KB_COL0), row),
            pl.BlockSpec((B_WIDTH, tm), col),
            pl.BlockSpec((B_WIDTH, tm), col),
            pl.BlockSpec((C_COLS, tm), col),
            pl.BlockSpec((tm, C_COLS), row),
            pl.BlockSpec((C_WIDTH, tm), col),
        ],
        out_shape=[
            jax.ShapeDtypeStruct((n, QB_COL0), BF16),
            jax.ShapeDtypeStruct((n, VB_COL0 - KB_COL0), BF16),
            jax.ShapeDtypeStruct((B_WIDTH, n), BF16),
            jax.ShapeDtypeStruct((B_WIDTH, n), BF16),
            jax.ShapeDtypeStruct((C_COLS, n), BF16),
            jax.ShapeDtypeStruct((n, C_COLS), BF16),
            jax.ShapeDtypeStruct((C_WIDTH, n), BF16),
        ],
        compiler_params=_params(1),
        name="input_stage",
    )(x2, g, wab, wcq, gq, wuq, wckv, gkv, wuk, wuv, wkr, cos_t, sin_t)


def _dot_tn(a, b):
    return lax.dot_general(a, b, (((0,), (0,)), ((), ())), preferred_element_type=F32)


def _dilated_kernel(q_ref, k_ref, v_ref, bias_ref, y_ref, qf_ref, kf_ref, vf_ref, m_ref, w_ref, a_ref):
    seq = q_ref.shape[0]
    chunk = 512

    def widen(c, carry):
        rows = pl.ds(pl.multiple_of(c * chunk, chunk), chunk)
        qf_ref[rows, :] = q_ref[rows, :].astype(F32)
        kf_ref[rows, :] = k_ref[rows, :].astype(F32)
        vf_ref[rows, :] = v_ref[rows, :].astype(F32)
        return carry

    lax.fori_loop(0, seq // chunk, widen, 0)

    tile = (A_BLOCK, A_BLOCK)
    low_lane = lax.broadcasted_iota(jnp.int32, tile, 1) < HEAD_DIM
    low_row = lax.broadcasted_iota(jnp.int32, tile, 0) < HEAD_DIM
    last = len(A_PATTERNS) - 1
    for g, (_, dil) in enumerate(A_PATTERNS):
        nb = seq // dil // A_BLOCK
        span = dil * A_BLOCK

        def attend(blk, g=g, dil=dil, nb=nb, span=span):
            r = blk // nb
            n = blk % nb
            base = r + n * span
            prev_base = jnp.maximum(base - span, r)
            rows = pl.ds(base, A_BLOCK, stride=dil)
            prev_rows = pl.ds(prev_base, A_BLOCK, stride=dil)
            q = qf_ref[rows, :]
            k2 = jnp.concatenate([kf_ref[prev_rows, :], kf_ref[rows, :]], axis=0).astype(BF16)
            v2 = jnp.concatenate([vf_ref[prev_rows, :], vf_ref[rows, :]], axis=0).astype(BF16)
            first_block = jnp.where(n == 0, 1, 0)
            outs, lses = [], []
            for hh in range(2):
                qm = jnp.where(low_lane if hh == 0 else ~low_lane, q, 0.0).astype(BF16)
                s = _dot_nt(k2, qm) + bias_ref[hh, g, first_block]
                m = jnp.max(s, axis=0, keepdims=True)
                p = jnp.exp2(s - m)
                l = jnp.sum(p, axis=0, keepdims=True)
                outs.append(_dot_tn(v2, p.astype(BF16)) * (1.0 / l))
                lses.append(jnp.broadcast_to(m + jnp.log2(l), tile))
            o_blk = jnp.where(low_row, outs[0], outs[1]).T
            lse_blk = jnp.where(low_row, lses[0], lses[1]).T
            return rows, o_blk, lse_blk

        def merge(rows, o_blk, lse_blk, g=g):
            if g == 0:
                m_ref[rows, :] = lse_blk
                a_ref[rows, :] = o_blk
                return
            m_old = m_ref[rows, :]
            m_new = jnp.maximum(m_old, lse_blk)
            c_old = jnp.exp2(m_old - m_new)
            c_blk = jnp.exp2(lse_blk - m_new)
            w_new = c_blk + (c_old if g == 1 else c_old * w_ref[rows, :])
            a_new = c_old * a_ref[rows, :] + c_blk * o_blk
            if g < last:
                m_ref[rows, :] = m_new
                w_ref[rows, :] = w_new
                a_ref[rows, :] = a_new
            else:
                a_ref[rows, :] = a_new / w_new

        def step(it, carry, attend=attend, merge=merge):
            done = [attend(it * A_BLOCKS_PER_STEP + u) for u in range(A_BLOCKS_PER_STEP)]
            for args in done:
                merge(*args)
            return carry

        lax.fori_loop(0, dil * nb // A_BLOCKS_PER_STEP, step, 0)

    def narrow(c, carry):
        rows = pl.ds(pl.multiple_of(c * chunk, chunk), chunk)
        y_ref[rows, :] = a_ref[rows, :].astype(BF16)
        return carry

    lax.fori_loop(0, seq // chunk, narrow, 0)


def _dilated_attention(qkv, bias_tiles, batch, seq):
    pairs = A_HEADS // 2
    col = lambda off: (lambda b, p: (b, off * pairs + p))
    state = pltpu.VMEM((seq, 128), F32)
    return pl.pallas_call(
        _dilated_kernel,
        grid=(batch, pairs),
        in_specs=[
            pl.BlockSpec((seq, 128), col(0)),
            pl.BlockSpec((seq, 128), col(1)),
            pl.BlockSpec((seq, 128), col(2)),
            pl.BlockSpec((2, len(A_PATTERNS), 2, 2 * A_BLOCK, A_BLOCK), lambda b, p: (p, 0, 0, 0, 0)),
        ],
        out_specs=pl.BlockSpec((seq, 128), lambda b, p: (b, p)),
        out_shape=jax.ShapeDtypeStruct((batch * seq, A_WIDTH), BF16),
        scratch_shapes=[state] * 6,
        compiler_params=_params(2),
        name="dilated_attention",
    )(qkv, qkv, qkv, bias_tiles)


def _flash_init(m_ref, acc_ref):
    m_ref[...] = jnp.full(m_ref.shape, NEG, F32)
    acc_ref[...] = jnp.zeros(acc_ref.shape, F32)


def _with_ones(v_t):
    return jnp.concatenate([v_t, jnp.ones((ONES_ROWS, v_t.shape[1]), v_t.dtype)], axis=0)


def _key_rows(part):
    return slice(part * KEY_PART, (part + 1) * KEY_PART)


def _store_scores(s, slot, idx, s_ref, cmax_ref):
    s_ref[slot, idx] = s
    cmax_ref[slot, idx] = jnp.max(s, axis=0, keepdims=True)


def _masked_part(slot, idx, part, masked, s_ref):
    s = s_ref[slot, idx, _key_rows(part), :]
    if masked:
        key = lax.broadcasted_iota(jnp.int32, s.shape, 0) + part * KEY_PART
        s = jnp.where(key <= lax.broadcasted_iota(jnp.int32, s.shape, 1), s, NEG)
    return s


def _flash_begin(slot, idx, masked, m_ref, s_ref, cmax_ref):
    if masked:
        col_max = functools.reduce(jnp.maximum, [
            jnp.max(_masked_part(slot, idx, part, True, s_ref), axis=0, keepdims=True)
            for part in range(ATT_TILE // KEY_PART)])
    else:
        col_max = cmax_ref[slot, idx]
    m_prev = m_ref[idx]
    m_new = jnp.maximum(m_prev, col_max)
    m_ref[idx] = m_new
    return m_new, jnp.exp2(m_prev - m_new)


def _flash_part(slot, idx, part, masked, m_new, v_ext, s_ref):
    p = jnp.exp2(_masked_part(slot, idx, part, masked, s_ref) - m_new)
    return _dot(v_ext[:, _key_rows(part)], p.astype(BF16))


def _flash_result(idx, dv, acc_ref):
    acc = acc_ref[idx]
    return acc[:dv] * (1.0 / acc[dv:dv + 1])


def _flash_sweep(n_tiles, scores, absorb, finish, m_ref, acc_ref, s_ref, cmax_ref):
    n_items = n_tiles * (n_tiles + 1) // 2
    assert n_items % 2 == 0

    def following(i, c):
        end = c == i
        return jnp.where(end, i + 1, i), jnp.where(end, 0, c + 1)

    parts = ATT_TILE // KEY_PART

    def item(nxt, cur, diag, slot):
        for idx in range(2):
            scores(*nxt, 1 - slot, idx)
            m_new, alpha = _flash_begin(slot, idx, diag, m_ref, s_ref, cmax_ref)
            pv = None
            for part in range(parts):
                piece = absorb(cur, slot, idx, part, diag, m_new)
                pv = piece if pv is None else pv + piece
            acc_ref[idx] = alpha * acc_ref[idx] + pv

    for idx in range(2):
        scores(0, 0, 0, idx)

    def two_items(u, carry):
        i0, c0 = carry
        i1, c1 = following(i0, c0)
        i2, c2 = following(i1, c1)
        i2s = jnp.minimum(i2, n_tiles - 1)

        def variant(diag0, diag1):
            def run():
                item((i1, c1), c0, diag0, 0)
                if diag0:
                    finish(i0)
                item((i2s, c2), c1, diag1, 1)
                if diag1:
                    finish(i1)
            return run

        kind = 2 * (c0 == i0).astype(jnp.int32) + (c1 == i1).astype(jnp.int32)
        lax.switch(kind, [variant(False, False), variant(False, True),
                          variant(True, False), variant(True, True)])
        return i2, c2

    lax.fori_loop(0, n_items // 2, two_items, (jnp.int32(0), jnp.int32(0)))


def _row_in_ranges(shape, ranges):
    row = lax.broadcasted_iota(jnp.int32, shape, 0)
    hit = None
    for lo, hi in ranges:
        r = (row >= lo) & (row < hi)
        hit = r if hit is None else hit | r
    return hit


def _diff_bias(bias_ref, d):
    sub = ATT_TILE // BIAS_TILE
    rows = [jnp.concatenate([bias_ref[0, jnp.clip(sub * d + qa - ka, 0, B_BIAS_TILES - 1)]
                             for qa in range(sub)], axis=1) for ka in range(sub)]
    return jnp.concatenate(rows, axis=0)


def _tile_cols(i):
    return pl.ds(pl.multiple_of(i * ATT_TILE, ATT_TILE), ATT_TILE)


def _diff_kernel(lam_init, qt_ref, k_ref, vt_ref, bias_ref, lam_ref, g_ref, y_ref,
                 qm_ref, m_ref, acc_ref, s_ref, cmax_ref):
    t = ATT_TILE
    n_tiles = qt_ref.shape[1] // t

    def split_queries(i, carry):
        qf = qt_ref[:, _tile_cols(i)].astype(F32)
        first = lax.broadcasted_iota(jnp.int32, qf.shape, 0) < B_QK_DIM
        qm_ref[0, :, _tile_cols(i)] = jnp.where(first, qf, 0.0).astype(BF16)
        qm_ref[1, :, _tile_cols(i)] = jnp.where(first, 0.0, qf).astype(BF16)
        return carry

    lax.fori_loop(0, n_tiles, split_queries, 0)
    lv = lam_ref[...]
    lam = (jnp.exp(jnp.sum(lv[0:1] * lv[1:2], axis=-1, keepdims=True))
           - jnp.exp(jnp.sum(lv[2:3] * lv[3:4], axis=-1, keepdims=True)) + lam_init)
    _flash_init(m_ref, acc_ref)

    def scores(i, c, slot, idx):
        k = k_ref[_tile_cols(c), :]
        s = _dot(k, qm_ref[idx, :, _tile_cols(i)]) + _diff_bias(bias_ref, i - c)
        _store_scores(s, slot, idx, s_ref, cmax_ref)

    def absorb(c, slot, idx, part, masked, m_new):
        v_ext = _with_ones(vt_ref[:, _tile_cols(c)])
        return _flash_part(slot, idx, part, masked, m_new, v_ext, s_ref)

    def finish(i):
        o = _flash_result(0, B_V_DIM, acc_ref) - lam * _flash_result(1, B_V_DIM, acc_ref)
        y = o * lax.rsqrt(jnp.mean(o * o, axis=0, keepdims=True) + EPS) * g_ref[...] * (1.0 - lam_init)
        y_ref[_tile_cols(i), :] = y.T.astype(BF16)
        _flash_init(m_ref, acc_ref)

    _flash_sweep(n_tiles, scores, absorb, finish, m_ref, acc_ref, s_ref, cmax_ref)


def _diff_attention(qb_t, kb, vb_t, bias_tiles, lam_vecs, g_col, batch, seq, lam_init):
    t = ATT_TILE
    return pl.pallas_call(
        functools.partial(_diff_kernel, lam_init),
        grid=(batch, B_HEADS),
        in_specs=[
            pl.BlockSpec((128, seq), lambda b, h: (h, b)),
            pl.BlockSpec((seq, 128), lambda b, h: (b, h)),
            pl.BlockSpec((128, seq), lambda b, h: (h, b)),
            pl.BlockSpec((1, B_BIAS_TILES, BIAS_TILE, BIAS_TILE), lambda b, h: (h, 0, 0, 0)),
            _const_spec((4, B_QK_DIM)),
            _const_spec((B_V_DIM, 1)),
        ],
        out_specs=pl.BlockSpec((seq, 128), lambda b, h: (b, h)),
        out_shape=jax.ShapeDtypeStruct((batch * seq, B_WIDTH), BF16),
        scratch_shapes=[
            pltpu.VMEM((2, 128, seq), BF16),
            pltpu.VMEM((2, 1, t), F32),
            pltpu.VMEM((2, B_V_DIM + ONES_ROWS, t), F32),
            pltpu.VMEM((2, 2, t, t), F32),
            pltpu.VMEM((2, 2, 1, t), F32),
        ],
        compiler_params=_params(2),
        name="diff_attention",
    )(qb_t, kb, vb_t, bias_tiles, lam_vecs, g_col)


def _mla_kernel(qt_ref, k_ref, vt_ref, y_ref, qm_ref, m_ref, acc_ref, s_ref, cmax_ref):
    t = ATT_TILE
    n_tiles = qt_ref.shape[1] // t

    def split_heads(i, carry):
        qf = qt_ref[:, _tile_cols(i)].astype(F32)
        qm_ref[0, :, _tile_cols(i)] = jnp.where(_row_in_ranges(qf.shape, C_HEAD_A_RANGES), qf, 0.0).astype(BF16)
        qm_ref[1, :, _tile_cols(i)] = jnp.where(_row_in_ranges(qf.shape, C_HEAD_B_RANGES), qf, 0.0).astype(BF16)
        return carry

    lax.fori_loop(0, n_tiles, split_heads, 0)
    _flash_init(m_ref, acc_ref)

    def scores(i, c, slot, idx):
        k = k_ref[_tile_cols(c), :]
        _store_scores(_dot(k, qm_ref[idx, :, _tile_cols(i)]), slot, idx, s_ref, cmax_ref)

    def absorb(c, slot, idx, part, masked, m_new):
        v_t = vt_ref[idx * C_V:(idx + 1) * C_V, _tile_cols(c)]
        return _flash_part(slot, idx, part, masked, m_new, _with_ones(v_t), s_ref)

    def finish(i):
        o = jnp.concatenate([_flash_result(0, C_V, acc_ref), _flash_result(1, C_V, acc_ref)], axis=0)
        y_ref[_tile_cols(i), :] = o.T.astype(BF16)
        _flash_init(m_ref, acc_ref)

    _flash_sweep(n_tiles, scores, absorb, finish, m_ref, acc_ref, s_ref, cmax_ref)


def _mla_attention(qc_t, kc, vc_t, batch, seq):
    t = ATT_TILE
    return pl.pallas_call(
        _mla_kernel,
        grid=(batch, C_PAIRS),
        in_specs=[
            pl.BlockSpec((C_PAIR_LANES, seq), lambda b, j: (j, b)),
            pl.BlockSpec((seq, C_PAIR_LANES), lambda b, j: (b, j)),
            pl.BlockSpec((2 * C_V, seq), lambda b, j: (j, b)),
        ],
        out_specs=pl.BlockSpec((seq, 2 * C_V), lambda b, j: (b, j)),
        out_shape=jax.ShapeDtypeStruct((batch * seq, C_WIDTH), BF16),
        scratch_shapes=[
            pltpu.VMEM((2, C_PAIR_LANES, seq), BF16),
            pltpu.VMEM((2, 1, t), F32),
            pltpu.VMEM((2, C_V + ONES_ROWS, t), F32),
            pltpu.VMEM((2, 2, t, t), F32),
            pltpu.VMEM((2, 2, 1, t), F32),
        ],
        compiler_params=_params(2),
        name="mla_attention",
    )(qc_t, kc, vc_t)


def _sigmoid(z):
    return 1.0 / (1.0 + jnp.exp(-z))


def _merge_kernel(x_ref, g_ref, ya_ref, yb_ref, yc_ref, wg_ref, bg_ref, wa_ref, wb_ref, wc_ref,
                  wo_ref, out_ref):
    x = x_ref[...]
    h = _rms(x, g_ref[...]).astype(BF16)
    merged = jnp.zeros(x.shape, F32)
    for k, (y_ref, w_ref) in enumerate(((ya_ref, wa_ref), (yb_ref, wb_ref), (yc_ref, wc_ref))):
        cols = slice(k * D_MODEL, (k + 1) * D_MODEL)
        gate = _sigmoid(_dot(h, wg_ref[:, cols]) + bg_ref[:, cols])
        merged = merged + gate * _dot(y_ref[...], w_ref[...])
    out_ref[...] = x + _dot(merged.astype(BF16), wo_ref[...])


def _merge_stage(x2, g, ya, yb, yc, wg, bg, wa, wb, wc, wo):
    n = x2.shape[0]
    tm = ROW_TILE
    row = lambda i: (i, 0)
    return pl.pallas_call(
        _merge_kernel,
        grid=(n // tm,),
        in_specs=[
            pl.BlockSpec((tm, D_MODEL), row),
            _const_spec((1, D_MODEL)),
            pl.BlockSpec((tm, A_WIDTH), row),
            pl.BlockSpec((tm, B_WIDTH), row),
            pl.BlockSpec((tm, C_WIDTH), row),
            _const_spec((D_MODEL, N_BRANCH * D_MODEL)),
            _const_spec((1, N_BRANCH * D_MODEL)),
            _const_spec((A_WIDTH, D_MODEL)),
            _const_spec((B_WIDTH, D_MODEL)),
            _const_spec((C_WIDTH, D_MODEL)),
            _const_spec((D_MODEL, D_MODEL)),
        ],
        out_specs=pl.BlockSpec((tm, D_MODEL), row),
        out_shape=jax.ShapeDtypeStruct((n, D_MODEL), F32),
        compiler_params=_params(1),
        name="gated_merge",
    )(x2, g, ya, yb, yc, wg, bg, wa, wb, wc, wo)


def _ffn_kernel(final_norm, x_ref, g_ref, wg_ref, wu_ref, wd_ref, gf_ref, out_ref):
    x = x_ref[...]
    h = _rms(x, g_ref[...]).astype(BF16)
    acc = jnp.zeros(x.shape, F32)
    for c in range(0, FFN_HIDDEN, FFN_CHUNK):
        cols = slice(c, c + FFN_CHUNK)
        gate = _dot(h, wg_ref[:, cols])
        act = gate * _sigmoid(gate) * _dot(h, wu_ref[:, cols])
        acc = acc + _dot(act.astype(BF16), wd_ref[cols, :])
    y = x + acc
    if final_norm:
        y = _rms(y, gf_ref[...])
    out_ref[...] = y


def _ffn_stage(x2, g, wg, wu, wd, gf, final_norm):
    n = x2.shape[0]
    tm = ROW_TILE
    row = lambda i: (i, 0)
    return pl.pallas_call(
        functools.partial(_ffn_kernel, final_norm),
        grid=(n // tm,),
        in_specs=[
            pl.BlockSpec((tm, D_MODEL), row),
            _const_spec((1, D_MODEL)),
            _const_spec((D_MODEL, FFN_HIDDEN)),
            _const_spec((D_MODEL, FFN_HIDDEN)),
            _const_spec((FFN_HIDDEN, D_MODEL)),
            _const_spec((1, D_MODEL)),
        ],
        out_specs=pl.BlockSpec((tm, D_MODEL), row),
        out_shape=jax.ShapeDtypeStruct((n, D_MODEL), F32),
        compiler_params=_params(1),
        name="swiglu_final" if final_norm else "swiglu",
    )(x2, g, wg, wu, wd, gf)


def _gather_cols(w, idx):
    idx = np.asarray(idx)
    zero = idx == w.shape[1]
    cuts = [0] + [j for j in range(1, len(idx))
                  if zero[j] != zero[j - 1] or (not zero[j] and idx[j] != idx[j - 1] + 1)]
    pieces = []
    for lo, hi in zip(cuts, cuts[1:] + [len(idx)]):
        if idx[lo] == w.shape[1]:
            pieces.append(jnp.zeros((w.shape[0], hi - lo), w.dtype))
        else:
            pieces.append(w[:, int(idx[lo]):int(idx[lo]) + hi - lo])
    return jnp.concatenate(pieces, axis=1)


def kernel(x, rel_bias_table, ln_mix_g, w_in, lambda_q1, lambda_k1, lambda_q2, lambda_k2, diff_subln_g, mla_q_norm_g, w_uq, mla_kv_norm_g, w_ukv, w_gate, b_gate, w_br_a, w_br_b, w_br_c, w_o, ln_ffn_g, w_ffn_gate, w_ffn_up, w_ffn_down, final_norm_g):
    batch, seq, _ = x.shape
    depth = w_in.shape[0]
    assert all(seq % (A_BLOCK * dil) == 0 for _, dil in A_PATTERNS)
    assert (seq // A_BLOCK) % A_BLOCKS_PER_STEP == 0
    assert seq % ATT_TILE == 0 and seq % ROW_TILE == 0
    lay = _C_LAYOUT
    cos_t, sin_t = _rope_tables(seq)

    a_tiles = _bias_expand(rel_bias_table, _a_bucket_index(), A_HEADS, 0, 2 * A_BLOCK, "bias_expand_a",
                           gain=LOG2E)
    a_tiles = a_tiles.reshape(A_HEADS, len(A_PATTERNS), 2, 2 * A_BLOCK, A_BLOCK)
    b_tiles = _bias_expand(rel_bias_table, _b_bucket_index(), B_HEADS, A_HEADS, BIAS_TILE,
                           "bias_expand_b", gain=LOG2E).reshape(B_HEADS, B_BIAS_TILES, BIAS_TILE, BIAS_TILE)

    cq0 = AB_COLS
    ckv0 = cq0 + C_Q_RANK
    kr0 = ckv0 + C_KV_RANK
    x2 = x.reshape(batch * seq, D_MODEL)
    for l in range(depth):
        w_in_l = w_in[l]
        qkv_a, kb, qb_t, vb_t, qc_t, kc, vc_t = _input_stage(
            x2, ln_mix_g[l][None, :],
            w_in_l[:, :AB_COLS].astype(BF16),
            w_in_l[:, cq0:ckv0].astype(BF16), mla_q_norm_g[l][None, :],
            _gather_cols(w_uq[l], lay["q"]).astype(BF16),
            w_in_l[:, ckv0:kr0].astype(BF16), mla_kv_norm_g[l][None, :],
            _gather_cols(w_ukv[l], lay["k"]).astype(BF16),
            _gather_cols(w_ukv[l], lay["v"]).astype(BF16),
            _gather_cols(w_in_l[:, kr0:], lay["kr"]).astype(BF16),
            cos_t, sin_t, seq)

        ya = _dilated_attention(qkv_a, a_tiles, batch, seq)

        lam_init = 0.8 - 0.6 * math.exp(-0.3 * l)
        lam_vecs = jnp.stack([lambda_q1[l], lambda_k1[l], lambda_q2[l], lambda_k2[l]]).astype(F32)
        yb = _diff_attention(qb_t, kb, vb_t, b_tiles, lam_vecs, diff_subln_g[l][:, None],
                             batch, seq, lam_init)
        yc = _mla_attention(qc_t, kc, vc_t, batch, seq)

        x2 = _merge_stage(x2, ln_mix_g[l][None, :], ya, yb, yc,
                          w_gate[l].astype(BF16), b_gate[l][None, :],
                          w_br_a[l].astype(BF16), w_br_b[l].astype(BF16), w_br_c[l].astype(BF16),
                          w_o[l].astype(BF16))
        x2 = _ffn_stage(x2, ln_ffn_g[l][None, :], w_ffn_gate[l].astype(BF16),
                        w_ffn_up[l].astype(BF16), w_ffn_down[l].astype(BF16),
                        final_norm_g[None, :], final_norm=(l == depth - 1))
    return x2.reshape(batch, seq, D_MODEL)
```

```python
import functools
import math

import numpy as np
import jax
import jax.numpy as jnp
from jax import lax
from jax.experimental import pallas as pl
from jax.experimental.pallas import tpu as pltpu

D_MODEL = 1024
HEAD_DIM = 64
A_HEADS = 8
A_PATTERNS = ((2048, 16), (512, 4), (128, 1))
A_WIDTH = A_HEADS * HEAD_DIM
A_BLOCK = 128
A_BLOCKS_PER_STEP = 16
B_HEADS = 4
B_QK_DIM = HEAD_DIM
B_V_DIM = 2 * HEAD_DIM
B_WIDTH = B_HEADS * B_V_DIM
C_HEADS = 8
C_PAIRS = C_HEADS // 2
C_Q_RANK = 768
C_KV_RANK = 256
C_NOPE = 64
C_ROPE = 32
C_V = 64
C_WIDTH = C_HEADS * C_V
C_PAIR_LANES = 256
C_COLS = C_PAIRS * C_PAIR_LANES
ROPE_THETA = 10000.0
REL_BUCKETS = 32
REL_MAX_DIST = 2048
AB_COLS = 3 * A_WIDTH + 4 * B_HEADS * B_QK_DIM + B_WIDTH
QB_COL0 = 3 * A_WIDTH
KB_COL0 = QB_COL0 + 2 * B_HEADS * B_QK_DIM
VB_COL0 = KB_COL0 + 2 * B_HEADS * B_QK_DIM
CQ_COL0 = AB_COLS
CKV_COL0 = CQ_COL0 + C_Q_RANK
KR_COL0 = CKV_COL0 + C_KV_RANK
N_BRANCH = 3
FFN_HIDDEN = 2816
EPS = 1e-6
NEG = -1e30
LOG2E = math.log2(math.e)
A_SCORE_SCALE = HEAD_DIM ** -0.5 * LOG2E
B_SCORE_SCALE = B_QK_DIM ** -0.5 * LOG2E
C_SCORE_SCALE = (C_NOPE + C_ROPE) ** -0.5 * LOG2E
ONES_ROWS = 16

ROW_TILE = 512
ATT_TILE = 512
KEY_PART = 256
BIAS_TILE = 256
B_BIAS_TILES = 8
FFN_CHUNK = 256
VMEM_LIMIT = 56 * 1024 * 1024

F32 = jnp.float32
BF16 = jnp.bfloat16


def _dot(a, b):
    return jnp.dot(a, b, preferred_element_type=F32)


def _dot_nt(a, b):
    return lax.dot_general(a, b, (((1,), (1,)), ((), ())), preferred_element_type=F32)


def _rms(x, g):
    return x * lax.rsqrt(jnp.mean(x * x, axis=-1, keepdims=True) + EPS) * g


def _swap_halves(t):
    half = t.shape[1] // 2
    return jnp.concatenate([t[:, half:], t[:, :half]], axis=1)


def _const_spec(shape):
    nd = len(shape)
    return pl.BlockSpec(shape, lambda *_: (0,) * nd, pipeline_mode=pl.Buffered(1))


def _layer_spec(shape, layer):
    nd = len(shape)
    return pl.BlockSpec((None,) + tuple(shape), lambda *_: (layer,) + (0,) * nd, pipeline_mode=pl.Buffered(1))


def _cast_kernel(w_ref, o_ref):
    o_ref[...] = w_ref[...].astype(o_ref.dtype)


def _to_bf16(w):
    depth, rows, cols = w.shape
    block_rows = next(r for r in (512, 704, rows) if rows % r == 0)
    spec = pl.BlockSpec((1, block_rows, cols), lambda l, i: (l, i, 0))
    return pl.pallas_call(
        _cast_kernel,
        grid=(depth, rows // block_rows),
        in_specs=[spec],
        out_specs=spec,
        out_shape=jax.ShapeDtypeStruct(w.shape, BF16),
        compiler_params=_params(2),
        name="weights_to_bf16",
    )(w)


def _params(n_axes):
    return pltpu.CompilerParams(dimension_semantics=("parallel",) * n_axes,
                                vmem_limit_bytes=VMEM_LIMIT)


def _t5_bucket_np(dist):
    dist = np.maximum(dist, 0)
    exact = REL_BUCKETS // 2
    ratio = np.maximum(dist, 1).astype(np.float32) / np.float32(exact)
    log_ratio = np.log(ratio).astype(np.float32) / np.float32(math.log(REL_MAX_DIST / exact))
    large = np.minimum(exact + (log_ratio * np.float32(REL_BUCKETS - exact)).astype(np.int32),
                       REL_BUCKETS - 1)
    return np.where(dist < exact, dist, large).astype(np.int32)


MASKED_BUCKET = REL_BUCKETS


def _a_bucket_index():
    kj = np.arange(2 * A_BLOCK)[:, None]
    qi = np.arange(A_BLOCK)[None, :]
    step = qi + A_BLOCK - kj
    in_band = (step >= 0) & (step <= A_BLOCK)
    tiles = []
    for _, dil in A_PATTERNS:
        bucket = _t5_bucket_np(step * dil)
        for first_block in (False, True):
            valid = in_band & (kj >= A_BLOCK) if first_block else in_band
            tiles.append(np.where(valid, bucket, MASKED_BUCKET))
    return np.concatenate(tiles, axis=0).astype(np.int32)


def _b_bucket_index():
    kj = np.arange(BIAS_TILE)[:, None]
    qi = np.arange(BIAS_TILE)[None, :]
    e = np.arange(B_BIAS_TILES)[:, None, None]
    idx = _t5_bucket_np(e * BIAS_TILE + qi - kj)
    assert (idx[-1] == REL_BUCKETS - 1).all()
    return idx.reshape(B_BIAS_TILES * BIAS_TILE, BIAS_TILE)


def _c_layout():
    hq = C_NOPE + C_ROPE
    q_zero = C_HEADS * hq
    kv_zero = C_HEADS * (C_NOPE + C_V)
    half = C_ROPE // 2
    q_idx = np.full((C_PAIRS, C_PAIR_LANES), q_zero, np.int32)
    k_idx = np.full((C_PAIRS, C_PAIR_LANES), kv_zero, np.int32)
    for j in range(C_PAIRS):
        a, b = 2 * j, 2 * j + 1
        q_idx[j, 0:64] = a * hq + np.arange(64)
        q_idx[j, 64:80] = a * hq + C_NOPE + np.arange(half)
        q_idx[j, 80:96] = b * hq + C_NOPE + np.arange(half)
        q_idx[j, 128:192] = b * hq + np.arange(64)
        q_idx[j, 192:208] = a * hq + C_NOPE + half + np.arange(half)
        q_idx[j, 208:224] = b * hq + C_NOPE + half + np.arange(half)
        k_idx[j, 0:64] = a * (C_NOPE + C_V) + np.arange(64)
        k_idx[j, 128:192] = b * (C_NOPE + C_V) + np.arange(64)
    v_idx = (np.arange(C_HEADS)[:, None] * (C_NOPE + C_V) + C_NOPE + np.arange(C_V)[None, :])
    kr_idx = np.full((C_PAIR_LANES,), C_ROPE, np.int32)
    kr_idx[64:80] = np.arange(half)
    kr_idx[80:96] = np.arange(half)
    kr_idx[192:208] = half + np.arange(half)
    kr_idx[208:224] = half + np.arange(half)
    rope_lane = np.full((C_PAIR_LANES,), -1, np.int32)
    for start in (64, 80, 192, 208):
        rope_lane[start:start + half] = np.arange(half)
    sin_sign = np.zeros((C_PAIR_LANES,), np.float32)
    sin_sign[64:96] = -1.0
    sin_sign[192:224] = 1.0
    return dict(q=q_idx.reshape(-1), k=k_idx.reshape(-1), v=v_idx.reshape(-1), kr=kr_idx,
                rope_lane=rope_lane, sin_sign=sin_sign)


_C_LAYOUT = _c_layout()
C_HEAD_A_RANGES = ((0, 80), (192, 208))
C_HEAD_B_RANGES = ((80, 96), (128, 192), (208, 224))


def _rope_tables(seq):
    pos = np.arange(seq, dtype=np.float64)
    inv_freq = ROPE_THETA ** (-np.arange(0, C_ROPE, 2, dtype=np.float64) / C_ROPE)
    ang = pos[:, None] * inv_freq[None, :]
    cos, sin = np.cos(ang), np.sin(ang)
    lane = _C_LAYOUT["rope_lane"]
    gather = np.maximum(lane, 0)
    cos_t = np.where((lane >= 0)[None, :], cos[:, gather], 1.0)
    sin_t = sin[:, gather] * _C_LAYOUT["sin_sign"][None, :]
    return jnp.asarray(cos_t, F32), jnp.asarray(sin_t, F32)


def _bias_expand_kernel(n_heads, head0, gain, table_ref, idx_ref, out_ref):
    idx = idx_ref[...]
    hit = [idx == b for b in range(1, REL_BUCKETS)]
    masked = idx == MASKED_BUCKET
    for h in range(n_heads):
        acc = jnp.full(idx.shape, table_ref[0, head0 + h] * gain, F32)
        for b in range(1, REL_BUCKETS):
            acc = jnp.where(hit[b - 1], table_ref[b, head0 + h] * gain, acc)
        out_ref[h] = jnp.where(masked, NEG, acc)


def _bias_expand(table, idx, n_heads, head0, block_rows, name, gain=1.0):
    rows, cols = idx.shape
    return pl.pallas_call(
        functools.partial(_bias_expand_kernel, n_heads, head0, gain),
        grid=(rows // block_rows,),
        in_specs=[
            pl.BlockSpec(memory_space=pltpu.SMEM),
            pl.BlockSpec((block_rows, cols), lambda i: (i, 0)),
        ],
        out_specs=pl.BlockSpec((n_heads, block_rows, cols), lambda i: (0, i, 0)),
        out_shape=jax.ShapeDtypeStruct((n_heads, rows, cols), F32),
        compiler_params=_params(1),
        name=name,
    )(table, jnp.asarray(idx))


def _input_kernel(x_ref, g_ref, win_ref, gq_ref, wuq_ref, gkv_ref, wuk_ref,
                  wuv_ref, wkr_ref, cos_ref, sin_ref,
                  qkva_ref, kb_ref, qbt_ref, vbt_ref, qct_ref, kc_ref, vct_ref):
    h = _rms(x_ref[...], g_ref[...]).astype(BF16)
    cos = cos_ref[...]
    sin = sin_ref[...]

    def project(c):
        t = _dot(h, win_ref[:, c:c + 512])
        if c < QB_COL0:
            qkva_ref[:, c:c + 512] = (t * A_SCORE_SCALE if c == 0 else t).astype(BF16)
        elif c == QB_COL0:
            qbt_ref[...] = (t * B_SCORE_SCALE).T.astype(BF16)
        elif c == KB_COL0:
            kb_ref[...] = t.astype(BF16)
        else:
            vbt_ref[...] = t.T.astype(BF16)

    chunks = iter(range(0, AB_COLS, 512))
    cq = _dot(h, win_ref[:, CQ_COL0:CKV_COL0])
    ckv = _dot(h, win_ref[:, CKV_COL0:KR_COL0])
    kr = _dot(h, wkr_ref[...])
    project(next(chunks))
    cqn = _rms(cq, gq_ref[...]).astype(BF16)
    project(next(chunks))
    ckvn = _rms(ckv, gkv_ref[...]).astype(BF16)
    kr_rot = kr * cos + _swap_halves(kr) * sin
    for j in range(C_PAIRS):
        cols = slice(j * C_PAIR_LANES, (j + 1) * C_PAIR_LANES)
        t = _dot(cqn, wuq_ref[:, cols])
        qct_ref[cols, :] = ((t * cos + _swap_halves(t) * sin) * C_SCORE_SCALE).T.astype(BF16)
        project(next(chunks))
    for j in range(C_PAIRS):
        cols = slice(j * C_PAIR_LANES, (j + 1) * C_PAIR_LANES)
        kc_ref[:, cols] = (_dot(ckvn, wuk_ref[:, cols]) + kr_rot).astype(BF16)
    vct_ref[...] = _dot(ckvn, wuv_ref[...]).T.astype(BF16)
    assert next(chunks, None) is None


def _input_stage(x2, layer, g, w_in, gq, wuq, gkv, wuk, wuv, wkr, cos_t, sin_t, seq):
    n = x2.shape[0]
    tm = ROW_TILE
    pos_blocks = seq // tm
    row = lambda i: (i, 0)
    col = lambda i: (0, i)
    return pl.pallas_call(
        _input_kernel,
        grid=(n // tm,),
        in_specs=[
            pl.BlockSpec((tm, D_MODEL), row),
            _layer_spec((1, D_MODEL), layer),
            _layer_spec((D_MODEL, w_in.shape[2]), layer),
            _layer_spec((1, C_Q_RANK), layer),
            _const_spec((C_Q_RANK, C_COLS)),
            _layer_spec((1, C_KV_RANK), layer),
            _const_spec((C_KV_RANK, C_COLS)),
            _const_spec((C_KV_RANK, C_WIDTH)),
            _const_spec((D_MODEL, C_PAIR_LANES)),
            pl.BlockSpec((tm, C_PAIR_LANES), lambda i: (i % pos_blocks, 0)),
            pl.BlockSpec((tm, C_PAIR_LANES), lambda i: (i % pos_blocks, 0)),
        ],
        out_specs=[
            pl.BlockSpec((tm, QB_COL0), row),
            pl.BlockSpec((tm, VB_COL0 - KB_COL0), row),
            pl.BlockSpec((B_WIDTH, tm), col),
            pl.BlockSpec((B_WIDTH, tm), col),
            pl.BlockSpec((C_COLS, tm), col),
            pl.BlockSpec((tm, C_COLS), row),
            pl.BlockSpec((C_WIDTH, tm), col),
        ],
        out_shape=[
            jax.ShapeDtypeStruct((n, QB_COL0), BF16),
            jax.ShapeDtypeStruct((n, VB_COL0 - KB_COL0), BF16),
            jax.ShapeDtypeStruct((B_WIDTH, n), BF16),
            jax.ShapeDtypeStruct((B_WIDTH, n), BF16),
            jax.ShapeDtypeStruct((C_COLS, n), BF16),
            jax.ShapeDtypeStruct((n, C_COLS), BF16),
            jax.ShapeDtypeStruct((C_WIDTH, n), BF16),
        ],
        compiler_params=_params(1),
        name="input_stage",
    )(x2, g, w_in, gq, wuq, gkv, wuk, wuv, wkr, cos_t, sin_t)


def _dot_tn(a, b):
    return lax.dot_general(a, b, (((0,), (0,)), ((), ())), preferred_element_type=F32)


def _dilated_kernel(q_ref, k_ref, v_ref, bias_ref, y_ref, qf_ref, kf_ref, vf_ref, m_ref, w_ref, a_ref):
    seq = q_ref.shape[0]
    chunk = 512

    def widen(c, carry):
        rows = pl.ds(pl.multiple_of(c * chunk, chunk), chunk)
        qf_ref[rows, :] = q_ref[rows, :].astype(F32)
        kf_ref[rows, :] = k_ref[rows, :].astype(F32)
        vf_ref[rows, :] = v_ref[rows, :].astype(F32)
        return carry

    lax.fori_loop(0, seq // chunk, widen, 0)

    tile = (A_BLOCK, A_BLOCK)
    low_lane = lax.broadcasted_iota(jnp.int32, tile, 1) < HEAD_DIM
    low_row = lax.broadcasted_iota(jnp.int32, tile, 0) < HEAD_DIM
    last = len(A_PATTERNS) - 1
    for g, (_, dil) in enumerate(A_PATTERNS):
        nb = seq // dil // A_BLOCK
        span = dil * A_BLOCK

        def attend(blk, g=g, dil=dil, nb=nb, span=span):
            r = blk // nb
            n = blk % nb
            base = r + n * span
            prev_base = jnp.maximum(base - span, r)
            rows = pl.ds(base, A_BLOCK, stride=dil)
            prev_rows = pl.ds(prev_base, A_BLOCK, stride=dil)
            q = qf_ref[rows, :]
            k2 = jnp.concatenate([kf_ref[prev_rows, :], kf_ref[rows, :]], axis=0).astype(BF16)
            v2 = jnp.concatenate([vf_ref[prev_rows, :], vf_ref[rows, :]], axis=0).astype(BF16)
            first_block = jnp.where(n == 0, 1, 0)
            outs, lses = [], []
            for hh in range(2):
                qm = jnp.where(low_lane if hh == 0 else ~low_lane, q, 0.0).astype(BF16)
                s = _dot_nt(k2, qm) + bias_ref[hh, g, first_block]
                m = jnp.max(s, axis=0, keepdims=True)
                p = jnp.exp2(s - m)
                l = jnp.sum(p, axis=0, keepdims=True)
                outs.append(_dot_tn(v2, p.astype(BF16)) * (1.0 / l))
                lses.append(jnp.broadcast_to(m + jnp.log2(l), tile))
            o_blk = jnp.where(low_row, outs[0], outs[1]).T
            lse_blk = jnp.where(low_row, lses[0], lses[1]).T
            return rows, o_blk, lse_blk

        def merge(rows, o_blk, lse_blk, g=g):
            if g == 0:
                m_ref[rows, :] = lse_blk
                a_ref[rows, :] = o_blk
                return
            m_old = m_ref[rows, :]
            m_new = jnp.maximum(m_old, lse_blk)
            c_old = jnp.exp2(m_old - m_new)
            c_blk = jnp.exp2(lse_blk - m_new)
            w_new = c_blk + (c_old if g == 1 else c_old * w_ref[rows, :])
            a_new = c_old * a_ref[rows, :] + c_blk * o_blk
            if g < last:
                m_ref[rows, :] = m_new
                w_ref[rows, :] = w_new
                a_ref[rows, :] = a_new
            else:
                a_ref[rows, :] = a_new / w_new

        def step(it, carry, attend=attend, merge=merge):
            done = [attend(it * A_BLOCKS_PER_STEP + u) for u in range(A_BLOCKS_PER_STEP)]
            for args in done:
                merge(*args)
            return carry

        lax.fori_loop(0, dil * nb // A_BLOCKS_PER_STEP, step, 0)

    def narrow(c, carry):
        rows = pl.ds(pl.multiple_of(c * chunk, chunk), chunk)
        y_ref[rows, :] = a_ref[rows, :].astype(BF16)
        return carry

    lax.fori_loop(0, seq // chunk, narrow, 0)


def _dilated_attention(qkv, bias_tiles, batch, seq):
    pairs = A_HEADS // 2
    col = lambda off: (lambda b, p: (b, off * pairs + p))
    state = pltpu.VMEM((seq, 128), F32)
    return pl.pallas_call(
        _dilated_kernel,
        grid=(batch, pairs),
        in_specs=[
            pl.BlockSpec((seq, 128), col(0)),
            pl.BlockSpec((seq, 128), col(1)),
            pl.BlockSpec((seq, 128), col(2)),
            pl.BlockSpec((2, len(A_PATTERNS), 2, 2 * A_BLOCK, A_BLOCK), lambda b, p: (p, 0, 0, 0, 0)),
        ],
        out_specs=pl.BlockSpec((seq, 128), lambda b, p: (b, p)),
        out_shape=jax.ShapeDtypeStruct((batch * seq, A_WIDTH), BF16),
        scratch_shapes=[state] * 6,
        compiler_params=_params(2),
        name="dilated_attention",
    )(qkv, qkv, qkv, bias_tiles)


def _flash_init(m_ref, acc_ref):
    m_ref[...] = jnp.full(m_ref.shape, NEG, F32)
    acc_ref[...] = jnp.zeros(acc_ref.shape, F32)


def _with_ones(v_t):
    return jnp.concatenate([v_t, jnp.ones((ONES_ROWS, v_t.shape[1]), v_t.dtype)], axis=0)


def _key_rows(part):
    return slice(part * KEY_PART, (part + 1) * KEY_PART)


def _store_scores(s, slot, idx, s_ref, cmax_ref):
    s_ref[slot, idx] = s
    cmax_ref[slot, idx] = jnp.max(s, axis=0, keepdims=True)


def _masked_part(slot, idx, part, masked, s_ref):
    s = s_ref[slot, idx, _key_rows(part), :]
    if masked:
        key = lax.broadcasted_iota(jnp.int32, s.shape, 0) + part * KEY_PART
        s = jnp.where(key <= lax.broadcasted_iota(jnp.int32, s.shape, 1), s, NEG)
    return s


def _flash_begin(slot, idx, masked, m_ref, s_ref, cmax_ref):
    if masked:
        col_max = functools.reduce(jnp.maximum, [
            jnp.max(_masked_part(slot, idx, part, True, s_ref), axis=0, keepdims=True)
            for part in range(ATT_TILE // KEY_PART)])
    else:
        col_max = cmax_ref[slot, idx]
    m_prev = m_ref[idx]
    m_new = jnp.maximum(m_prev, col_max)
    m_ref[idx] = m_new
    return m_new, jnp.exp2(m_prev - m_new)


def _flash_part(slot, idx, part, masked, m_new, v_ext, s_ref):
    p = jnp.exp2(_masked_part(slot, idx, part, masked, s_ref) - m_new)
    return _dot(v_ext[:, _key_rows(part)], p.astype(BF16))


def _flash_result(idx, dv, acc_ref):
    acc = acc_ref[idx]
    return acc[:dv] * (1.0 / acc[dv:dv + 1])


def _flash_sweep(n_tiles, scores, absorb, finish, m_ref, acc_ref, s_ref, cmax_ref):
    n_items = n_tiles * (n_tiles + 1) // 2
    assert n_items % 2 == 0

    def following(i, c):
        end = c == i
        return jnp.where(end, i + 1, i), jnp.where(end, 0, c + 1)

    parts = ATT_TILE // KEY_PART

    def item(nxt, cur, diag, slot):
        for idx in range(2):
            scores(*nxt, 1 - slot, idx)
            m_new, alpha = _flash_begin(slot, idx, diag, m_ref, s_ref, cmax_ref)
            pv = None
            for part in range(parts):
                piece = absorb(cur, slot, idx, part, diag, m_new)
                pv = piece if pv is None else pv + piece
            acc_ref[idx] = alpha * acc_ref[idx] + pv

    for idx in range(2):
        scores(0, 0, 0, idx)

    def two_items(u, carry):
        i0, c0 = carry
        i1, c1 = following(i0, c0)
        i2, c2 = following(i1, c1)
        i2s = jnp.minimum(i2, n_tiles - 1)

        def variant(diag0, diag1):
            def run():
                item((i1, c1), c0, diag0, 0)
                if diag0:
                    finish(i0)
                item((i2s, c2), c1, diag1, 1)
                if diag1:
                    finish(i1)
            return run

        kind = 2 * (c0 == i0).astype(jnp.int32) + (c1 == i1).astype(jnp.int32)
        lax.switch(kind, [variant(False, False), variant(False, True),
                          variant(True, False), variant(True, True)])
        return i2, c2

    lax.fori_loop(0, n_items // 2, two_items, (jnp.int32(0), jnp.int32(0)))


def _row_in_ranges(shape, ranges):
    row = lax.broadcasted_iota(jnp.int32, shape, 0)
    hit = None
    for lo, hi in ranges:
        r = (row >= lo) & (row < hi)
        hit = r if hit is None else hit | r
    return hit


def _diff_bias(bias_ref, d):
    sub = ATT_TILE // BIAS_TILE
    rows = [jnp.concatenate([bias_ref[0, jnp.clip(sub * d + qa - ka, 0, B_BIAS_TILES - 1)]
                             for qa in range(sub)], axis=1) for ka in range(sub)]
    return jnp.concatenate(rows, axis=0)


def _tile_cols(i):
    return pl.ds(pl.multiple_of(i * ATT_TILE, ATT_TILE), ATT_TILE)


def _diff_kernel(lam_init, qt_ref, k_ref, vt_ref, bias_ref, lam_ref, g_ref, y_ref,
                 qm_ref, m_ref, acc_ref, s_ref, cmax_ref):
    t = ATT_TILE
    n_tiles = qt_ref.shape[1] // t

    def split_queries(i, carry):
        qf = qt_ref[:, _tile_cols(i)].astype(F32)
        first = lax.broadcasted_iota(jnp.int32, qf.shape, 0) < B_QK_DIM
        qm_ref[0, :, _tile_cols(i)] = jnp.where(first, qf, 0.0).astype(BF16)
        qm_ref[1, :, _tile_cols(i)] = jnp.where(first, 0.0, qf).astype(BF16)
        return carry

    lax.fori_loop(0, n_tiles, split_queries, 0)
    lv = lam_ref[...]
    lam = (jnp.exp(jnp.sum(lv[0:1] * lv[1:2], axis=-1, keepdims=True))
           - jnp.exp(jnp.sum(lv[2:3] * lv[3:4], axis=-1, keepdims=True)) + lam_init)
    _flash_init(m_ref, acc_ref)

    def scores(i, c, slot, idx):
        k = k_ref[_tile_cols(c), :]
        s = _dot(k, qm_ref[idx, :, _tile_cols(i)]) + _diff_bias(bias_ref, i - c)
        _store_scores(s, slot, idx, s_ref, cmax_ref)

    def absorb(c, slot, idx, part, masked, m_new):
        v_ext = _with_ones(vt_ref[:, _tile_cols(c)])
        return _flash_part(slot, idx, part, masked, m_new, v_ext, s_ref)

    def finish(i):
        o = _flash_result(0, B_V_DIM, acc_ref) - lam * _flash_result(1, B_V_DIM, acc_ref)
        y = o * lax.rsqrt(jnp.mean(o * o, axis=0, keepdims=True) + EPS) * g_ref[...] * (1.0 - lam_init)
        y_ref[_tile_cols(i), :] = y.T.astype(BF16)
        _flash_init(m_ref, acc_ref)

    _flash_sweep(n_tiles, scores, absorb, finish, m_ref, acc_ref, s_ref, cmax_ref)


def _diff_attention(qb_t, kb, vb_t, bias_tiles, lam_vecs, g_col, batch, seq, lam_init):
    t = ATT_TILE
    return pl.pallas_call(
        functools.partial(_diff_kernel, lam_init),
        grid=(batch, B_HEADS),
        in_specs=[
            pl.BlockSpec((128, seq), lambda b, h: (h, b)),
            pl.BlockSpec((seq, 128), lambda b, h: (b, h)),
            pl.BlockSpec((128, seq), lambda b, h: (h, b)),
            pl.BlockSpec((1, B_BIAS_TILES, BIAS_TILE, BIAS_TILE), lambda b, h: (h, 0, 0, 0)),
            _const_spec((4, B_QK_DIM)),
            _const_spec((B_V_DIM, 1)),
        ],
        out_specs=pl.BlockSpec((seq, 128), lambda b, h: (b, h)),
        out_shape=jax.ShapeDtypeStruct((batch * seq, B_WIDTH), BF16),
        scratch_shapes=[
            pltpu.VMEM((2, 128, seq), BF16),
            pltpu.VMEM((2, 1, t), F32),
            pltpu.VMEM((2, B_V_DIM + ONES_ROWS, t), F32),
            pltpu.VMEM((2, 2, t, t), F32),
            pltpu.VMEM((2, 2, 1, t), F32),
        ],
        compiler_params=_params(2),
        name="diff_attention",
    )(qb_t, kb, vb_t, bias_tiles, lam_vecs, g_col)


def _mla_kernel(qt_ref, k_ref, vt_ref, y_ref, qm_ref, m_ref, acc_ref, s_ref, cmax_ref):
    t = ATT_TILE
    n_tiles = qt_ref.shape[1] // t

    def split_heads(i, carry):
        qf = qt_ref[:, _tile_cols(i)].astype(F32)
        qm_ref[0, :, _tile_cols(i)] = jnp.where(_row_in_ranges(qf.shape, C_HEAD_A_RANGES), qf, 0.0).astype(BF16)
        qm_ref[1, :, _tile_cols(i)] = jnp.where(_row_in_ranges(qf.shape, C_HEAD_B_RANGES), qf, 0.0).astype(BF16)
        return carry

    lax.fori_loop(0, n_tiles, split_heads, 0)
    _flash_init(m_ref, acc_ref)

    def scores(i, c, slot, idx):
        k = k_ref[_tile_cols(c), :]
        _store_scores(_dot(k, qm_ref[idx, :, _tile_cols(i)]), slot, idx, s_ref, cmax_ref)

    def absorb(c, slot, idx, part, masked, m_new):
        v_t = vt_ref[idx * C_V:(idx + 1) * C_V, _tile_cols(c)]
        return _flash_part(slot, idx, part, masked, m_new, _with_ones(v_t), s_ref)

    def finish(i):
        o = jnp.concatenate([_flash_result(0, C_V, acc_ref), _flash_result(1, C_V, acc_ref)], axis=0)
        y_ref[_tile_cols(i), :] = o.T.astype(BF16)
        _flash_init(m_ref, acc_ref)

    _flash_sweep(n_tiles, scores, absorb, finish, m_ref, acc_ref, s_ref, cmax_ref)


def _mla_attention(qc_t, kc, vc_t, batch, seq):
    t = ATT_TILE
    return pl.pallas_call(
        _mla_kernel,
        grid=(batch, C_PAIRS),
        in_specs=[
            pl.BlockSpec((C_PAIR_LANES, seq), lambda b, j: (j, b)),
            pl.BlockSpec((seq, C_PAIR_LANES), lambda b, j: (b, j)),
            pl.BlockSpec((2 * C_V, seq), lambda b, j: (j, b)),
        ],
        out_specs=pl.BlockSpec((seq, 2 * C_V), lambda b, j: (b, j)),
        out_shape=jax.ShapeDtypeStruct((batch * seq, C_WIDTH), BF16),
        scratch_shapes=[
            pltpu.VMEM((2, C_PAIR_LANES, seq), BF16),
            pltpu.VMEM((2, 1, t), F32),
            pltpu.VMEM((2, C_V + ONES_ROWS, t), F32),
            pltpu.VMEM((2, 2, t, t), F32),
            pltpu.VMEM((2, 2, 1, t), F32),
        ],
        compiler_params=_params(2),
        name="mla_attention",
    )(qc_t, kc, vc_t)


def _sigmoid(z):
    return 1.0 / (1.0 + jnp.exp(-z))


def _merge_kernel(x_ref, g_ref, ya_ref, yb_ref, yc_ref, wg_ref, bg_ref, wa_ref, wb_ref, wc_ref,
                  wo_ref, out_ref):
    x = x_ref[...]
    h = _rms(x, g_ref[...]).astype(BF16)
    merged = jnp.zeros(x.shape, F32)
    for k, (y_ref, w_ref) in enumerate(((ya_ref, wa_ref), (yb_ref, wb_ref), (yc_ref, wc_ref))):
        cols = slice(k * D_MODEL, (k + 1) * D_MODEL)
        gate = _sigmoid(_dot(h, wg_ref[:, cols]) + bg_ref[:, cols])
        merged = merged + gate * _dot(y_ref[...], w_ref[...])
    out_ref[...] = x + _dot(merged.astype(BF16), wo_ref[...])


def _merge_stage(x2, layer, g, ya, yb, yc, wg, bg, wa, wb, wc, wo):
    n = x2.shape[0]
    tm = ROW_TILE
    row = lambda i: (i, 0)
    return pl.pallas_call(
        _merge_kernel,
        grid=(n // tm,),
        in_specs=[
            pl.BlockSpec((tm, D_MODEL), row),
            _layer_spec((1, D_MODEL), layer),
            pl.BlockSpec((tm, A_WIDTH), row),
            pl.BlockSpec((tm, B_WIDTH), row),
            pl.BlockSpec((tm, C_WIDTH), row),
            _layer_spec((D_MODEL, N_BRANCH * D_MODEL), layer),
            _layer_spec((1, N_BRANCH * D_MODEL), layer),
            _layer_spec((A_WIDTH, D_MODEL), layer),
            _layer_spec((B_WIDTH, D_MODEL), layer),
            _layer_spec((C_WIDTH, D_MODEL), layer),
            _layer_spec((D_MODEL, D_MODEL), layer),
        ],
        out_specs=pl.BlockSpec((tm, D_MODEL), row),
        out_shape=jax.ShapeDtypeStruct((n, D_MODEL), F32),
        compiler_params=_params(1),
        name="gated_merge",
    )(x2, g, ya, yb, yc, wg, bg, wa, wb, wc, wo)


def _ffn_kernel(final_norm, x_ref, g_ref, wg_ref, wu_ref, wd_ref, gf_ref, out_ref):
    x = x_ref[...]
    h = _rms(x, g_ref[...]).astype(BF16)
    acc = jnp.zeros(x.shape, F32)
    for c in range(0, FFN_HIDDEN, FFN_CHUNK):
        cols = slice(c, c + FFN_CHUNK)
        gate = _dot(h, wg_ref[:, cols])
        act = gate * _sigmoid(gate) * _dot(h, wu_ref[:, cols])
        acc = acc + _dot(act.astype(BF16), wd_ref[cols, :])
    y = x + acc
    if final_norm:
        y = _rms(y, gf_ref[...])
    out_ref[...] = y


def _ffn_stage(x2, layer, g, wg, wu, wd, gf, final_norm):
    n = x2.shape[0]
    tm = ROW_TILE
    row = lambda i: (i, 0)
    return pl.pallas_call(
        functools.partial(_ffn_kernel, final_norm),
        grid=(n // tm,),
        in_specs=[
            pl.BlockSpec((tm, D_MODEL), row),
            _layer_spec((1, D_MODEL), layer),
            _layer_spec((D_MODEL, FFN_HIDDEN), layer),
            _layer_spec((D_MODEL, FFN_HIDDEN), layer),
            _layer_spec((FFN_HIDDEN, D_MODEL), layer),
            _const_spec((1, D_MODEL)),
        ],
        out_specs=pl.BlockSpec((tm, D_MODEL), row),
        out_shape=jax.ShapeDtypeStruct((n, D_MODEL), F32),
        compiler_params=_params(1),
        name="swiglu_final" if final_norm else "swiglu",
    )(x2, g, wg, wu, wd, gf)


def _gather_cols(w, idx):
    idx = np.asarray(idx)
    zero = idx == w.shape[1]
    cuts = [0] + [j for j in range(1, len(idx))
                  if zero[j] != zero[j - 1] or (not zero[j] and idx[j] != idx[j - 1] + 1)]
    pieces = []
    for lo, hi in zip(cuts, cuts[1:] + [len(idx)]):
        if idx[lo] == w.shape[1]:
            pieces.append(jnp.zeros((w.shape[0], hi - lo), w.dtype))
        else:
            pieces.append(w[:, int(idx[lo]):int(idx[lo]) + hi - lo])
    return jnp.concatenate(pieces, axis=1)


def kernel(x, rel_bias_table, ln_mix_g, w_in, lambda_q1, lambda_k1, lambda_q2, lambda_k2, diff_subln_g, mla_q_norm_g, w_uq, mla_kv_norm_g, w_ukv, w_gate, b_gate, w_br_a, w_br_b, w_br_c, w_o, ln_ffn_g, w_ffn_gate, w_ffn_up, w_ffn_down, final_norm_g):
    batch, seq, _ = x.shape
    depth = w_in.shape[0]
    assert all(seq % (A_BLOCK * dil) == 0 for _, dil in A_PATTERNS)
    assert (seq // A_BLOCK) % A_BLOCKS_PER_STEP == 0
    assert seq % ATT_TILE == 0 and seq % ROW_TILE == 0
    lay = _C_LAYOUT
    cos_t, sin_t = _rope_tables(seq)

    a_tiles = _bias_expand(rel_bias_table, _a_bucket_index(), A_HEADS, 0, 2 * A_BLOCK, "bias_expand_a",
                           gain=LOG2E)
    a_tiles = a_tiles.reshape(A_HEADS, len(A_PATTERNS), 2, 2 * A_BLOCK, A_BLOCK)
    b_tiles = _bias_expand(rel_bias_table, _b_bucket_index(), B_HEADS, A_HEADS, BIAS_TILE,
                           "bias_expand_b", gain=LOG2E).reshape(B_HEADS, B_BIAS_TILES, BIAS_TILE, BIAS_TILE)

    rows = lambda v: v[:, None, :]
    w_in16, w_gate16, w_o16 = _to_bf16(w_in), _to_bf16(w_gate), _to_bf16(w_o)
    w_br16 = [_to_bf16(w) for w in (w_br_a, w_br_b, w_br_c)]
    w_ffn16 = [_to_bf16(w) for w in (w_ffn_gate, w_ffn_up, w_ffn_down)]
    g_mix, g_q, g_kv, g_ffn, b_gate3 = (rows(v) for v in (ln_mix_g, mla_q_norm_g, mla_kv_norm_g, ln_ffn_g, b_gate))

    x2 = x.reshape(batch * seq, D_MODEL)
    for l in range(depth):
        qkv_a, kb, qb_t, vb_t, qc_t, kc, vc_t = _input_stage(
            x2, l, g_mix, w_in16, g_q,
            _gather_cols(w_uq[l], lay["q"]).astype(BF16), g_kv,
            _gather_cols(w_ukv[l], lay["k"]).astype(BF16),
            _gather_cols(w_ukv[l], lay["v"]).astype(BF16),
            _gather_cols(w_in[l][:, KR_COL0:], lay["kr"]).astype(BF16),
            cos_t, sin_t, seq)

        ya = _dilated_attention(qkv_a, a_tiles, batch, seq)

        lam_init = 0.8 - 0.6 * math.exp(-0.3 * l)
        lam_vecs = jnp.stack([lambda_q1[l], lambda_k1[l], lambda_q2[l], lambda_k2[l]]).astype(F32)
        yb = _diff_attention(qb_t, kb, vb_t, b_tiles, lam_vecs, diff_subln_g[l][:, None],
                             batch, seq, lam_init)
        yc = _mla_attention(qc_t, kc, vc_t, batch, seq)

        x2 = _merge_stage(x2, l, g_mix, ya, yb, yc, w_gate16, b_gate3, *w_br16, w_o16)
        x2 = _ffn_stage(x2, l, g_ffn, *w_ffn16, final_norm_g[None, :], final_norm=(l == depth - 1))
    return x2.reshape(batch, seq, D_MODEL)
```

```python
import functools
import math

import numpy as np
import jax
import jax.numpy as jnp
from jax import lax
from jax.experimental import pallas as pl
from jax.experimental.pallas import tpu as pltpu

D_MODEL = 1024
HEAD_DIM = 64
A_HEADS = 8
A_PATTERNS = ((2048, 16), (512, 4), (128, 1))
A_WIDTH = A_HEADS * HEAD_DIM
A_BLOCK = 128
A_BLOCKS_PER_STEP = 16
B_HEADS = 4
B_QK_DIM = HEAD_DIM
B_V_DIM = 2 * HEAD_DIM
B_WIDTH = B_HEADS * B_V_DIM
C_HEADS = 8
C_PAIRS = C_HEADS // 2
C_Q_RANK = 768
C_KV_RANK = 256
C_NOPE = 64
C_ROPE = 32
C_V = 64
C_WIDTH = C_HEADS * C_V
C_PAIR_LANES = 256
C_COLS = C_PAIRS * C_PAIR_LANES
ROPE_THETA = 10000.0
REL_BUCKETS = 32
REL_MAX_DIST = 2048
AB_COLS = 3 * A_WIDTH + 4 * B_HEADS * B_QK_DIM + B_WIDTH
QB_COL0 = 3 * A_WIDTH
KB_COL0 = QB_COL0 + 2 * B_HEADS * B_QK_DIM
VB_COL0 = KB_COL0 + 2 * B_HEADS * B_QK_DIM
CQ_COL0 = AB_COLS
CKV_COL0 = CQ_COL0 + C_Q_RANK
KR_COL0 = CKV_COL0 + C_KV_RANK
N_BRANCH = 3
FFN_HIDDEN = 2816
EPS = 1e-6
NEG = -1e30
LOG2E = math.log2(math.e)
A_SCORE_SCALE = HEAD_DIM ** -0.5 * LOG2E
B_SCORE_SCALE = B_QK_DIM ** -0.5 * LOG2E
C_SCORE_SCALE = (C_NOPE + C_ROPE) ** -0.5 * LOG2E
ONES_ROWS = 16

ROW_TILE = 512
ATT_TILE = 512
KEY_PART = 256
BIAS_TILE = 256
B_BIAS_TILES = 8
FFN_CHUNK = 256
VMEM_LIMIT = 56 * 1024 * 1024

F32 = jnp.float32
BF16 = jnp.bfloat16


def _dot(a, b):
    return jnp.dot(a, b, preferred_element_type=F32)


def _dot_nt(a, b):
    return lax.dot_general(a, b, (((1,), (1,)), ((), ())), preferred_element_type=F32)


def _rms(x, g):
    return x * lax.rsqrt(jnp.mean(x * x, axis=-1, keepdims=True) + EPS) * g


def _swap_halves(t):
    half = t.shape[1] // 2
    return jnp.concatenate([t[:, half:], t[:, :half]], axis=1)


def _const_spec(shape):
    nd = len(shape)
    return pl.BlockSpec(shape, lambda *_: (0,) * nd, pipeline_mode=pl.Buffered(1))


def _layer_spec(shape, layer):
    nd = len(shape)
    return pl.BlockSpec((None,) + tuple(shape), lambda *_: (layer,) + (0,) * nd, pipeline_mode=pl.Buffered(1))


def _cast_kernel(w_ref, o_ref):
    o_ref[...] = w_ref[...].astype(o_ref.dtype)


def _to_bf16(w):
    depth, rows, cols = w.shape
    block_rows = next(r for r in (512, 704, rows) if rows % r == 0)
    spec = pl.BlockSpec((1, block_rows, cols), lambda l, i: (l, i, 0))
    return pl.pallas_call(
        _cast_kernel,
        grid=(depth, rows // block_rows),
        in_specs=[spec],
        out_specs=spec,
        out_shape=jax.ShapeDtypeStruct(w.shape, BF16),
        compiler_params=_params(2),
        name="weights_to_bf16",
    )(w)


def _params(n_axes):
    return pltpu.CompilerParams(dimension_semantics=("parallel",) * n_axes,
                                vmem_limit_bytes=VMEM_LIMIT)


def _t5_bucket_np(dist):
    dist = np.maximum(dist, 0)
    exact = REL_BUCKETS // 2
    ratio = np.maximum(dist, 1).astype(np.float32) / np.float32(exact)
    log_ratio = np.log(ratio).astype(np.float32) / np.float32(math.log(REL_MAX_DIST / exact))
    large = np.minimum(exact + (log_ratio * np.float32(REL_BUCKETS - exact)).astype(np.int32),
                       REL_BUCKETS - 1)
    return np.where(dist < exact, dist, large).astype(np.int32)


MASKED_BUCKET = REL_BUCKETS


def _a_bucket_index():
    kj = np.arange(2 * A_BLOCK)[:, None]
    qi = np.arange(A_BLOCK)[None, :]
    step = qi + A_BLOCK - kj
    in_band = (step >= 0) & (step <= A_BLOCK)
    tiles = []
    for _, dil in A_PATTERNS:
        bucket = _t5_bucket_np(step * dil)
        for first_block in (False, True):
            valid = in_band & (kj >= A_BLOCK) if first_block else in_band
            tiles.append(np.where(valid, bucket, MASKED_BUCKET))
    return np.concatenate(tiles, axis=0).astype(np.int32)


def _b_bucket_index():
    kj = np.arange(BIAS_TILE)[:, None]
    qi = np.arange(BIAS_TILE)[None, :]
    e = np.arange(B_BIAS_TILES)[:, None, None]
    idx = _t5_bucket_np(e * BIAS_TILE + qi - kj)
    assert (idx[-1] == REL_BUCKETS - 1).all()
    return idx.reshape(B_BIAS_TILES * BIAS_TILE, BIAS_TILE)


def _c_layout():
    hq = C_NOPE + C_ROPE
    q_zero = C_HEADS * hq
    kv_zero = C_HEADS * (C_NOPE + C_V)
    half = C_ROPE // 2
    q_idx = np.full((C_PAIRS, C_PAIR_LANES), q_zero, np.int32)
    k_idx = np.full((C_PAIRS, C_PAIR_LANES), kv_zero, np.int32)
    for j in range(C_PAIRS):
        a, b = 2 * j, 2 * j + 1
        q_idx[j, 0:64] = a * hq + np.arange(64)
        q_idx[j, 64:80] = a * hq + C_NOPE + np.arange(half)
        q_idx[j, 80:96] = b * hq + C_NOPE + np.arange(half)
        q_idx[j, 128:192] = b * hq + np.arange(64)
        q_idx[j, 192:208] = a * hq + C_NOPE + half + np.arange(half)
        q_idx[j, 208:224] = b * hq + C_NOPE + half + np.arange(half)
        k_idx[j, 0:64] = a * (C_NOPE + C_V) + np.arange(64)
        k_idx[j, 128:192] = b * (C_NOPE + C_V) + np.arange(64)
    v_idx = (np.arange(C_HEADS)[:, None] * (C_NOPE + C_V) + C_NOPE + np.arange(C_V)[None, :])
    kr_idx = np.full((C_PAIR_LANES,), C_ROPE, np.int32)
    kr_idx[64:80] = np.arange(half)
    kr_idx[80:96] = np.arange(half)
    kr_idx[192:208] = half + np.arange(half)
    kr_idx[208:224] = half + np.arange(half)
    rope_lane = np.full((C_PAIR_LANES,), -1, np.int32)
    for start in (64, 80, 192, 208):
        rope_lane[start:start + half] = np.arange(half)
    sin_sign = np.zeros((C_PAIR_LANES,), np.float32)
    sin_sign[64:96] = -1.0
    sin_sign[192:224] = 1.0
    return dict(q=q_idx.reshape(-1), k=k_idx.reshape(-1), v=v_idx.reshape(-1), kr=kr_idx,
                rope_lane=rope_lane, sin_sign=sin_sign)


_C_LAYOUT = _c_layout()
C_HEAD_A_RANGES = ((0, 80), (192, 208))
C_HEAD_B_RANGES = ((80, 96), (128, 192), (208, 224))


def _rope_tables(seq):
    pos = np.arange(seq, dtype=np.float64)
    inv_freq = ROPE_THETA ** (-np.arange(0, C_ROPE, 2, dtype=np.float64) / C_ROPE)
    ang = pos[:, None] * inv_freq[None, :]
    cos, sin = np.cos(ang), np.sin(ang)
    lane = _C_LAYOUT["rope_lane"]
    gather = np.maximum(lane, 0)
    cos_t = np.where((lane >= 0)[None, :], cos[:, gather], 1.0)
    sin_t = sin[:, gather] * _C_LAYOUT["sin_sign"][None, :]
    return jnp.asarray(cos_t, F32), jnp.asarray(sin_t, F32)


def _bias_expand_kernel(n_heads, table_ref, idx_ref, out_ref):
    rows, cols = idx_ref.shape
    for c in range(0, cols, 128):
        idx = idx_ref[:, c:c + 128]
        for h in range(n_heads):
            head_table = jnp.broadcast_to(table_ref[h:h + 1, :], (rows, 128))
            out_ref[h, :, c:c + 128] = jnp.take_along_axis(head_table, idx, axis=1)


def _bias_expand(table, idx, n_heads, head0, block_rows, name, gain=1.0):
    rows, cols = idx.shape
    lanes = jnp.full((n_heads, 128), NEG, F32)
    lanes = lanes.at[:, :REL_BUCKETS].set(table[:, head0:head0 + n_heads].T * gain)
    return pl.pallas_call(
        functools.partial(_bias_expand_kernel, n_heads),
        grid=(rows // block_rows,),
        in_specs=[
            _const_spec((n_heads, 128)),
            pl.BlockSpec((block_rows, cols), lambda i: (i, 0)),
        ],
        out_specs=pl.BlockSpec((n_heads, block_rows, cols), lambda i: (0, i, 0)),
        out_shape=jax.ShapeDtypeStruct((n_heads, rows, cols), F32),
        compiler_params=_params(1),
        name=name,
    )(lanes, jnp.asarray(idx))


def _input_kernel(x_ref, g_ref, win_ref, gq_ref, wuq_ref, gkv_ref, wuk_ref,
                  wuv_ref, wkr_ref, cos_ref, sin_ref,
                  qkva_ref, kb_ref, qbt_ref, vbt_ref, qct_ref, kc_ref, vct_ref):
    h = _rms(x_ref[...], g_ref[...]).astype(BF16)
    cos = cos_ref[...]
    sin = sin_ref[...]

    def project(c):
        t = _dot(h, win_ref[:, c:c + 512])
        if c < QB_COL0:
            qkva_ref[:, c:c + 512] = (t * A_SCORE_SCALE if c == 0 else t).astype(BF16)
        elif c == QB_COL0:
            qbt_ref[...] = (t * B_SCORE_SCALE).T.astype(BF16)
        elif c == KB_COL0:
            kb_ref[...] = t.astype(BF16)
        else:
            vbt_ref[...] = t.T.astype(BF16)

    chunks = iter(range(0, AB_COLS, 512))
    cq = _dot(h, win_ref[:, CQ_COL0:CKV_COL0])
    ckv = _dot(h, win_ref[:, CKV_COL0:KR_COL0])
    kr = _dot(h, wkr_ref[...])
    project(next(chunks))
    cqn = _rms(cq, gq_ref[...]).astype(BF16)
    project(next(chunks))
    ckvn = _rms(ckv, gkv_ref[...]).astype(BF16)
    kr_rot = kr * cos + _swap_halves(kr) * sin
    for j in range(C_PAIRS):
        cols = slice(j * C_PAIR_LANES, (j + 1) * C_PAIR_LANES)
        t = _dot(cqn, wuq_ref[:, cols])
        qct_ref[cols, :] = ((t * cos + _swap_halves(t) * sin) * C_SCORE_SCALE).T.astype(BF16)
        project(next(chunks))
    for j in range(C_PAIRS):
        cols = slice(j * C_PAIR_LANES, (j + 1) * C_PAIR_LANES)
        kc_ref[:, cols] = (_dot(ckvn, wuk_ref[:, cols]) + kr_rot).astype(BF16)
    vct_ref[...] = _dot(ckvn, wuv_ref[...]).T.astype(BF16)
    assert next(chunks, None) is None


def _input_stage(x2, layer, g, w_in, gq, wuq, gkv, wuk, wuv, wkr, cos_t, sin_t, seq):
    n = x2.shape[0]
    tm = ROW_TILE
    pos_blocks = seq // tm
    row = lambda i: (i, 0)
    col = lambda i: (0, i)
    return pl.pallas_call(
        _input_kernel,
        grid=(n // tm,),
        in_specs=[
            pl.BlockSpec((tm, D_MODEL), row),
            _layer_spec((1, D_MODEL), layer),
            _const_spec((D_MODEL, KR_COL0)),
            _layer_spec((1, C_Q_RANK), layer),
            _const_spec((C_Q_RANK, C_COLS)),
            _layer_spec((1, C_KV_RANK), layer),
            _const_spec((C_KV_RANK, C_COLS)),
            _const_spec((C_KV_RANK, C_WIDTH)),
            _const_spec((D_MODEL, C_PAIR_LANES)),
            pl.BlockSpec((tm, C_PAIR_LANES), lambda i: (i % pos_blocks, 0)),
            pl.BlockSpec((tm, C_PAIR_LANES), lambda i: (i % pos_blocks, 0)),
        ],
        out_specs=[
            pl.BlockSpec((tm, QB_COL0), row),
            pl.BlockSpec((tm, VB_COL0 - KB_COL0), row),
            pl.BlockSpec((B_WIDTH, tm), col),
            pl.BlockSpec((B_WIDTH, tm), col),
            pl.BlockSpec((C_COLS, tm), col),
            pl.BlockSpec((tm, C_COLS), row),
            pl.BlockSpec((C_WIDTH, tm), col),
        ],
        out_shape=[
            jax.ShapeDtypeStruct((n, QB_COL0), BF16),
            jax.ShapeDtypeStruct((n, VB_COL0 - KB_COL0), BF16),
            jax.ShapeDtypeStruct((B_WIDTH, n), BF16),
            jax.ShapeDtypeStruct((B_WIDTH, n), BF16),
            jax.ShapeDtypeStruct((C_COLS, n), BF16),
            jax.ShapeDtypeStruct((n, C_COLS), BF16),
            jax.ShapeDtypeStruct((C_WIDTH, n), BF16),
        ],
        compiler_params=_params(1),
        name="input_stage",
    )(x2, g, w_in, gq, wuq, gkv, wuk, wuv, wkr, cos_t, sin_t)


def _dot_tn(a, b):
    return lax.dot_general(a, b, (((0,), (0,)), ((), ())), preferred_element_type=F32)


def _dilated_kernel(q_ref, k_ref, v_ref, bias_ref, y_ref, qf_ref, kf_ref, vf_ref, m_ref, w_ref, a_ref):
    seq = q_ref.shape[0]
    chunk = 512

    def widen(c, carry):
        rows = pl.ds(pl.multiple_of(c * chunk, chunk), chunk)
        qf_ref[rows, :] = q_ref[rows, :].astype(F32)
        kf_ref[rows, :] = k_ref[rows, :].astype(F32)
        vf_ref[rows, :] = v_ref[rows, :].astype(F32)
        return carry

    lax.fori_loop(0, seq // chunk, widen, 0)

    tile = (A_BLOCK, A_BLOCK)
    low_lane = lax.broadcasted_iota(jnp.int32, tile, 1) < HEAD_DIM
    low_row = lax.broadcasted_iota(jnp.int32, tile, 0) < HEAD_DIM
    last = len(A_PATTERNS) - 1
    for g, (_, dil) in enumerate(A_PATTERNS):
        nb = seq // dil // A_BLOCK
        span = dil * A_BLOCK

        def attend(blk, g=g, dil=dil, nb=nb, span=span):
            r = blk // nb
            n = blk % nb
            base = r + n * span
            prev_base = jnp.maximum(base - span, r)
            rows = pl.ds(base, A_BLOCK, stride=dil)
            prev_rows = pl.ds(prev_base, A_BLOCK, stride=dil)
            q = qf_ref[rows, :]
            k2 = jnp.concatenate([kf_ref[prev_rows, :], kf_ref[rows, :]], axis=0).astype(BF16)
            v2 = jnp.concatenate([vf_ref[prev_rows, :], vf_ref[rows, :]], axis=0).astype(BF16)
            first_block = jnp.where(n == 0, 1, 0)
            outs, lses = [], []
            for hh in range(2):
                qm = jnp.where(low_lane if hh == 0 else ~low_lane, q, 0.0).astype(BF16)
                s = _dot_nt(k2, qm) + bias_ref[hh, g, first_block]
                m = jnp.max(s, axis=0, keepdims=True)
                p = jnp.exp2(s - m)
                l = jnp.sum(p, axis=0, keepdims=True)
                outs.append(_dot_tn(v2, p.astype(BF16)) * (1.0 / l))
                lses.append(jnp.broadcast_to(m + jnp.log2(l), tile))
            o_blk = jnp.where(low_row, outs[0], outs[1]).T
            lse_blk = jnp.where(low_row, lses[0], lses[1]).T
            return rows, o_blk, lse_blk

        def merge(rows, o_blk, lse_blk, g=g):
            if g == 0:
                m_ref[rows, :] = lse_blk
                a_ref[rows, :] = o_blk
                return
            m_old = m_ref[rows, :]
            m_new = jnp.maximum(m_old, lse_blk)
            c_old = jnp.exp2(m_old - m_new)
            c_blk = jnp.exp2(lse_blk - m_new)
            w_new = c_blk + (c_old if g == 1 else c_old * w_ref[rows, :])
            a_new = c_old * a_ref[rows, :] + c_blk * o_blk
            if g < last:
                m_ref[rows, :] = m_new
                w_ref[rows, :] = w_new
                a_ref[rows, :] = a_new
            else:
                a_ref[rows, :] = a_new / w_new

        def step(it, carry, attend=attend, merge=merge):
            done = [attend(it * A_BLOCKS_PER_STEP + u) for u in range(A_BLOCKS_PER_STEP)]
            for args in done:
                merge(*args)
            return carry

        lax.fori_loop(0, dil * nb // A_BLOCKS_PER_STEP, step, 0)

    def narrow(c, carry):
        rows = pl.ds(pl.multiple_of(c * chunk, chunk), chunk)
        y_ref[rows, :] = a_ref[rows, :].astype(BF16)
        return carry

    lax.fori_loop(0, seq // chunk, narrow, 0)


def _dilated_attention(qkv, bias_tiles, batch, seq):
    pairs = A_HEADS // 2
    col = lambda off: (lambda b, p: (b, off * pairs + p))
    state = pltpu.VMEM((seq, 128), F32)
    return pl.pallas_call(
        _dilated_kernel,
        grid=(batch, pairs),
        in_specs=[
            pl.BlockSpec((seq, 128), col(0)),
            pl.BlockSpec((seq, 128), col(1)),
            pl.BlockSpec((seq, 128), col(2)),
            pl.BlockSpec((2, len(A_PATTERNS), 2, 2 * A_BLOCK, A_BLOCK), lambda b, p: (p, 0, 0, 0, 0)),
        ],
        out_specs=pl.BlockSpec((seq, 128), lambda b, p: (b, p)),
        out_shape=jax.ShapeDtypeStruct((batch * seq, A_WIDTH), BF16),
        scratch_shapes=[state] * 6,
        compiler_params=_params(2),
        name="dilated_attention",
    )(qkv, qkv, qkv, bias_tiles)


def _flash_init(m_ref, acc_ref):
    m_ref[...] = jnp.full(m_ref.shape, NEG, F32)
    acc_ref[...] = jnp.zeros(acc_ref.shape, F32)


def _with_ones(v_t):
    return jnp.concatenate([v_t, jnp.ones((ONES_ROWS, v_t.shape[1]), v_t.dtype)], axis=0)


def _key_rows(part):
    return slice(part * KEY_PART, (part + 1) * KEY_PART)


def _store_scores(s, slot, idx, s_ref, cmax_ref):
    s_ref[slot, idx] = s
    cmax_ref[slot, idx] = jnp.max(s, axis=0, keepdims=True)


def _masked_part(slot, idx, part, masked, s_ref):
    s = s_ref[slot, idx, _key_rows(part), :]
    if masked:
        key = lax.broadcasted_iota(jnp.int32, s.shape, 0) + part * KEY_PART
        s = jnp.where(key <= lax.broadcasted_iota(jnp.int32, s.shape, 1), s, NEG)
    return s


def _flash_begin(slot, idx, masked, m_ref, s_ref, cmax_ref):
    if masked:
        col_max = functools.reduce(jnp.maximum, [
            jnp.max(_masked_part(slot, idx, part, True, s_ref), axis=0, keepdims=True)
            for part in range(ATT_TILE // KEY_PART)])
    else:
        col_max = cmax_ref[slot, idx]
    m_prev = m_ref[idx]
    m_new = jnp.maximum(m_prev, col_max)
    m_ref[idx] = m_new
    return m_new, jnp.exp2(m_prev - m_new)


def _flash_part(slot, idx, part, masked, m_new, v_ext, s_ref):
    p = jnp.exp2(_masked_part(slot, idx, part, masked, s_ref) - m_new)
    return _dot(v_ext[:, _key_rows(part)], p.astype(BF16))


def _flash_result(idx, dv, acc_ref):
    acc = acc_ref[idx]
    return acc[:dv] * (1.0 / acc[dv:dv + 1])


def _flash_sweep(n_tiles, scores, absorb, finish, m_ref, acc_ref, s_ref, cmax_ref):
    n_items = n_tiles * (n_tiles + 1) // 2
    assert n_items % 2 == 0

    def following(i, c):
        end = c == i
        return jnp.where(end, i + 1, i), jnp.where(end, 0, c + 1)

    parts = ATT_TILE // KEY_PART

    def item(nxt, cur, diag, slot):
        for idx in range(2):
            scores(*nxt, 1 - slot, idx)
            m_new, alpha = _flash_begin(slot, idx, diag, m_ref, s_ref, cmax_ref)
            pv = None
            for part in range(parts):
                piece = absorb(cur, slot, idx, part, diag, m_new)
                pv = piece if pv is None else pv + piece
            acc_ref[idx] = alpha * acc_ref[idx] + pv

    for idx in range(2):
        scores(0, 0, 0, idx)

    def two_items(u, carry):
        i0, c0 = carry
        i1, c1 = following(i0, c0)
        i2, c2 = following(i1, c1)
        i2s = jnp.minimum(i2, n_tiles - 1)

        def variant(diag0, diag1):
            def run():
                item((i1, c1), c0, diag0, 0)
                if diag0:
                    finish(i0)
                item((i2s, c2), c1, diag1, 1)
                if diag1:
                    finish(i1)
            return run

        kind = 2 * (c0 == i0).astype(jnp.int32) + (c1 == i1).astype(jnp.int32)
        lax.switch(kind, [variant(False, False), variant(False, True),
                          variant(True, False), variant(True, True)])
        return i2, c2

    lax.fori_loop(0, n_items // 2, two_items, (jnp.int32(0), jnp.int32(0)))


def _row_in_ranges(shape, ranges):
    row = lax.broadcasted_iota(jnp.int32, shape, 0)
    hit = None
    for lo, hi in ranges:
        r = (row >= lo) & (row < hi)
        hit = r if hit is None else hit | r
    return hit


def _diff_bias(bias_ref, d):
    sub = ATT_TILE // BIAS_TILE
    rows = [jnp.concatenate([bias_ref[0, jnp.clip(sub * d + qa - ka, 0, B_BIAS_TILES - 1)]
                             for qa in range(sub)], axis=1) for ka in range(sub)]
    return jnp.concatenate(rows, axis=0)


def _tile_cols(i):
    return pl.ds(pl.multiple_of(i * ATT_TILE, ATT_TILE), ATT_TILE)


def _diff_kernel(lam_init, qt_ref, k_ref, vt_ref, bias_ref, lam_ref, g_ref, y_ref,
                 qm_ref, m_ref, acc_ref, s_ref, cmax_ref):
    t = ATT_TILE
    n_tiles = qt_ref.shape[1] // t

    def split_queries(i, carry):
        qf = qt_ref[:, _tile_cols(i)].astype(F32)
        first = lax.broadcasted_iota(jnp.int32, qf.shape, 0) < B_QK_DIM
        qm_ref[0, :, _tile_cols(i)] = jnp.where(first, qf, 0.0).astype(BF16)
        qm_ref[1, :, _tile_cols(i)] = jnp.where(first, 0.0, qf).astype(BF16)
        return carry

    lax.fori_loop(0, n_tiles, split_queries, 0)
    lv = lam_ref[...]
    lam = (jnp.exp(jnp.sum(lv[0:1] * lv[1:2], axis=-1, keepdims=True))
           - jnp.exp(jnp.sum(lv[2:3] * lv[3:4], axis=-1, keepdims=True)) + lam_init)
    _flash_init(m_ref, acc_ref)

    def scores(i, c, slot, idx):
        k = k_ref[_tile_cols(c), :]
        s = _dot(k, qm_ref[idx, :, _tile_cols(i)]) + _diff_bias(bias_ref, i - c)
        _store_scores(s, slot, idx, s_ref, cmax_ref)

    def absorb(c, slot, idx, part, masked, m_new):
        v_ext = _with_ones(vt_ref[:, _tile_cols(c)])
        return _flash_part(slot, idx, part, masked, m_new, v_ext, s_ref)

    def finish(i):
        o = _flash_result(0, B_V_DIM, acc_ref) - lam * _flash_result(1, B_V_DIM, acc_ref)
        y = o * lax.rsqrt(jnp.mean(o * o, axis=0, keepdims=True) + EPS) * g_ref[...] * (1.0 - lam_init)
        y_ref[_tile_cols(i), :] = y.T.astype(BF16)
        _flash_init(m_ref, acc_ref)

    _flash_sweep(n_tiles, scores, absorb, finish, m_ref, acc_ref, s_ref, cmax_ref)


def _diff_attention(qb_t, kb, vb_t, bias_tiles, lam_vecs, g_col, batch, seq, lam_init):
    t = ATT_TILE
    return pl.pallas_call(
        functools.partial(_diff_kernel, lam_init),
        grid=(batch, B_HEADS),
        in_specs=[
            pl.BlockSpec((128, seq), lambda b, h: (h, b)),
            pl.BlockSpec((seq, 128), lambda b, h: (b, h)),
            pl.BlockSpec((128, seq), lambda b, h: (h, b)),
            pl.BlockSpec((1, B_BIAS_TILES, BIAS_TILE, BIAS_TILE), lambda b, h: (h, 0, 0, 0)),
            _const_spec((4, B_QK_DIM)),
            _const_spec((B_V_DIM, 1)),
        ],
        out_specs=pl.BlockSpec((seq, 128), lambda b, h: (b, h)),
        out_shape=jax.ShapeDtypeStruct((batch * seq, B_WIDTH), BF16),
        scratch_shapes=[
            pltpu.VMEM((2, 128, seq), BF16),
            pltpu.VMEM((2, 1, t), F32),
            pltpu.VMEM((2, B_V_DIM + ONES_ROWS, t), F32),
            pltpu.VMEM((2, 2, t, t), F32),
            pltpu.VMEM((2, 2, 1, t), F32),
        ],
        compiler_params=_params(2),
        name="diff_attention",
    )(qb_t, kb, vb_t, bias_tiles, lam_vecs, g_col)


def _mla_kernel(qt_ref, k_ref, vt_ref, y_ref, qm_ref, m_ref, acc_ref, s_ref, cmax_ref):
    t = ATT_TILE
    n_tiles = qt_ref.shape[1] // t

    def split_heads(i, carry):
        qf = qt_ref[:, _tile_cols(i)].astype(F32)
        qm_ref[0, :, _tile_cols(i)] = jnp.where(_row_in_ranges(qf.shape, C_HEAD_A_RANGES), qf, 0.0).astype(BF16)
        qm_ref[1, :, _tile_cols(i)] = jnp.where(_row_in_ranges(qf.shape, C_HEAD_B_RANGES), qf, 0.0).astype(BF16)
        return carry

    lax.fori_loop(0, n_tiles, split_heads, 0)
    _flash_init(m_ref, acc_ref)

    def scores(i, c, slot, idx):
        k = k_ref[_tile_cols(c), :]
        _store_scores(_dot(k, qm_ref[idx, :, _tile_cols(i)]), slot, idx, s_ref, cmax_ref)

    def absorb(c, slot, idx, part, masked, m_new):
        v_t = vt_ref[idx * C_V:(idx + 1) * C_V, _tile_cols(c)]
        return _flash_part(slot, idx, part, masked, m_new, _with_ones(v_t), s_ref)

    def finish(i):
        o = jnp.concatenate([_flash_result(0, C_V, acc_ref), _flash_result(1, C_V, acc_ref)], axis=0)
        y_ref[_tile_cols(i), :] = o.T.astype(BF16)
        _flash_init(m_ref, acc_ref)

    _flash_sweep(n_tiles, scores, absorb, finish, m_ref, acc_ref, s_ref, cmax_ref)


def _mla_attention(qc_t, kc, vc_t, batch, seq):
    t = ATT_TILE
    return pl.pallas_call(
        _mla_kernel,
        grid=(batch, C_PAIRS),
        in_specs=[
            pl.BlockSpec((C_PAIR_LANES, seq), lambda b, j: (j, b)),
            pl.BlockSpec((seq, C_PAIR_LANES), lambda b, j: (b, j)),
            pl.BlockSpec((2 * C_V, seq), lambda b, j: (j, b)),
        ],
        out_specs=pl.BlockSpec((seq, 2 * C_V), lambda b, j: (b, j)),
        out_shape=jax.ShapeDtypeStruct((batch * seq, C_WIDTH), BF16),
        scratch_shapes=[
            pltpu.VMEM((2, C_PAIR_LANES, seq), BF16),
            pltpu.VMEM((2, 1, t), F32),
            pltpu.VMEM((2, C_V + ONES_ROWS, t), F32),
            pltpu.VMEM((2, 2, t, t), F32),
            pltpu.VMEM((2, 2, 1, t), F32),
        ],
        compiler_params=_params(2),
        name="mla_attention",
    )(qc_t, kc, vc_t)


def _sigmoid(z):
    return 1.0 / (1.0 + jnp.exp(-z))


def _merge_kernel(x_ref, g_ref, ya_ref, yb_ref, yc_ref, wg_ref, bg_ref, wa_ref, wb_ref, wc_ref,
                  wo_ref, out_ref):
    x = x_ref[...]
    h = _rms(x, g_ref[...]).astype(BF16)
    merged = jnp.zeros(x.shape, F32)
    for k, (y_ref, w_ref) in enumerate(((ya_ref, wa_ref), (yb_ref, wb_ref), (yc_ref, wc_ref))):
        cols = slice(k * D_MODEL, (k + 1) * D_MODEL)
        gate = _sigmoid(_dot(h, wg_ref[:, cols]) + bg_ref[:, cols])
        merged = merged + gate * _dot(y_ref[...], w_ref[...])
    out_ref[...] = x + _dot(merged.astype(BF16), wo_ref[...])


def _merge_stage(x2, layer, g, ya, yb, yc, wg, bg, wa, wb, wc, wo):
    n = x2.shape[0]
    tm = ROW_TILE
    row = lambda i: (i, 0)
    return pl.pallas_call(
        _merge_kernel,
        grid=(n // tm,),
        in_specs=[
            pl.BlockSpec((tm, D_MODEL), row),
            _layer_spec((1, D_MODEL), layer),
            pl.BlockSpec((tm, A_WIDTH), row),
            pl.BlockSpec((tm, B_WIDTH), row),
            pl.BlockSpec((tm, C_WIDTH), row),
            _layer_spec((D_MODEL, N_BRANCH * D_MODEL), layer),
            _layer_spec((1, N_BRANCH * D_MODEL), layer),
            _layer_spec((A_WIDTH, D_MODEL), layer),
            _layer_spec((B_WIDTH, D_MODEL), layer),
            _layer_spec((C_WIDTH, D_MODEL), layer),
            _layer_spec((D_MODEL, D_MODEL), layer),
        ],
        out_specs=pl.BlockSpec((tm, D_MODEL), row),
        out_shape=jax.ShapeDtypeStruct((n, D_MODEL), F32),
        compiler_params=_params(1),
        name="gated_merge",
    )(x2, g, ya, yb, yc, wg, bg, wa, wb, wc, wo)


def _ffn_kernel(final_norm, x_ref, g_ref, wg_ref, wu_ref, wd_ref, gf_ref, out_ref):
    x = x_ref[...]
    h = _rms(x, g_ref[...]).astype(BF16)
    acc = jnp.zeros(x.shape, F32)
    for c in range(0, FFN_HIDDEN, FFN_CHUNK):
        cols = slice(c, c + FFN_CHUNK)
        gate = _dot(h, wg_ref[:, cols])
        act = gate * _sigmoid(gate) * _dot(h, wu_ref[:, cols])
        acc = acc + _dot(act.astype(BF16), wd_ref[cols, :])
    y = x + acc
    if final_norm:
        y = _rms(y, gf_ref[...])
    out_ref[...] = y


def _ffn_stage(x2, layer, g, wg, wu, wd, gf, final_norm):
    n = x2.shape[0]
    tm = ROW_TILE
    row = lambda i: (i, 0)
    return pl.pallas_call(
        functools.partial(_ffn_kernel, final_norm),
        grid=(n // tm,),
        in_specs=[
            pl.BlockSpec((tm, D_MODEL), row),
            _layer_spec((1, D_MODEL), layer),
            _layer_spec((D_MODEL, FFN_HIDDEN), layer),
            _layer_spec((D_MODEL, FFN_HIDDEN), layer),
            _layer_spec((FFN_HIDDEN, D_MODEL), layer),
            _const_spec((1, D_MODEL)),
        ],
        out_specs=pl.BlockSpec((tm, D_MODEL), row),
        out_shape=jax.ShapeDtypeStruct((n, D_MODEL), F32),
        compiler_params=_params(1),
        name="swiglu_final" if final_norm else "swiglu",
    )(x2, g, wg, wu, wd, gf)


def _gather_cols(w, idx):
    idx = np.asarray(idx)
    zero = idx == w.shape[1]
    cuts = [0] + [j for j in range(1, len(idx))
                  if zero[j] != zero[j - 1] or (not zero[j] and idx[j] != idx[j - 1] + 1)]
    pieces = []
    for lo, hi in zip(cuts, cuts[1:] + [len(idx)]):
        if idx[lo] == w.shape[1]:
            pieces.append(jnp.zeros((w.shape[0], hi - lo), w.dtype))
        else:
            pieces.append(w[:, int(idx[lo]):int(idx[lo]) + hi - lo])
    return jnp.concatenate(pieces, axis=1)


def kernel(x, rel_bias_table, ln_mix_g, w_in, lambda_q1, lambda_k1, lambda_q2, lambda_k2, diff_subln_g, mla_q_norm_g, w_uq, mla_kv_norm_g, w_ukv, w_gate, b_gate, w_br_a, w_br_b, w_br_c, w_o, ln_ffn_g, w_ffn_gate, w_ffn_up, w_ffn_down, final_norm_g):
    batch, seq, _ = x.shape
    depth = w_in.shape[0]
    assert all(seq % (A_BLOCK * dil) == 0 for _, dil in A_PATTERNS)
    assert (seq // A_BLOCK) % A_BLOCKS_PER_STEP == 0
    assert seq % ATT_TILE == 0 and seq % ROW_TILE == 0
    lay = _C_LAYOUT
    cos_t, sin_t = _rope_tables(seq)

    a_tiles = _bias_expand(rel_bias_table, _a_bucket_index(), A_HEADS, 0, 2 * A_BLOCK, "bias_expand_a",
                           gain=LOG2E)
    a_tiles = a_tiles.reshape(A_HEADS, len(A_PATTERNS), 2, 2 * A_BLOCK, A_BLOCK)
    b_tiles = _bias_expand(rel_bias_table, _b_bucket_index(), B_HEADS, A_HEADS, BIAS_TILE,
                           "bias_expand_b", gain=LOG2E).reshape(B_HEADS, B_BIAS_TILES, BIAS_TILE, BIAS_TILE)

    rows = lambda v: v[:, None, :]
    w_gate16, w_o16 = _to_bf16(w_gate), _to_bf16(w_o)
    w_br16 = [_to_bf16(w) for w in (w_br_a, w_br_b, w_br_c)]
    w_ffn16 = [_to_bf16(w) for w in (w_ffn_gate, w_ffn_up, w_ffn_down)]
    g_mix, g_q, g_kv, g_ffn, b_gate3 = (rows(v) for v in (ln_mix_g, mla_q_norm_g, mla_kv_norm_g, ln_ffn_g, b_gate))

    x2 = x.reshape(batch * seq, D_MODEL)
    for l in range(depth):
        qkv_a, kb, qb_t, vb_t, qc_t, kc, vc_t = _input_stage(
            x2, l, g_mix, w_in[l][:, :KR_COL0].astype(BF16), g_q,
            _gather_cols(w_uq[l], lay["q"]).astype(BF16), g_kv,
            _gather_cols(w_ukv[l], lay["k"]).astype(BF16),
            _gather_cols(w_ukv[l], lay["v"]).astype(BF16),
            _gather_cols(w_in[l][:, KR_COL0:], lay["kr"]).astype(BF16),
            cos_t, sin_t, seq)

        ya = _dilated_attention(qkv_a, a_tiles, batch, seq)

        lam_init = 0.8 - 0.6 * math.exp(-0.3 * l)
        lam_vecs = jnp.stack([lambda_q1[l], lambda_k1[l], lambda_q2[l], lambda_k2[l]]).astype(F32)
        yb = _diff_attention(qb_t, kb, vb_t, b_tiles, lam_vecs, diff_subln_g[l][:, None],
                             batch, seq, lam_init)
        yc = _mla_attention(qc_t, kc, vc_t, batch, seq)

        x2 = _merge_stage(x2, l, g_mix, ya, yb, yc, w_gate16, b_gate3, *w_br16, w_o16)
        x2 = _ffn_stage(x2, l, g_ffn, *w_ffn16, final_norm_g[None, :], final_norm=(l == depth - 1))
    return x2.reshape(batch, seq, D_MODEL)
```

```python
import functools
import math

import numpy as np
import jax
import jax.numpy as jnp
from jax import lax
from jax.experimental import pallas as pl
from jax.experimental.pallas import tpu as pltpu

D_MODEL = 1024
HEAD_DIM = 64
A_HEADS = 8
A_PATTERNS = ((2048, 16), (512, 4), (128, 1))
A_WIDTH = A_HEADS * HEAD_DIM
A_BLOCK = 128
A_BLOCKS_PER_STEP = 16
B_HEADS = 4
B_QK_DIM = HEAD_DIM
B_V_DIM = 2 * HEAD_DIM
B_WIDTH = B_HEADS * B_V_DIM
C_HEADS = 8
C_PAIRS = C_HEADS // 2
C_Q_RANK = 768
C_KV_RANK = 256
C_NOPE = 64
C_ROPE = 32
C_V = 64
C_WIDTH = C_HEADS * C_V
C_PAIR_LANES = 256
C_COLS = C_PAIRS * C_PAIR_LANES
ROPE_THETA = 10000.0
REL_BUCKETS = 32
REL_MAX_DIST = 2048
AB_COLS = 3 * A_WIDTH + 4 * B_HEADS * B_QK_DIM + B_WIDTH
QB_COL0 = 3 * A_WIDTH
KB_COL0 = QB_COL0 + 2 * B_HEADS * B_QK_DIM
VB_COL0 = KB_COL0 + 2 * B_HEADS * B_QK_DIM
CQ_COL0 = AB_COLS
CKV_COL0 = CQ_COL0 + C_Q_RANK
KR_COL0 = CKV_COL0 + C_KV_RANK
N_BRANCH = 3
FFN_HIDDEN = 2816
EPS = 1e-6
NEG = -1e30
LOG2E = math.log2(math.e)
A_SCORE_SCALE = HEAD_DIM ** -0.5 * LOG2E
B_SCORE_SCALE = B_QK_DIM ** -0.5 * LOG2E
C_SCORE_SCALE = (C_NOPE + C_ROPE) ** -0.5 * LOG2E
ONES_ROWS = 16

ROW_TILE = 1024
ATT_TILE = 512
KEY_PART = 256
BIAS_TILE = 256
B_BIAS_TILES = 8
FFN_CHUNK = 256
VMEM_LIMIT = 56 * 1024 * 1024

F32 = jnp.float32
BF16 = jnp.bfloat16


def _dot(a, b):
    return jnp.dot(a, b, preferred_element_type=F32)


def _dot_nt(a, b):
    return lax.dot_general(a, b, (((1,), (1,)), ((), ())), preferred_element_type=F32)


def _rms(x, g):
    return x * lax.rsqrt(jnp.mean(x * x, axis=-1, keepdims=True) + EPS) * g


def _swap_halves(t):
    half = t.shape[1] // 2
    return jnp.concatenate([t[:, half:], t[:, :half]], axis=1)


def _const_spec(shape):
    nd = len(shape)
    return pl.BlockSpec(shape, lambda *_: (0,) * nd, pipeline_mode=pl.Buffered(1))


def _layer_spec(shape, layer):
    nd = len(shape)
    return pl.BlockSpec((None,) + tuple(shape), lambda *_: (layer,) + (0,) * nd, pipeline_mode=pl.Buffered(1))


def _cast_kernel(w_ref, o_ref):
    o_ref[...] = w_ref[...].astype(o_ref.dtype)


def _to_bf16(w):
    depth, rows, cols = w.shape
    block_rows = next(r for r in (512, 704, rows) if rows % r == 0)
    spec = pl.BlockSpec((1, block_rows, cols), lambda l, i: (l, i, 0))
    return pl.pallas_call(
        _cast_kernel,
        grid=(depth, rows // block_rows),
        in_specs=[spec],
        out_specs=spec,
        out_shape=jax.ShapeDtypeStruct(w.shape, BF16),
        compiler_params=_params(2),
        name="weights_to_bf16",
    )(w)


def _params(n_axes):
    return pltpu.CompilerParams(dimension_semantics=("parallel",) * n_axes,
                                vmem_limit_bytes=VMEM_LIMIT)


def _t5_bucket_np(dist):
    dist = np.maximum(dist, 0)
    exact = REL_BUCKETS // 2
    ratio = np.maximum(dist, 1).astype(np.float32) / np.float32(exact)
    log_ratio = np.log(ratio).astype(np.float32) / np.float32(math.log(REL_MAX_DIST / exact))
    large = np.minimum(exact + (log_ratio * np.float32(REL_BUCKETS - exact)).astype(np.int32),
                       REL_BUCKETS - 1)
    return np.where(dist < exact, dist, large).astype(np.int32)


MASKED_BUCKET = REL_BUCKETS


def _a_bucket_index():
    kj = np.arange(2 * A_BLOCK)[:, None]
    qi = np.arange(A_BLOCK)[None, :]
    step = qi + A_BLOCK - kj
    in_band = (step >= 0) & (step <= A_BLOCK)
    tiles = []
    for _, dil in A_PATTERNS:
        bucket = _t5_bucket_np(step * dil)
        for first_block in (False, True):
            valid = in_band & (kj >= A_BLOCK) if first_block else in_band
            tiles.append(np.where(valid, bucket, MASKED_BUCKET))
    return np.concatenate(tiles, axis=0).astype(np.int32)


def _b_bucket_index():
    kj = np.arange(BIAS_TILE)[:, None]
    qi = np.arange(BIAS_TILE)[None, :]
    e = np.arange(B_BIAS_TILES)[:, None, None]
    idx = _t5_bucket_np(e * BIAS_TILE + qi - kj)
    assert (idx[-1] == REL_BUCKETS - 1).all()
    return idx.reshape(B_BIAS_TILES * BIAS_TILE, BIAS_TILE)


def _c_layout():
    hq = C_NOPE + C_ROPE
    q_zero = C_HEADS * hq
    kv_zero = C_HEADS * (C_NOPE + C_V)
    half = C_ROPE // 2
    q_idx = np.full((C_PAIRS, C_PAIR_LANES), q_zero, np.int32)
    k_idx = np.full((C_PAIRS, C_PAIR_LANES), kv_zero, np.int32)
    for j in range(C_PAIRS):
        a, b = 2 * j, 2 * j + 1
        q_idx[j, 0:64] = a * hq + np.arange(64)
        q_idx[j, 64:80] = a * hq + C_NOPE + np.arange(half)
        q_idx[j, 80:96] = b * hq + C_NOPE + np.arange(half)
        q_idx[j, 128:192] = b * hq + np.arange(64)
        q_idx[j, 192:208] = a * hq + C_NOPE + half + np.arange(half)
        q_idx[j, 208:224] = b * hq + C_NOPE + half + np.arange(half)
        k_idx[j, 0:64] = a * (C_NOPE + C_V) + np.arange(64)
        k_idx[j, 128:192] = b * (C_NOPE + C_V) + np.arange(64)
    v_idx = (np.arange(C_HEADS)[:, None] * (C_NOPE + C_V) + C_NOPE + np.arange(C_V)[None, :])
    kr_idx = np.full((C_PAIR_LANES,), C_ROPE, np.int32)
    kr_idx[64:80] = np.arange(half)
    kr_idx[80:96] = np.arange(half)
    kr_idx[192:208] = half + np.arange(half)
    kr_idx[208:224] = half + np.arange(half)
    rope_lane = np.full((C_PAIR_LANES,), -1, np.int32)
    for start in (64, 80, 192, 208):
        rope_lane[start:start + half] = np.arange(half)
    sin_sign = np.zeros((C_PAIR_LANES,), np.float32)
    sin_sign[64:96] = -1.0
    sin_sign[192:224] = 1.0
    return dict(q=q_idx.reshape(-1), k=k_idx.reshape(-1), v=v_idx.reshape(-1), kr=kr_idx,
                rope_lane=rope_lane, sin_sign=sin_sign)


_C_LAYOUT = _c_layout()
C_HEAD_A_RANGES = ((0, 80), (192, 208))
C_HEAD_B_RANGES = ((80, 96), (128, 192), (208, 224))


def _rope_tables(seq):
    pos = np.arange(seq, dtype=np.float64)
    inv_freq = ROPE_THETA ** (-np.arange(0, C_ROPE, 2, dtype=np.float64) / C_ROPE)
    ang = pos[:, None] * inv_freq[None, :]
    cos, sin = np.cos(ang), np.sin(ang)
    lane = _C_LAYOUT["rope_lane"]
    gather = np.maximum(lane, 0)
    cos_t = np.where((lane >= 0)[None, :], cos[:, gather], 1.0)
    sin_t = sin[:, gather] * _C_LAYOUT["sin_sign"][None, :]
    return jnp.asarray(cos_t, F32), jnp.asarray(sin_t, F32)


def _bias_expand_kernel(n_heads, table_ref, idx_ref, out_ref):
    rows, cols = idx_ref.shape
    for c in range(0, cols, 128):
        idx = idx_ref[:, c:c + 128]
        for h in range(n_heads):
            head_table = jnp.broadcast_to(table_ref[h:h + 1, :], (rows, 128))
            out_ref[h, :, c:c + 128] = jnp.take_along_axis(head_table, idx, axis=1)


def _bias_expand(table, idx, n_heads, head0, block_rows, name, gain=1.0):
    rows, cols = idx.shape
    lanes = jnp.full((n_heads, 128), NEG, F32)
    lanes = lanes.at[:, :REL_BUCKETS].set(table[:, head0:head0 + n_heads].T * gain)
    return pl.pallas_call(
        functools.partial(_bias_expand_kernel, n_heads),
        grid=(rows // block_rows,),
        in_specs=[
            _const_spec((n_heads, 128)),
            pl.BlockSpec((block_rows, cols), lambda i: (i, 0)),
        ],
        out_specs=pl.BlockSpec((n_heads, block_rows, cols), lambda i: (0, i, 0)),
        out_shape=jax.ShapeDtypeStruct((n_heads, rows, cols), F32),
        compiler_params=_params(1),
        name=name,
    )(lanes, jnp.asarray(idx))


def _input_kernel(x_ref, g_ref, win_ref, gq_ref, wuq_ref, gkv_ref, wuk_ref,
                  wuv_ref, wkr_ref, cos_ref, sin_ref,
                  qkva_ref, kb_ref, qbt_ref, vbt_ref, qct_ref, kc_ref, vct_ref):
    h = _rms(x_ref[...], g_ref[...]).astype(BF16)
    cos = cos_ref[...]
    sin = sin_ref[...]

    def project(c):
        t = _dot(h, win_ref[:, c:c + 512])
        if c < QB_COL0:
            qkva_ref[:, c:c + 512] = (t * A_SCORE_SCALE if c == 0 else t).astype(BF16)
        elif c == QB_COL0:
            qbt_ref[...] = (t * B_SCORE_SCALE).T.astype(BF16)
        elif c == KB_COL0:
            kb_ref[...] = t.astype(BF16)
        else:
            vbt_ref[...] = t.T.astype(BF16)

    chunks = iter(range(0, AB_COLS, 512))
    cq = _dot(h, win_ref[:, CQ_COL0:CKV_COL0])
    ckv = _dot(h, win_ref[:, CKV_COL0:KR_COL0])
    kr = _dot(h, wkr_ref[...])
    project(next(chunks))
    cqn = _rms(cq, gq_ref[...]).astype(BF16)
    project(next(chunks))
    ckvn = _rms(ckv, gkv_ref[...]).astype(BF16)
    kr_rot = kr * cos + _swap_halves(kr) * sin
    for j in range(C_PAIRS):
        cols = slice(j * C_PAIR_LANES, (j + 1) * C_PAIR_LANES)
        t = _dot(cqn, wuq_ref[:, cols])
        qct_ref[cols, :] = ((t * cos + _swap_halves(t) * sin) * C_SCORE_SCALE).T.astype(BF16)
        project(next(chunks))
    for j in range(C_PAIRS):
        cols = slice(j * C_PAIR_LANES, (j + 1) * C_PAIR_LANES)
        kc_ref[:, cols] = (_dot(ckvn, wuk_ref[:, cols]) + kr_rot).astype(BF16)
    vct_ref[...] = _dot(ckvn, wuv_ref[...]).T.astype(BF16)
    assert next(chunks, None) is None


def _input_stage(x2, layer, g, w_in, gq, wuq, gkv, wuk, wuv, wkr, cos_t, sin_t, seq):
    n = x2.shape[0]
    tm = ROW_TILE
    pos_blocks = seq // tm
    row = lambda i: (i, 0)
    col = lambda i: (0, i)
    return pl.pallas_call(
        _input_kernel,
        grid=(n // tm,),
        in_specs=[
            pl.BlockSpec((tm, D_MODEL), row),
            _layer_spec((1, D_MODEL), layer),
            _const_spec((D_MODEL, KR_COL0)),
            _layer_spec((1, C_Q_RANK), layer),
            _const_spec((C_Q_RANK, C_COLS)),
            _layer_spec((1, C_KV_RANK), layer),
            _const_spec((C_KV_RANK, C_COLS)),
            _const_spec((C_KV_RANK, C_WIDTH)),
            _const_spec((D_MODEL, C_PAIR_LANES)),
            pl.BlockSpec((tm, C_PAIR_LANES), lambda i: (i % pos_blocks, 0)),
            pl.BlockSpec((tm, C_PAIR_LANES), lambda i: (i % pos_blocks, 0)),
        ],
        out_specs=[
            pl.BlockSpec((tm, QB_COL0), row),
            pl.BlockSpec((tm, VB_COL0 - KB_COL0), row),
            pl.BlockSpec((B_WIDTH, tm), col),
            pl.BlockSpec((B_WIDTH, tm), col),
            pl.BlockSpec((C_COLS, tm), col),
            pl.BlockSpec((tm, C_COLS), row),
            pl.BlockSpec((C_WIDTH, tm), col),
        ],
        out_shape=[
            jax.ShapeDtypeStruct((n, QB_COL0), BF16),
            jax.ShapeDtypeStruct((n, VB_COL0 - KB_COL0), BF16),
            jax.ShapeDtypeStruct((B_WIDTH, n), BF16),
            jax.ShapeDtypeStruct((B_WIDTH, n), BF16),
            jax.ShapeDtypeStruct((C_COLS, n), BF16),
            jax.ShapeDtypeStruct((n, C_COLS), BF16),
            jax.ShapeDtypeStruct((C_WIDTH, n), BF16),
        ],
        compiler_params=_params(1),
        name="input_stage",
    )(x2, g, w_in, gq, wuq, gkv, wuk, wuv, wkr, cos_t, sin_t)


def _dot_tn(a, b):
    return lax.dot_general(a, b, (((0,), (0,)), ((), ())), preferred_element_type=F32)


def _dilated_kernel(q_ref, k_ref, v_ref, bias_ref, y_ref, qf_ref, kf_ref, vf_ref, m_ref, w_ref, a_ref):
    seq = q_ref.shape[0]
    chunk = 512

    def widen(c, carry):
        rows = pl.ds(pl.multiple_of(c * chunk, chunk), chunk)
        qf_ref[rows, :] = q_ref[rows, :].astype(F32)
        kf_ref[rows, :] = k_ref[rows, :].astype(F32)
        vf_ref[rows, :] = v_ref[rows, :].astype(F32)
        return carry

    lax.fori_loop(0, seq // chunk, widen, 0)

    tile = (A_BLOCK, A_BLOCK)
    low_lane = lax.broadcasted_iota(jnp.int32, tile, 1) < HEAD_DIM
    low_row = lax.broadcasted_iota(jnp.int32, tile, 0) < HEAD_DIM
    last = len(A_PATTERNS) - 1
    for g, (_, dil) in enumerate(A_PATTERNS):
        nb = seq // dil // A_BLOCK
        span = dil * A_BLOCK

        def attend(blk, g=g, dil=dil, nb=nb, span=span):
            r = blk // nb
            n = blk % nb
            base = r + n * span
            prev_base = jnp.maximum(base - span, r)
            rows = pl.ds(base, A_BLOCK, stride=dil)
            prev_rows = pl.ds(prev_base, A_BLOCK, stride=dil)
            q = qf_ref[rows, :]
            k2 = jnp.concatenate([kf_ref[prev_rows, :], kf_ref[rows, :]], axis=0).astype(BF16)
            v2 = jnp.concatenate([vf_ref[prev_rows, :], vf_ref[rows, :]], axis=0).astype(BF16)
            first_block = jnp.where(n == 0, 1, 0)
            outs, lses = [], []
            for hh in range(2):
                qm = jnp.where(low_lane if hh == 0 else ~low_lane, q, 0.0).astype(BF16)
                s = _dot_nt(k2, qm) + bias_ref[hh, g, first_block]
                m = jnp.max(s, axis=0, keepdims=True)
                p = jnp.exp2(s - m)
                l = jnp.sum(p, axis=0, keepdims=True)
                outs.append(_dot_tn(v2, p.astype(BF16)) * (1.0 / l))
                lses.append(jnp.broadcast_to(m + jnp.log2(l), tile))
            o_blk = jnp.where(low_row, outs[0], outs[1]).T
            lse_blk = jnp.where(low_row, lses[0], lses[1]).T
            return rows, o_blk, lse_blk

        def merge(rows, o_blk, lse_blk, g=g):
            if g == 0:
                m_ref[rows, :] = lse_blk
                a_ref[rows, :] = o_blk
                return
            m_old = m_ref[rows, :]
            m_new = jnp.maximum(m_old, lse_blk)
            c_old = jnp.exp2(m_old - m_new)
            c_blk = jnp.exp2(lse_blk - m_new)
            w_new = c_blk + (c_old if g == 1 else c_old * w_ref[rows, :])
            a_new = c_old * a_ref[rows, :] + c_blk * o_blk
            if g < last:
                m_ref[rows, :] = m_new
                w_ref[rows, :] = w_new
                a_ref[rows, :] = a_new
            else:
                a_ref[rows, :] = a_new / w_new

        def step(it, carry, attend=attend, merge=merge):
            done = [attend(it * A_BLOCKS_PER_STEP + u) for u in range(A_BLOCKS_PER_STEP)]
            for args in done:
                merge(*args)
            return carry

        lax.fori_loop(0, dil * nb // A_BLOCKS_PER_STEP, step, 0)

    def narrow(c, carry):
        rows = pl.ds(pl.multiple_of(c * chunk, chunk), chunk)
        y_ref[rows, :] = a_ref[rows, :].astype(BF16)
        return carry

    lax.fori_loop(0, seq // chunk, narrow, 0)


def _dilated_attention(qkv, bias_tiles, batch, seq):
    pairs = A_HEADS // 2
    col = lambda off: (lambda b, p: (b, off * pairs + p))
    state = pltpu.VMEM((seq, 128), F32)
    return pl.pallas_call(
        _dilated_kernel,
        grid=(batch, pairs),
        in_specs=[
            pl.BlockSpec((seq, 128), col(0)),
            pl.BlockSpec((seq, 128), col(1)),
            pl.BlockSpec((seq, 128), col(2)),
            pl.BlockSpec((2, len(A_PATTERNS), 2, 2 * A_BLOCK, A_BLOCK), lambda b, p: (p, 0, 0, 0, 0)),
        ],
        out_specs=pl.BlockSpec((seq, 128), lambda b, p: (b, p)),
        out_shape=jax.ShapeDtypeStruct((batch * seq, A_WIDTH), BF16),
        scratch_shapes=[state] * 6,
        compiler_params=_params(2),
        name="dilated_attention",
    )(qkv, qkv, qkv, bias_tiles)


def _flash_init(m_ref, acc_ref):
    m_ref[...] = jnp.full(m_ref.shape, NEG, F32)
    acc_ref[...] = jnp.zeros(acc_ref.shape, F32)


def _with_ones(v_t):
    return jnp.concatenate([v_t, jnp.ones((ONES_ROWS, v_t.shape[1]), v_t.dtype)], axis=0)


def _key_rows(part):
    return slice(part * KEY_PART, (part + 1) * KEY_PART)


def _store_scores(s, slot, idx, s_ref, cmax_ref):
    s_ref[slot, idx] = s
    cmax_ref[slot, idx] = jnp.max(s, axis=0, keepdims=True)


def _masked_part(slot, idx, part, masked, s_ref):
    s = s_ref[slot, idx, _key_rows(part), :]
    if masked:
        key = lax.broadcasted_iota(jnp.int32, s.shape, 0) + part * KEY_PART
        s = jnp.where(key <= lax.broadcasted_iota(jnp.int32, s.shape, 1), s, NEG)
    return s


def _flash_begin(slot, idx, masked, m_ref, s_ref, cmax_ref):
    if masked:
        col_max = functools.reduce(jnp.maximum, [
            jnp.max(_masked_part(slot, idx, part, True, s_ref), axis=0, keepdims=True)
            for part in range(ATT_TILE // KEY_PART)])
    else:
        col_max = cmax_ref[slot, idx]
    m_prev = m_ref[idx]
    m_new = jnp.maximum(m_prev, col_max)
    m_ref[idx] = m_new
    return m_new, jnp.exp2(m_prev - m_new)


def _flash_part(slot, idx, part, masked, m_new, v_ext, s_ref):
    p = jnp.exp2(_masked_part(slot, idx, part, masked, s_ref) - m_new)
    return _dot(v_ext[:, _key_rows(part)], p.astype(BF16))


def _flash_result(idx, dv, acc_ref):
    acc = acc_ref[idx]
    return acc[:dv] * (1.0 / acc[dv:dv + 1])


def _flash_sweep(n_tiles, scores, absorb, finish, m_ref, acc_ref, s_ref, cmax_ref):
    n_items = n_tiles * (n_tiles + 1) // 2
    assert n_items % 2 == 0

    def following(i, c):
        end = c == i
        return jnp.where(end, i + 1, i), jnp.where(end, 0, c + 1)

    parts = ATT_TILE // KEY_PART

    def item(nxt, cur, diag, slot):
        for idx in range(2):
            scores(*nxt, 1 - slot, idx)
            m_new, alpha = _flash_begin(slot, idx, diag, m_ref, s_ref, cmax_ref)
            pv = None
            for part in range(parts):
                piece = absorb(cur, slot, idx, part, diag, m_new)
                pv = piece if pv is None else pv + piece
            acc_ref[idx] = alpha * acc_ref[idx] + pv

    for idx in range(2):
        scores(0, 0, 0, idx)

    def two_items(u, carry):
        i0, c0 = carry
        i1, c1 = following(i0, c0)
        i2, c2 = following(i1, c1)
        i2s = jnp.minimum(i2, n_tiles - 1)

        def variant(diag0, diag1):
            def run():
                item((i1, c1), c0, diag0, 0)
                if diag0:
                    finish(i0)
                item((i2s, c2), c1, diag1, 1)
                if diag1:
                    finish(i1)
            return run

        kind = 2 * (c0 == i0).astype(jnp.int32) + (c1 == i1).astype(jnp.int32)
        lax.switch(kind, [variant(False, False), variant(False, True),
                          variant(True, False), variant(True, True)])
        return i2, c2

    lax.fori_loop(0, n_items // 2, two_items, (jnp.int32(0), jnp.int32(0)))


def _row_in_ranges(shape, ranges):
    row = lax.broadcasted_iota(jnp.int32, shape, 0)
    hit = None
    for lo, hi in ranges:
        r = (row >= lo) & (row < hi)
        hit = r if hit is None else hit | r
    return hit


def _diff_bias(bias_ref, d):
    sub = ATT_TILE // BIAS_TILE
    rows = [jnp.concatenate([bias_ref[0, jnp.clip(sub * d + qa - ka, 0, B_BIAS_TILES - 1)]
                             for qa in range(sub)], axis=1) for ka in range(sub)]
    return jnp.concatenate(rows, axis=0)


def _tile_cols(i):
    return pl.ds(pl.multiple_of(i * ATT_TILE, ATT_TILE), ATT_TILE)


def _diff_kernel(lam_init, qt_ref, k_ref, vt_ref, bias_ref, lam_ref, g_ref, y_ref,
                 qm_ref, m_ref, acc_ref, s_ref, cmax_ref):
    t = ATT_TILE
    n_tiles = qt_ref.shape[1] // t

    def split_queries(i, carry):
        qf = qt_ref[:, _tile_cols(i)].astype(F32)
        first = lax.broadcasted_iota(jnp.int32, qf.shape, 0) < B_QK_DIM
        qm_ref[0, :, _tile_cols(i)] = jnp.where(first, qf, 0.0).astype(BF16)
        qm_ref[1, :, _tile_cols(i)] = jnp.where(first, 0.0, qf).astype(BF16)
        return carry

    lax.fori_loop(0, n_tiles, split_queries, 0)
    lv = lam_ref[...]
    lam = (jnp.exp(jnp.sum(lv[0:1] * lv[1:2], axis=-1, keepdims=True))
           - jnp.exp(jnp.sum(lv[2:3] * lv[3:4], axis=-1, keepdims=True)) + lam_init)
    _flash_init(m_ref, acc_ref)

    def scores(i, c, slot, idx):
        k = k_ref[_tile_cols(c), :]
        s = _dot(k, qm_ref[idx, :, _tile_cols(i)]) + _diff_bias(bias_ref, i - c)
        _store_scores(s, slot, idx, s_ref, cmax_ref)

    def absorb(c, slot, idx, part, masked, m_new):
        v_ext = _with_ones(vt_ref[:, _tile_cols(c)])
        return _flash_part(slot, idx, part, masked, m_new, v_ext, s_ref)

    def finish(i):
        o = _flash_result(0, B_V_DIM, acc_ref) - lam * _flash_result(1, B_V_DIM, acc_ref)
        y = o * lax.rsqrt(jnp.mean(o * o, axis=0, keepdims=True) + EPS) * g_ref[...] * (1.0 - lam_init)
        y_ref[_tile_cols(i), :] = y.T.astype(BF16)
        _flash_init(m_ref, acc_ref)

    _flash_sweep(n_tiles, scores, absorb, finish, m_ref, acc_ref, s_ref, cmax_ref)


def _diff_attention(qb_t, kb, vb_t, bias_tiles, lam_vecs, g_col, batch, seq, lam_init):
    t = ATT_TILE
    return pl.pallas_call(
        functools.partial(_diff_kernel, lam_init),
        grid=(batch, B_HEADS),
        in_specs=[
            pl.BlockSpec((128, seq), lambda b, h: (h, b)),
            pl.BlockSpec((seq, 128), lambda b, h: (b, h)),
            pl.BlockSpec((128, seq), lambda b, h: (h, b)),
            pl.BlockSpec((1, B_BIAS_TILES, BIAS_TILE, BIAS_TILE), lambda b, h: (h, 0, 0, 0)),
            _const_spec((4, B_QK_DIM)),
            _const_spec((B_V_DIM, 1)),
        ],
        out_specs=pl.BlockSpec((seq, 128), lambda b, h: (b, h)),
        out_shape=jax.ShapeDtypeStruct((batch * seq, B_WIDTH), BF16),
        scratch_shapes=[
            pltpu.VMEM((2, 128, seq), BF16),
            pltpu.VMEM((2, 1, t), F32),
            pltpu.VMEM((2, B_V_DIM + ONES_ROWS, t), F32),
            pltpu.VMEM((2, 2, t, t), F32),
            pltpu.VMEM((2, 2, 1, t), F32),
        ],
        compiler_params=_params(2),
        name="diff_attention",
    )(qb_t, kb, vb_t, bias_tiles, lam_vecs, g_col)


def _mla_kernel(qt_ref, k_ref, vt_ref, y_ref, qm_ref, m_ref, acc_ref, s_ref, cmax_ref):
    t = ATT_TILE
    n_tiles = qt_ref.shape[1] // t

    def split_heads(i, carry):
        qf = qt_ref[:, _tile_cols(i)].astype(F32)
        qm_ref[0, :, _tile_cols(i)] = jnp.where(_row_in_ranges(qf.shape, C_HEAD_A_RANGES), qf, 0.0).astype(BF16)
        qm_ref[1, :, _tile_cols(i)] = jnp.where(_row_in_ranges(qf.shape, C_HEAD_B_RANGES), qf, 0.0).astype(BF16)
        return carry

    lax.fori_loop(0, n_tiles, split_heads, 0)
    _flash_init(m_ref, acc_ref)

    def scores(i, c, slot, idx):
        k = k_ref[_tile_cols(c), :]
        _store_scores(_dot(k, qm_ref[idx, :, _tile_cols(i)]), slot, idx, s_ref, cmax_ref)

    def absorb(c, slot, idx, part, masked, m_new):
        v_t = vt_ref[idx * C_V:(idx + 1) * C_V, _tile_cols(c)]
        return _flash_part(slot, idx, part, masked, m_new, _with_ones(v_t), s_ref)

    def finish(i):
        o = jnp.concatenate([_flash_result(0, C_V, acc_ref), _flash_result(1, C_V, acc_ref)], axis=0)
        y_ref[_tile_cols(i), :] = o.T.astype(BF16)
        _flash_init(m_ref, acc_ref)

    _flash_sweep(n_tiles, scores, absorb, finish, m_ref, acc_ref, s_ref, cmax_ref)


def _mla_attention(qc_t, kc, vc_t, batch, seq):
    t = ATT_TILE
    return pl.pallas_call(
        _mla_kernel,
        grid=(batch, C_PAIRS),
        in_specs=[
            pl.BlockSpec((C_PAIR_LANES, seq), lambda b, j: (j, b)),
            pl.BlockSpec((seq, C_PAIR_LANES), lambda b, j: (b, j)),
            pl.BlockSpec((2 * C_V, seq), lambda b, j: (j, b)),
        ],
        out_specs=pl.BlockSpec((seq, 2 * C_V), lambda b, j: (b, j)),
        out_shape=jax.ShapeDtypeStruct((batch * seq, C_WIDTH), BF16),
        scratch_shapes=[
            pltpu.VMEM((2, C_PAIR_LANES, seq), BF16),
            pltpu.VMEM((2, 1, t), F32),
            pltpu.VMEM((2, C_V + ONES_ROWS, t), F32),
            pltpu.VMEM((2, 2, t, t), F32),
            pltpu.VMEM((2, 2, 1, t), F32),
        ],
        compiler_params=_params(2),
        name="mla_attention",
    )(qc_t, kc, vc_t)


def _sigmoid(z):
    return 1.0 / (1.0 + jnp.exp(-z))


def _merge_kernel(x_ref, g_ref, ya_ref, yb_ref, yc_ref, wg_ref, bg_ref, wa_ref, wb_ref, wc_ref,
                  wo_ref, out_ref):
    x = x_ref[...]
    h = _rms(x, g_ref[...]).astype(BF16)
    merged = jnp.zeros(x.shape, F32)
    for k, (y_ref, w_ref) in enumerate(((ya_ref, wa_ref), (yb_ref, wb_ref), (yc_ref, wc_ref))):
        cols = slice(k * D_MODEL, (k + 1) * D_MODEL)
        gate = _sigmoid(_dot(h, wg_ref[:, cols]) + bg_ref[:, cols])
        merged = merged + gate * _dot(y_ref[...], w_ref[...])
    out_ref[...] = x + _dot(merged.astype(BF16), wo_ref[...])


def _merge_stage(x2, layer, g, ya, yb, yc, wg, bg, wa, wb, wc, wo):
    n = x2.shape[0]
    tm = ROW_TILE
    row = lambda i: (i, 0)
    return pl.pallas_call(
        _merge_kernel,
        grid=(n // tm,),
        in_specs=[
            pl.BlockSpec((tm, D_MODEL), row),
            _layer_spec((1, D_MODEL), layer),
            pl.BlockSpec((tm, A_WIDTH), row),
            pl.BlockSpec((tm, B_WIDTH), row),
            pl.BlockSpec((tm, C_WIDTH), row),
            _layer_spec((D_MODEL, N_BRANCH * D_MODEL), layer),
            _layer_spec((1, N_BRANCH * D_MODEL), layer),
            _layer_spec((A_WIDTH, D_MODEL), layer),
            _layer_spec((B_WIDTH, D_MODEL), layer),
            _layer_spec((C_WIDTH, D_MODEL), layer),
            _layer_spec((D_MODEL, D_MODEL), layer),
        ],
        out_specs=pl.BlockSpec((tm, D_MODEL), row),
        out_shape=jax.ShapeDtypeStruct((n, D_MODEL), F32),
        compiler_params=_params(1),
        name="gated_merge",
    )(x2, g, ya, yb, yc, wg, bg, wa, wb, wc, wo)


def _ffn_kernel(final_norm, x_ref, g_ref, wg_ref, wu_ref, wd_ref, gf_ref, out_ref):
    x = x_ref[...]
    h = _rms(x, g_ref[...]).astype(BF16)
    acc = jnp.zeros(x.shape, F32)
    for c in range(0, FFN_HIDDEN, FFN_CHUNK):
        cols = slice(c, c + FFN_CHUNK)
        gate = _dot(h, wg_ref[:, cols])
        act = gate * _sigmoid(gate) * _dot(h, wu_ref[:, cols])
        acc = acc + _dot(act.astype(BF16), wd_ref[cols, :])
    y = x + acc
    if final_norm:
        y = _rms(y, gf_ref[...])
    out_ref[...] = y


def _ffn_stage(x2, layer, g, wg, wu, wd, gf, final_norm):
    n = x2.shape[0]
    tm = ROW_TILE
    row = lambda i: (i, 0)
    return pl.pallas_call(
        functools.partial(_ffn_kernel, final_norm),
        grid=(n // tm,),
        in_specs=[
            pl.BlockSpec((tm, D_MODEL), row),
            _layer_spec((1, D_MODEL), layer),
            _layer_spec((D_MODEL, FFN_HIDDEN), layer),
            _layer_spec((D_MODEL, FFN_HIDDEN), layer),
            _layer_spec((FFN_HIDDEN, D_MODEL), layer),
            _const_spec((1, D_MODEL)),
        ],
        out_specs=pl.BlockSpec((tm, D_MODEL), row),
        out_shape=jax.ShapeDtypeStruct((n, D_MODEL), F32),
        compiler_params=_params(1),
        name="swiglu_final" if final_norm else "swiglu",
    )(x2, g, wg, wu, wd, gf)


def _gather_cols(w, idx):
    idx = np.asarray(idx)
    zero = idx == w.shape[1]
    cuts = [0] + [j for j in range(1, len(idx))
                  if zero[j] != zero[j - 1] or (not zero[j] and idx[j] != idx[j - 1] + 1)]
    pieces = []
    for lo, hi in zip(cuts, cuts[1:] + [len(idx)]):
        if idx[lo] == w.shape[1]:
            pieces.append(jnp.zeros((w.shape[0], hi - lo), w.dtype))
        else:
            pieces.append(w[:, int(idx[lo]):int(idx[lo]) + hi - lo])
    return jnp.concatenate(pieces, axis=1)


def kernel(x, rel_bias_table, ln_mix_g, w_in, lambda_q1, lambda_k1, lambda_q2, lambda_k2, diff_subln_g, mla_q_norm_g, w_uq, mla_kv_norm_g, w_ukv, w_gate, b_gate, w_br_a, w_br_b, w_br_c, w_o, ln_ffn_g, w_ffn_gate, w_ffn_up, w_ffn_down, final_norm_g):
    batch, seq, _ = x.shape
    depth = w_in.shape[0]
    assert all(seq % (A_BLOCK * dil) == 0 for _, dil in A_PATTERNS)
    assert (seq // A_BLOCK) % A_BLOCKS_PER_STEP == 0
    assert seq % ATT_TILE == 0 and seq % ROW_TILE == 0
    lay = _C_LAYOUT
    cos_t, sin_t = _rope_tables(seq)

    a_tiles = _bias_expand(rel_bias_table, _a_bucket_index(), A_HEADS, 0, 2 * A_BLOCK, "bias_expand_a",
                           gain=LOG2E)
    a_tiles = a_tiles.reshape(A_HEADS, len(A_PATTERNS), 2, 2 * A_BLOCK, A_BLOCK)
    b_tiles = _bias_expand(rel_bias_table, _b_bucket_index(), B_HEADS, A_HEADS, BIAS_TILE,
                           "bias_expand_b", gain=LOG2E).reshape(B_HEADS, B_BIAS_TILES, BIAS_TILE, BIAS_TILE)

    rows = lambda v: v[:, None, :]
    w_gate16, w_o16 = _to_bf16(w_gate), _to_bf16(w_o)
    w_br16 = [_to_bf16(w) for w in (w_br_a, w_br_b, w_br_c)]
    w_ffn16 = [_to_bf16(w) for w in (w_ffn_gate, w_ffn_up, w_ffn_down)]
    g_mix, g_q, g_kv, g_ffn, b_gate3 = (rows(v) for v in (ln_mix_g, mla_q_norm_g, mla_kv_norm_g, ln_ffn_g, b_gate))

    x2 = x.reshape(batch * seq, D_MODEL)
    for l in range(depth):
        qkv_a, kb, qb_t, vb_t, qc_t, kc, vc_t = _input_stage(
            x2, l, g_mix, w_in[l][:, :KR_COL0].astype(BF16), g_q,
            _gather_cols(w_uq[l], lay["q"]).astype(BF16), g_kv,
            _gather_cols(w_ukv[l], lay["k"]).astype(BF16),
            _gather_cols(w_ukv[l], lay["v"]).astype(BF16),
            _gather_cols(w_in[l][:, KR_COL0:], lay["kr"]).astype(BF16),
            cos_t, sin_t, seq)

        ya = _dilated_attention(qkv_a, a_tiles, batch, seq)

        lam_init = 0.8 - 0.6 * math.exp(-0.3 * l)
        lam_vecs = jnp.stack([lambda_q1[l], lambda_k1[l], lambda_q2[l], lambda_k2[l]]).astype(F32)
        yb = _diff_attention(qb_t, kb, vb_t, b_tiles, lam_vecs, diff_subln_g[l][:, None],
                             batch, seq, lam_init)
        yc = _mla_attention(qc_t, kc, vc_t, batch, seq)

        x2 = _merge_stage(x2, l, g_mix, ya, yb, yc, w_gate16, b_gate3, *w_br16, w_o16)
        x2 = _ffn_stage(x2, l, g_ffn, *w_ffn16, final_norm_g[None, :], final_norm=(l == depth - 1))
    return x2.reshape(batch, seq, D_MODEL)
```

```python
import functools
import math

import numpy as np
import jax
import jax.numpy as jnp
from jax import lax
from jax.experimental import pallas as pl
from jax.experimental.pallas import tpu as pltpu

LANES = 128
D_MODEL = 1024
HEAD_DIM = 64
A_HEADS = 8
A_PATTERNS = ((2048, 16), (512, 4), (128, 1))
A_WIDTH = A_HEADS * HEAD_DIM
A_BLOCK = 128
A_BLOCKS_PER_STEP = 16
B_HEADS = 4
B_QK_DIM = HEAD_DIM
B_V_DIM = 2 * HEAD_DIM
B_WIDTH = B_HEADS * B_V_DIM
C_HEADS = 8
C_PAIRS = C_HEADS // 2
C_Q_RANK = 768
C_KV_RANK = 256
C_NOPE = 64
C_ROPE = 32
C_V = 64
C_WIDTH = C_HEADS * C_V
C_PAIR_LANES = 256
C_COLS = C_PAIRS * C_PAIR_LANES
ROPE_THETA = 10000.0
REL_BUCKETS = 32
REL_MAX_DIST = 2048
AB_COLS = 3 * A_WIDTH + 4 * B_HEADS * B_QK_DIM + B_WIDTH
QB_COL0 = 3 * A_WIDTH
KB_COL0 = QB_COL0 + 2 * B_HEADS * B_QK_DIM
VB_COL0 = KB_COL0 + 2 * B_HEADS * B_QK_DIM
CQ_COL0 = AB_COLS
CKV_COL0 = CQ_COL0 + C_Q_RANK
KR_COL0 = CKV_COL0 + C_KV_RANK
N_BRANCH = 3
FFN_HIDDEN = 2816
EPS = 1e-6
NEG = -1e30
LOG2E = math.log2(math.e)
A_SCORE_SCALE = HEAD_DIM ** -0.5 * LOG2E
B_SCORE_SCALE = B_QK_DIM ** -0.5 * LOG2E
C_SCORE_SCALE = (C_NOPE + C_ROPE) ** -0.5 * LOG2E
ONES_ROWS = 16

ROW_TILE = 1024
ATT_TILE = 512
KEY_PART = 256
BIAS_TILE = 256
B_BIAS_TILES = 8
FFN_CHUNK = 256
PROJ_CHUNK = 512
CAST_MAX_ROWS = 704
VMEM_LIMIT = 56 * 1024 * 1024

F32 = jnp.float32
BF16 = jnp.bfloat16


def _dot(a, b):
    return jnp.dot(a, b, preferred_element_type=F32)


def _dot_nt(a, b):
    return lax.dot_general(a, b, (((1,), (1,)), ((), ())), preferred_element_type=F32)


def _rms(x, g):
    return x * lax.rsqrt(jnp.mean(x * x, axis=-1, keepdims=True) + EPS) * g


def _swap_halves(t):
    half = t.shape[1] // 2
    return jnp.concatenate([t[:, half:], t[:, :half]], axis=1)


def _const_spec(shape):
    nd = len(shape)
    return pl.BlockSpec(shape, lambda *_: (0,) * nd, pipeline_mode=pl.Buffered(1))


def _layer_spec(shape, layer):
    nd = len(shape)
    return pl.BlockSpec((None,) + tuple(shape), lambda *_: (layer,) + (0,) * nd, pipeline_mode=pl.Buffered(1))


def _cast_kernel(w_ref, o_ref):
    o_ref[...] = w_ref[...].astype(o_ref.dtype)


def _to_bf16(w):
    depth, rows, cols = w.shape
    block_rows = max(r for r in range(16, CAST_MAX_ROWS + 1, 16) if rows % r == 0)
    spec = pl.BlockSpec((1, block_rows, cols), lambda l, i: (l, i, 0))
    return pl.pallas_call(
        _cast_kernel,
        grid=(depth, rows // block_rows),
        in_specs=[spec],
        out_specs=spec,
        out_shape=jax.ShapeDtypeStruct(w.shape, BF16),
        compiler_params=_params(2),
        name="weights_to_bf16",
    )(w)


def _params(n_axes):
    return pltpu.CompilerParams(dimension_semantics=("parallel",) * n_axes,
                                vmem_limit_bytes=VMEM_LIMIT)


def _t5_bucket_np(dist):
    dist = np.maximum(dist, 0)
    exact = REL_BUCKETS // 2
    ratio = np.maximum(dist, 1).astype(np.float32) / np.float32(exact)
    log_ratio = np.log(ratio).astype(np.float32) / np.float32(math.log(REL_MAX_DIST / exact))
    large = np.minimum(exact + (log_ratio * np.float32(REL_BUCKETS - exact)).astype(np.int32),
                       REL_BUCKETS - 1)
    return np.where(dist < exact, dist, large).astype(np.int32)


MASKED_BUCKET = REL_BUCKETS


def _a_bucket_index():
    kj = np.arange(2 * A_BLOCK)[:, None]
    qi = np.arange(A_BLOCK)[None, :]
    step = qi + A_BLOCK - kj
    in_band = (step >= 0) & (step <= A_BLOCK)
    tiles = []
    for _, dil in A_PATTERNS:
        bucket = _t5_bucket_np(step * dil)
        for first_block in (False, True):
            valid = in_band & (kj >= A_BLOCK) if first_block else in_band
            tiles.append(np.where(valid, bucket, MASKED_BUCKET))
    return np.concatenate(tiles, axis=0).astype(np.int32)


def _b_bucket_index():
    kj = np.arange(BIAS_TILE)[:, None]
    qi = np.arange(BIAS_TILE)[None, :]
    e = np.arange(B_BIAS_TILES)[:, None, None]
    idx = _t5_bucket_np(e * BIAS_TILE + qi - kj)
    assert (idx[-1] == REL_BUCKETS - 1).all()
    return idx.reshape(B_BIAS_TILES * BIAS_TILE, BIAS_TILE)


C_HALF = C_ROPE // 2
C_X1A, C_X1B = C_NOPE, C_NOPE + C_HALF
C_X2A, C_X2B = LANES + C_NOPE, LANES + C_NOPE + C_HALF
C_HEAD_A_RANGES = ((0, C_X1B), (C_X2A, C_X2B))
C_HEAD_B_RANGES = ((C_X1B, C_X1B + C_HALF), (LANES, C_X2A), (C_X2B, C_X2B + C_HALF))
assert C_PAIR_LANES == 2 * LANES and C_X1B + C_HALF <= LANES


def _c_layout():
    hq = C_NOPE + C_ROPE
    hkv = C_NOPE + C_V
    nope, half = np.arange(C_NOPE), np.arange(C_HALF)
    q_idx = np.full((C_PAIRS, C_PAIR_LANES), C_HEADS * hq, np.int32)
    k_idx = np.full((C_PAIRS, C_PAIR_LANES), C_HEADS * hkv, np.int32)
    for j in range(C_PAIRS):
        a, b = 2 * j, 2 * j + 1
        q_idx[j, 0:C_NOPE] = a * hq + nope
        q_idx[j, C_X1A:C_X1A + C_HALF] = a * hq + C_NOPE + half
        q_idx[j, C_X1B:C_X1B + C_HALF] = b * hq + C_NOPE + half
        q_idx[j, LANES:LANES + C_NOPE] = b * hq + nope
        q_idx[j, C_X2A:C_X2A + C_HALF] = a * hq + C_NOPE + C_HALF + half
        q_idx[j, C_X2B:C_X2B + C_HALF] = b * hq + C_NOPE + C_HALF + half
        k_idx[j, 0:C_NOPE] = a * hkv + nope
        k_idx[j, LANES:LANES + C_NOPE] = b * hkv + nope
    v_idx = np.arange(C_HEADS)[:, None] * hkv + C_NOPE + np.arange(C_V)[None, :]
    kr_idx = np.full((C_PAIR_LANES,), C_ROPE, np.int32)
    rope_lane = np.full((C_PAIR_LANES,), -1, np.int32)
    sin_sign = np.zeros((C_PAIR_LANES,), np.float32)
    for start, second_half in ((C_X1A, False), (C_X1B, False), (C_X2A, True), (C_X2B, True)):
        kr_idx[start:start + C_HALF] = half + (C_HALF if second_half else 0)
        rope_lane[start:start + C_HALF] = half
        sin_sign[start:start + C_HALF] = 1.0 if second_half else -1.0
    return dict(q=q_idx.reshape(-1), k=k_idx.reshape(-1), v=v_idx.reshape(-1), kr=kr_idx,
                rope_lane=rope_lane, sin_sign=sin_sign)


_C_LAYOUT = _c_layout()


def _rope_tables(seq):
    pos = np.arange(seq, dtype=np.float64)
    inv_freq = ROPE_THETA ** (-np.arange(0, C_ROPE, 2, dtype=np.float64) / C_ROPE)
    ang = pos[:, None] * inv_freq[None, :]
    cos, sin = np.cos(ang), np.sin(ang)
    lane = _C_LAYOUT["rope_lane"]
    gather = np.maximum(lane, 0)
    cos_t = np.where((lane >= 0)[None, :], cos[:, gather], 1.0)
    sin_t = sin[:, gather] * _C_LAYOUT["sin_sign"][None, :]
    return jnp.asarray(cos_t, F32), jnp.asarray(sin_t, F32)


def _bias_expand_kernel(n_heads, table_ref, idx_ref, out_ref):
    rows, cols = idx_ref.shape
    for c in range(0, cols, LANES):
        idx = idx_ref[:, c:c + LANES]
        for h in range(n_heads):
            head_table = jnp.broadcast_to(table_ref[h:h + 1, :], (rows, LANES))
            out_ref[h, :, c:c + LANES] = jnp.take_along_axis(head_table, idx, axis=1)


def _bias_expand(table, idx, n_heads, head0, block_rows, name, gain=1.0):
    assert MASKED_BUCKET < LANES
    rows, cols = idx.shape
    lanes = jnp.full((n_heads, LANES), NEG, F32)
    lanes = lanes.at[:, :REL_BUCKETS].set(table[:, head0:head0 + n_heads].T * gain)
    return pl.pallas_call(
        functools.partial(_bias_expand_kernel, n_heads),
        grid=(rows // block_rows,),
        in_specs=[
            _const_spec((n_heads, LANES)),
            pl.BlockSpec((block_rows, cols), lambda i: (i, 0)),
        ],
        out_specs=pl.BlockSpec((n_heads, block_rows, cols), lambda i: (0, i, 0)),
        out_shape=jax.ShapeDtypeStruct((n_heads, rows, cols), F32),
        compiler_params=_params(1),
        name=name,
    )(lanes, jnp.asarray(idx))


def _input_kernel(x_ref, g_ref, win_ref, gq_ref, wuq_ref, gkv_ref, wuk_ref,
                  wuv_ref, wkr_ref, cos_ref, sin_ref,
                  qkva_ref, kb_ref, qbt_ref, vbt_ref, qct_ref, kc_ref, vct_ref):
    h = _rms(x_ref[...], g_ref[...]).astype(BF16)
    cos = cos_ref[...]
    sin = sin_ref[...]

    def project(c):
        t = _dot(h, win_ref[:, c:c + PROJ_CHUNK])
        if c < QB_COL0:
            qkva_ref[:, c:c + PROJ_CHUNK] = (t * A_SCORE_SCALE if c == 0 else t).astype(BF16)
        elif c == QB_COL0:
            qbt_ref[...] = (t * B_SCORE_SCALE).T.astype(BF16)
        elif c == KB_COL0:
            kb_ref[...] = t.astype(BF16)
        else:
            vbt_ref[...] = t.T.astype(BF16)

    chunks = iter(range(0, AB_COLS, PROJ_CHUNK))
    cq = _dot(h, win_ref[:, CQ_COL0:CKV_COL0])
    ckv = _dot(h, win_ref[:, CKV_COL0:KR_COL0])
    kr = _dot(h, wkr_ref[...])
    project(next(chunks))
    cqn = _rms(cq, gq_ref[...]).astype(BF16)
    project(next(chunks))
    ckvn = _rms(ckv, gkv_ref[...]).astype(BF16)
    kr_rot = kr * cos + _swap_halves(kr) * sin
    for j in range(C_PAIRS):
        cols = slice(j * C_PAIR_LANES, (j + 1) * C_PAIR_LANES)
        t = _dot(cqn, wuq_ref[:, cols])
        qct_ref[cols, :] = ((t * cos + _swap_halves(t) * sin) * C_SCORE_SCALE).T.astype(BF16)
        project(next(chunks))
    for j in range(C_PAIRS):
        cols = slice(j * C_PAIR_LANES, (j + 1) * C_PAIR_LANES)
        kc_ref[:, cols] = (_dot(ckvn, wuk_ref[:, cols]) + kr_rot).astype(BF16)
    vct_ref[...] = _dot(ckvn, wuv_ref[...]).T.astype(BF16)
    assert next(chunks, None) is None


def _input_stage(x2, layer, g, w_in, gq, wuq, gkv, wuk, wuv, wkr, cos_t, sin_t, seq):
    n = x2.shape[0]
    tm = ROW_TILE
    pos_blocks = seq // tm
    row = lambda i: (i, 0)
    col = lambda i: (0, i)
    return pl.pallas_call(
        _input_kernel,
        grid=(n // tm,),
        in_specs=[
            pl.BlockSpec((tm, D_MODEL), row),
            _layer_spec((1, D_MODEL), layer),
            _const_spec((D_MODEL, KR_COL0)),
            _layer_spec((1, C_Q_RANK), layer),
            _const_spec((C_Q_RANK, C_COLS)),
            _layer_spec((1, C_KV_RANK), layer),
            _const_spec((C_KV_RANK, C_COLS)),
            _const_spec((C_KV_RANK, C_WIDTH)),
            _const_spec((D_MODEL, C_PAIR_LANES)),
            pl.BlockSpec((tm, C_PAIR_LANES), lambda i: (i % pos_blocks, 0)),
            pl.BlockSpec((tm, C_PAIR_LANES), lambda i: (i % pos_blocks, 0)),
        ],
        out_specs=[
            pl.BlockSpec((tm, QB_COL0), row),
            pl.BlockSpec((tm, VB_COL0 - KB_COL0), row),
            pl.BlockSpec((B_WIDTH, tm), col),
            pl.BlockSpec((B_WIDTH, tm), col),
            pl.BlockSpec((C_COLS, tm), col),
            pl.BlockSpec((tm, C_COLS), row),
            pl.BlockSpec((C_WIDTH, tm), col),
        ],
        out_shape=[
            jax.ShapeDtypeStruct((n, QB_COL0), BF16),
            jax.ShapeDtypeStruct((n, VB_COL0 - KB_COL0), BF16),
            jax.ShapeDtypeStruct((B_WIDTH, n), BF16),
            jax.ShapeDtypeStruct((B_WIDTH, n), BF16),
            jax.ShapeDtypeStruct((C_COLS, n), BF16),
            jax.ShapeDtypeStruct((n, C_COLS), BF16),
            jax.ShapeDtypeStruct((C_WIDTH, n), BF16),
        ],
        compiler_params=_params(1),
        name="input_stage",
    )(x2, g, w_in, gq, wuq, gkv, wuk, wuv, wkr, cos_t, sin_t)


def _dot_tn(a, b):
    return lax.dot_general(a, b, (((0,), (0,)), ((), ())), preferred_element_type=F32)


def _dilated_kernel(q_ref, k_ref, v_ref, bias_ref, y_ref, qf_ref, kf_ref, vf_ref, m_ref, w_ref, a_ref):
    seq = q_ref.shape[0]
    chunk = ATT_TILE

    def widen(c, carry):
        rows = pl.ds(pl.multiple_of(c * chunk, chunk), chunk)
        qf_ref[rows, :] = q_ref[rows, :].astype(F32)
        kf_ref[rows, :] = k_ref[rows, :].astype(F32)
        vf_ref[rows, :] = v_ref[rows, :].astype(F32)
        return carry

    lax.fori_loop(0, seq // chunk, widen, 0)

    tile = (A_BLOCK, A_BLOCK)
    low_lane = lax.broadcasted_iota(jnp.int32, tile, 1) < HEAD_DIM
    low_row = lax.broadcasted_iota(jnp.int32, tile, 0) < HEAD_DIM
    last = len(A_PATTERNS) - 1
    for g, (_, dil) in enumerate(A_PATTERNS):
        nb = seq // dil // A_BLOCK
        span = dil * A_BLOCK

        def attend(blk, g=g, dil=dil, nb=nb, span=span):
            r = blk // nb
            n = blk % nb
            base = r + n * span
            prev_base = jnp.maximum(base - span, r)
            rows = pl.ds(base, A_BLOCK, stride=dil)
            prev_rows = pl.ds(prev_base, A_BLOCK, stride=dil)
            q = qf_ref[rows, :]
            k2 = jnp.concatenate([kf_ref[prev_rows, :], kf_ref[rows, :]], axis=0).astype(BF16)
            v2 = jnp.concatenate([vf_ref[prev_rows, :], vf_ref[rows, :]], axis=0).astype(BF16)
            first_block = jnp.where(n == 0, 1, 0)
            outs, lses = [], []
            for hh in range(2):
                qm = jnp.where(low_lane if hh == 0 else ~low_lane, q, 0.0).astype(BF16)
                s = _dot_nt(k2, qm) + bias_ref[hh, g, first_block]
                m = jnp.max(s, axis=0, keepdims=True)
                p = jnp.exp2(s - m)
                l = jnp.sum(p, axis=0, keepdims=True)
                outs.append(_dot_tn(v2, p.astype(BF16)) * (1.0 / l))
                lses.append(jnp.broadcast_to(m + jnp.log2(l), tile))
            o_blk = jnp.where(low_row, outs[0], outs[1]).T
            lse_blk = jnp.where(low_row, lses[0], lses[1]).T
            return rows, o_blk, lse_blk

        def merge(rows, o_blk, lse_blk, g=g):
            if g == 0:
                m_ref[rows, :] = lse_blk
                a_ref[rows, :] = o_blk
                return
            m_old = m_ref[rows, :]
            m_new = jnp.maximum(m_old, lse_blk)
            c_old = jnp.exp2(m_old - m_new)
            c_blk = jnp.exp2(lse_blk - m_new)
            w_new = c_blk + (c_old if g == 1 else c_old * w_ref[rows, :])
            a_new = c_old * a_ref[rows, :] + c_blk * o_blk
            if g < last:
                m_ref[rows, :] = m_new
                w_ref[rows, :] = w_new
                a_ref[rows, :] = a_new
            else:
                a_ref[rows, :] = a_new / w_new

        def step(it, carry, attend=attend, merge=merge):
            done = [attend(it * A_BLOCKS_PER_STEP + u) for u in range(A_BLOCKS_PER_STEP)]
            for args in done:
                merge(*args)
            return carry

        lax.fori_loop(0, dil * nb // A_BLOCKS_PER_STEP, step, 0)

    def narrow(c, carry):
        rows = pl.ds(pl.multiple_of(c * chunk, chunk), chunk)
        y_ref[rows, :] = a_ref[rows, :].astype(BF16)
        return carry

    lax.fori_loop(0, seq // chunk, narrow, 0)


def _dilated_attention(qkv, bias_tiles, batch, seq):
    pairs = A_HEADS // 2
    col = lambda off: (lambda b, p: (b, off * pairs + p))
    pair = 2 * HEAD_DIM
    assert pair == LANES
    state = pltpu.VMEM((seq, pair), F32)
    return pl.pallas_call(
        _dilated_kernel,
        grid=(batch, pairs),
        in_specs=[
            pl.BlockSpec((seq, pair), col(0)),
            pl.BlockSpec((seq, pair), col(1)),
            pl.BlockSpec((seq, pair), col(2)),
            pl.BlockSpec((2, len(A_PATTERNS), 2, 2 * A_BLOCK, A_BLOCK), lambda b, p: (p, 0, 0, 0, 0)),
        ],
        out_specs=pl.BlockSpec((seq, pair), lambda b, p: (b, p)),
        out_shape=jax.ShapeDtypeStruct((batch * seq, A_WIDTH), BF16),
        scratch_shapes=[state] * 6,
        compiler_params=_params(2),
        name="dilated_attention",
    )(qkv, qkv, qkv, bias_tiles)


def _flash_init(m_ref, acc_ref):
    m_ref[...] = jnp.full(m_ref.shape, NEG, F32)
    acc_ref[...] = jnp.zeros(acc_ref.shape, F32)


def _with_ones(v_t):
    return jnp.concatenate([v_t, jnp.ones((ONES_ROWS, v_t.shape[1]), v_t.dtype)], axis=0)


def _key_rows(part):
    return slice(part * KEY_PART, (part + 1) * KEY_PART)


def _store_scores(s, slot, idx, s_ref, cmax_ref):
    s_ref[slot, idx] = s
    cmax_ref[slot, idx] = jnp.max(s, axis=0, keepdims=True)


def _masked_part(slot, idx, part, masked, s_ref):
    s = s_ref[slot, idx, _key_rows(part), :]
    if masked:
        key = lax.broadcasted_iota(jnp.int32, s.shape, 0) + part * KEY_PART
        s = jnp.where(key <= lax.broadcasted_iota(jnp.int32, s.shape, 1), s, NEG)
    return s


def _flash_begin(slot, idx, masked, m_ref, s_ref, cmax_ref):
    if masked:
        col_max = functools.reduce(jnp.maximum, [
            jnp.max(_masked_part(slot, idx, part, True, s_ref), axis=0, keepdims=True)
            for part in range(ATT_TILE // KEY_PART)])
    else:
        col_max = cmax_ref[slot, idx]
    m_prev = m_ref[idx]
    m_new = jnp.maximum(m_prev, col_max)
    m_ref[idx] = m_new
    return m_new, jnp.exp2(m_prev - m_new)


def _flash_part(slot, idx, part, masked, m_new, v_ext, s_ref):
    p = jnp.exp2(_masked_part(slot, idx, part, masked, s_ref) - m_new)
    return _dot(v_ext[:, _key_rows(part)], p.astype(BF16))


def _flash_result(idx, dv, acc_ref):
    acc = acc_ref[idx]
    return acc[:dv] * (1.0 / acc[dv:dv + 1])


def _flash_sweep(n_tiles, scores, absorb, finish, m_ref, acc_ref, s_ref, cmax_ref):
    n_items = n_tiles * (n_tiles + 1) // 2
    assert n_items % 2 == 0

    def following(i, c):
        end = c == i
        return jnp.where(end, i + 1, i), jnp.where(end, 0, c + 1)

    parts = ATT_TILE // KEY_PART

    def item(nxt, cur, diag, slot):
        for idx in range(2):
            scores(*nxt, 1 - slot, idx)
            m_new, alpha = _flash_begin(slot, idx, diag, m_ref, s_ref, cmax_ref)
            pv = None
            for part in range(parts):
                piece = absorb(cur, slot, idx, part, diag, m_new)
                pv = piece if pv is None else pv + piece
            acc_ref[idx] = alpha * acc_ref[idx] + pv

    for idx in range(2):
        scores(0, 0, 0, idx)

    def two_items(u, carry):
        i0, c0 = carry
        i1, c1 = following(i0, c0)
        i2, c2 = following(i1, c1)
        i2s = jnp.minimum(i2, n_tiles - 1)

        def variant(diag0, diag1):
            def run():
                item((i1, c1), c0, diag0, 0)
                if diag0:
                    finish(i0)
                item((i2s, c2), c1, diag1, 1)
                if diag1:
                    finish(i1)
            return run

        kind = 2 * (c0 == i0).astype(jnp.int32) + (c1 == i1).astype(jnp.int32)
        lax.switch(kind, [variant(False, False), variant(False, True),
                          variant(True, False), variant(True, True)])
        return i2, c2

    lax.fori_loop(0, n_items // 2, two_items, (jnp.int32(0), jnp.int32(0)))


def _row_in_ranges(shape, ranges):
    row = lax.broadcasted_iota(jnp.int32, shape, 0)
    hit = None
    for lo, hi in ranges:
        r = (row >= lo) & (row < hi)
        hit = r if hit is None else hit | r
    return hit


def _diff_bias(bias_ref, d):
    sub = ATT_TILE // BIAS_TILE
    rows = [jnp.concatenate([bias_ref[0, jnp.clip(sub * d + qa - ka, 0, B_BIAS_TILES - 1)]
                             for qa in range(sub)], axis=1) for ka in range(sub)]
    return jnp.concatenate(rows, axis=0)


def _tile_cols(i):
    return pl.ds(pl.multiple_of(i * ATT_TILE, ATT_TILE), ATT_TILE)


def _diff_kernel(lam_init, qt_ref, k_ref, vt_ref, bias_ref, lam_ref, g_ref, y_ref,
                 qm_ref, m_ref, acc_ref, s_ref, cmax_ref):
    t = ATT_TILE
    n_tiles = qt_ref.shape[1] // t

    def split_queries(i, carry):
        qf = qt_ref[:, _tile_cols(i)].astype(F32)
        first = lax.broadcasted_iota(jnp.int32, qf.shape, 0) < B_QK_DIM
        qm_ref[0, :, _tile_cols(i)] = jnp.where(first, qf, 0.0).astype(BF16)
        qm_ref[1, :, _tile_cols(i)] = jnp.where(first, 0.0, qf).astype(BF16)
        return carry

    lax.fori_loop(0, n_tiles, split_queries, 0)
    lv = lam_ref[...]
    lam = (jnp.exp(jnp.sum(lv[0:1] * lv[1:2], axis=-1, keepdims=True))
           - jnp.exp(jnp.sum(lv[2:3] * lv[3:4], axis=-1, keepdims=True)) + lam_init)
    _flash_init(m_ref, acc_ref)

    def scores(i, c, slot, idx):
        k = k_ref[_tile_cols(c), :]
        s = _dot(k, qm_ref[idx, :, _tile_cols(i)]) + _diff_bias(bias_ref, i - c)
        _store_scores(s, slot, idx, s_ref, cmax_ref)

    def absorb(c, slot, idx, part, masked, m_new):
        v_ext = _with_ones(vt_ref[:, _tile_cols(c)])
        return _flash_part(slot, idx, part, masked, m_new, v_ext, s_ref)

    def finish(i):
        o = _flash_result(0, B_V_DIM, acc_ref) - lam * _flash_result(1, B_V_DIM, acc_ref)
        y = o * lax.rsqrt(jnp.mean(o * o, axis=0, keepdims=True) + EPS) * g_ref[...] * (1.0 - lam_init)
        y_ref[_tile_cols(i), :] = y.T.astype(BF16)
        _flash_init(m_ref, acc_ref)

    _flash_sweep(n_tiles, scores, absorb, finish, m_ref, acc_ref, s_ref, cmax_ref)


def _diff_attention(qb_t, kb, vb_t, bias_tiles, lam_vecs, g_col, batch, seq, lam_init):
    t = ATT_TILE
    qk = 2 * B_QK_DIM
    return pl.pallas_call(
        functools.partial(_diff_kernel, lam_init),
        grid=(batch, B_HEADS),
        in_specs=[
            pl.BlockSpec((qk, seq), lambda b, h: (h, b)),
            pl.BlockSpec((seq, qk), lambda b, h: (b, h)),
            pl.BlockSpec((B_V_DIM, seq), lambda b, h: (h, b)),
            pl.BlockSpec((1, B_BIAS_TILES, BIAS_TILE, BIAS_TILE), lambda b, h: (h, 0, 0, 0)),
            _const_spec((4, B_QK_DIM)),
            _const_spec((B_V_DIM, 1)),
        ],
        out_specs=pl.BlockSpec((seq, B_V_DIM), lambda b, h: (b, h)),
        out_shape=jax.ShapeDtypeStruct((batch * seq, B_WIDTH), BF16),
        scratch_shapes=[
            pltpu.VMEM((2, qk, seq), BF16),
            pltpu.VMEM((2, 1, t), F32),
            pltpu.VMEM((2, B_V_DIM + ONES_ROWS, t), F32),
            pltpu.VMEM((2, 2, t, t), F32),
            pltpu.VMEM((2, 2, 1, t), F32),
        ],
        compiler_params=_params(2),
        name="diff_attention",
    )(qb_t, kb, vb_t, bias_tiles, lam_vecs, g_col)


def _mla_kernel(qt_ref, k_ref, vt_ref, y_ref, qm_ref, m_ref, acc_ref, s_ref, cmax_ref):
    t = ATT_TILE
    n_tiles = qt_ref.shape[1] // t

    def split_heads(i, carry):
        qf = qt_ref[:, _tile_cols(i)].astype(F32)
        qm_ref[0, :, _tile_cols(i)] = jnp.where(_row_in_ranges(qf.shape, C_HEAD_A_RANGES), qf, 0.0).astype(BF16)
        qm_ref[1, :, _tile_cols(i)] = jnp.where(_row_in_ranges(qf.shape, C_HEAD_B_RANGES), qf, 0.0).astype(BF16)
        return carry

    lax.fori_loop(0, n_tiles, split_heads, 0)
    _flash_init(m_ref, acc_ref)

    def scores(i, c, slot, idx):
        k = k_ref[_tile_cols(c), :]
        _store_scores(_dot(k, qm_ref[idx, :, _tile_cols(i)]), slot, idx, s_ref, cmax_ref)

    def absorb(c, slot, idx, part, masked, m_new):
        v_t = vt_ref[idx * C_V:(idx + 1) * C_V, _tile_cols(c)]
        return _flash_part(slot, idx, part, masked, m_new, _with_ones(v_t), s_ref)

    def finish(i):
        o = jnp.concatenate([_flash_result(0, C_V, acc_ref), _flash_result(1, C_V, acc_ref)], axis=0)
        y_ref[_tile_cols(i), :] = o.T.astype(BF16)
        _flash_init(m_ref, acc_ref)

    _flash_sweep(n_tiles, scores, absorb, finish, m_ref, acc_ref, s_ref, cmax_ref)


def _mla_attention(qc_t, kc, vc_t, batch, seq):
    t = ATT_TILE
    return pl.pallas_call(
        _mla_kernel,
        grid=(batch, C_PAIRS),
        in_specs=[
            pl.BlockSpec((C_PAIR_LANES, seq), lambda b, j: (j, b)),
            pl.BlockSpec((seq, C_PAIR_LANES), lambda b, j: (b, j)),
            pl.BlockSpec((2 * C_V, seq), lambda b, j: (j, b)),
        ],
        out_specs=pl.BlockSpec((seq, 2 * C_V), lambda b, j: (b, j)),
        out_shape=jax.ShapeDtypeStruct((batch * seq, C_WIDTH), BF16),
        scratch_shapes=[
            pltpu.VMEM((2, C_PAIR_LANES, seq), BF16),
            pltpu.VMEM((2, 1, t), F32),
            pltpu.VMEM((2, C_V + ONES_ROWS, t), F32),
            pltpu.VMEM((2, 2, t, t), F32),
            pltpu.VMEM((2, 2, 1, t), F32),
        ],
        compiler_params=_params(2),
        name="mla_attention",
    )(qc_t, kc, vc_t)


def _sigmoid(z):
    return 1.0 / (1.0 + jnp.exp(-z))


def _merge_kernel(x_ref, g_ref, ya_ref, yb_ref, yc_ref, wg_ref, bg_ref, wa_ref, wb_ref, wc_ref,
                  wo_ref, out_ref):
    x = x_ref[...]
    h = _rms(x, g_ref[...]).astype(BF16)
    merged = jnp.zeros(x.shape, F32)
    for k, (y_ref, w_ref) in enumerate(((ya_ref, wa_ref), (yb_ref, wb_ref), (yc_ref, wc_ref))):
        cols = slice(k * D_MODEL, (k + 1) * D_MODEL)
        gate = _sigmoid(_dot(h, wg_ref[:, cols]) + bg_ref[:, cols])
        merged = merged + gate * _dot(y_ref[...], w_ref[...])
    out_ref[...] = x + _dot(merged.astype(BF16), wo_ref[...])


def _merge_stage(x2, layer, g, ya, yb, yc, wg, bg, wa, wb, wc, wo):
    n = x2.shape[0]
    tm = ROW_TILE
    row = lambda i: (i, 0)
    return pl.pallas_call(
        _merge_kernel,
        grid=(n // tm,),
        in_specs=[
            pl.BlockSpec((tm, D_MODEL), row),
            _layer_spec((1, D_MODEL), layer),
            pl.BlockSpec((tm, A_WIDTH), row),
            pl.BlockSpec((tm, B_WIDTH), row),
            pl.BlockSpec((tm, C_WIDTH), row),
            _layer_spec((D_MODEL, N_BRANCH * D_MODEL), layer),
            _layer_spec((1, N_BRANCH * D_MODEL), layer),
            _layer_spec((A_WIDTH, D_MODEL), layer),
            _layer_spec((B_WIDTH, D_MODEL), layer),
            _layer_spec((C_WIDTH, D_MODEL), layer),
            _layer_spec((D_MODEL, D_MODEL), layer),
        ],
        out_specs=pl.BlockSpec((tm, D_MODEL), row),
        out_shape=jax.ShapeDtypeStruct((n, D_MODEL), F32),
        compiler_params=_params(1),
        name="gated_merge",
    )(x2, g, ya, yb, yc, wg, bg, wa, wb, wc, wo)


def _ffn_kernel(final_norm, x_ref, g_ref, wg_ref, wu_ref, wd_ref, gf_ref, out_ref):
    x = x_ref[...]
    h = _rms(x, g_ref[...]).astype(BF16)
    acc = jnp.zeros(x.shape, F32)
    for c in range(0, FFN_HIDDEN, FFN_CHUNK):
        cols = slice(c, c + FFN_CHUNK)
        gate = _dot(h, wg_ref[:, cols])
        act = gate * _sigmoid(gate) * _dot(h, wu_ref[:, cols])
        acc = acc + _dot(act.astype(BF16), wd_ref[cols, :])
    y = x + acc
    if final_norm:
        y = _rms(y, gf_ref[...])
    out_ref[...] = y


def _ffn_stage(x2, layer, g, wg, wu, wd, gf, final_norm):
    n = x2.shape[0]
    tm = ROW_TILE
    row = lambda i: (i, 0)
    return pl.pallas_call(
        functools.partial(_ffn_kernel, final_norm),
        grid=(n // tm,),
        in_specs=[
            pl.BlockSpec((tm, D_MODEL), row),
            _layer_spec((1, D_MODEL), layer),
            _layer_spec((D_MODEL, FFN_HIDDEN), layer),
            _layer_spec((D_MODEL, FFN_HIDDEN), layer),
            _layer_spec((FFN_HIDDEN, D_MODEL), layer),
            _const_spec((1, D_MODEL)),
        ],
        out_specs=pl.BlockSpec((tm, D_MODEL), row),
        out_shape=jax.ShapeDtypeStruct((n, D_MODEL), F32),
        compiler_params=_params(1),
        name="swiglu_final" if final_norm else "swiglu",
    )(x2, g, wg, wu, wd, gf)


def _gather_cols(w, idx):
    idx = np.asarray(idx)
    zero = idx == w.shape[1]
    cuts = [0] + [j for j in range(1, len(idx))
                  if zero[j] != zero[j - 1] or (not zero[j] and idx[j] != idx[j - 1] + 1)]
    pieces = []
    for lo, hi in zip(cuts, cuts[1:] + [len(idx)]):
        if idx[lo] == w.shape[1]:
            pieces.append(jnp.zeros((w.shape[0], hi - lo), w.dtype))
        else:
            pieces.append(w[:, int(idx[lo]):int(idx[lo]) + hi - lo])
    return jnp.concatenate(pieces, axis=1)


def kernel(x, rel_bias_table, ln_mix_g, w_in, lambda_q1, lambda_k1, lambda_q2, lambda_k2, diff_subln_g, mla_q_norm_g, w_uq, mla_kv_norm_g, w_ukv, w_gate, b_gate, w_br_a, w_br_b, w_br_c, w_o, ln_ffn_g, w_ffn_gate, w_ffn_up, w_ffn_down, final_norm_g):
    batch, seq, _ = x.shape
    depth = w_in.shape[0]
    assert all(seq % (A_BLOCK * dil) == 0 for _, dil in A_PATTERNS)
    assert (seq // A_BLOCK) % A_BLOCKS_PER_STEP == 0
    assert seq % ATT_TILE == 0 and seq % ROW_TILE == 0
    lay = _C_LAYOUT
    cos_t, sin_t = _rope_tables(seq)

    a_tiles = _bias_expand(rel_bias_table, _a_bucket_index(), A_HEADS, 0, 2 * A_BLOCK, "bias_expand_a",
                           gain=LOG2E)
    a_tiles = a_tiles.reshape(A_HEADS, len(A_PATTERNS), 2, 2 * A_BLOCK, A_BLOCK)
    b_tiles = _bias_expand(rel_bias_table, _b_bucket_index(), B_HEADS, A_HEADS, BIAS_TILE,
                           "bias_expand_b", gain=LOG2E).reshape(B_HEADS, B_BIAS_TILES, BIAS_TILE, BIAS_TILE)

    rows = lambda v: v[:, None, :]
    w_gate16, w_o16 = _to_bf16(w_gate), _to_bf16(w_o)
    w_br16 = [_to_bf16(w) for w in (w_br_a, w_br_b, w_br_c)]
    w_ffn16 = [_to_bf16(w) for w in (w_ffn_gate, w_ffn_up, w_ffn_down)]
    g_mix, g_q, g_kv, g_ffn, b_gate3 = (rows(v) for v in (ln_mix_g, mla_q_norm_g, mla_kv_norm_g, ln_ffn_g, b_gate))

    x2 = x.reshape(batch * seq, D_MODEL)
    for l in range(depth):
        qkv_a, kb, qb_t, vb_t, qc_t, kc, vc_t = _input_stage(
            x2, l, g_mix, w_in[l][:, :KR_COL0].astype(BF16), g_q,
            _gather_cols(w_uq[l], lay["q"]).astype(BF16), g_kv,
            _gather_cols(w_ukv[l], lay["k"]).astype(BF16),
            _gather_cols(w_ukv[l], lay["v"]).astype(BF16),
            _gather_cols(w_in[l][:, KR_COL0:], lay["kr"]).astype(BF16),
            cos_t, sin_t, seq)

        ya = _dilated_attention(qkv_a, a_tiles, batch, seq)

        lam_init = 0.8 - 0.6 * math.exp(-0.3 * l)
        lam_vecs = jnp.stack([lambda_q1[l], lambda_k1[l], lambda_q2[l], lambda_k2[l]]).astype(F32)
        yb = _diff_attention(qb_t, kb, vb_t, b_tiles, lam_vecs, diff_subln_g[l][:, None],
                             batch, seq, lam_init)
        yc = _mla_attention(qc_t, kc, vc_t, batch, seq)

        x2 = _merge_stage(x2, l, g_mix, ya, yb, yc, w_gate16, b_gate3, *w_br16, w_o16)
        x2 = _ffn_stage(x2, l, g_ffn, *w_ffn16, final_norm_g[None, :], final_norm=(l == depth - 1))
    return x2.reshape(batch, seq, D_MODEL)
```

```python
import functools
import math

import numpy as np
import jax
import jax.numpy as jnp
from jax import lax
from jax.experimental import pallas as pl
from jax.experimental.pallas import tpu as pltpu

LANES = 128
D_MODEL = 1024
HEAD_DIM = 64
A_HEADS = 8
A_PATTERNS = ((2048, 16), (512, 4), (128, 1))
A_WIDTH = A_HEADS * HEAD_DIM
A_BLOCK = 128
A_BLOCKS_PER_STEP = 32
B_HEADS = 4
B_QK_DIM = HEAD_DIM
B_V_DIM = 2 * HEAD_DIM
B_WIDTH = B_HEADS * B_V_DIM
C_HEADS = 8
C_PAIRS = C_HEADS // 2
C_Q_RANK = 768
C_KV_RANK = 256
C_NOPE = 64
C_ROPE = 32
C_V = 64
C_WIDTH = C_HEADS * C_V
C_PAIR_LANES = 256
C_COLS = C_PAIRS * C_PAIR_LANES
ROPE_THETA = 10000.0
REL_BUCKETS = 32
REL_MAX_DIST = 2048
AB_COLS = 3 * A_WIDTH + 4 * B_HEADS * B_QK_DIM + B_WIDTH
QB_COL0 = 3 * A_WIDTH
KB_COL0 = QB_COL0 + 2 * B_HEADS * B_QK_DIM
VB_COL0 = KB_COL0 + 2 * B_HEADS * B_QK_DIM
CQ_COL0 = AB_COLS
CKV_COL0 = CQ_COL0 + C_Q_RANK
KR_COL0 = CKV_COL0 + C_KV_RANK
N_BRANCH = 3
FFN_HIDDEN = 2816
EPS = 1e-6
NEG = -1e30
LOG2E = math.log2(math.e)
A_SCORE_SCALE = HEAD_DIM ** -0.5 * LOG2E
B_SCORE_SCALE = B_QK_DIM ** -0.5 * LOG2E
C_SCORE_SCALE = (C_NOPE + C_ROPE) ** -0.5 * LOG2E
ONES_ROWS = 16

ROW_TILE = 1024
ATT_TILE = 512
KEY_PART = 256
BIAS_TILE = 256
B_BIAS_TILES = 8
FFN_CHUNK = 256
PROJ_CHUNK = 512
CAST_MAX_ROWS = 704
VMEM_LIMIT = 56 * 1024 * 1024

F32 = jnp.float32
BF16 = jnp.bfloat16


def _dot(a, b):
    return jnp.dot(a, b, preferred_element_type=F32)


def _dot_nt(a, b):
    return lax.dot_general(a, b, (((1,), (1,)), ((), ())), preferred_element_type=F32)


def _rms(x, g):
    return x * lax.rsqrt(jnp.mean(x * x, axis=-1, keepdims=True) + EPS) * g


def _swap_halves(t):
    half = t.shape[1] // 2
    return jnp.concatenate([t[:, half:], t[:, :half]], axis=1)


def _const_spec(shape):
    nd = len(shape)
    return pl.BlockSpec(shape, lambda *_: (0,) * nd, pipeline_mode=pl.Buffered(1))


def _layer_spec(shape, layer):
    nd = len(shape)
    return pl.BlockSpec((None,) + tuple(shape), lambda *_: (layer,) + (0,) * nd, pipeline_mode=pl.Buffered(1))


def _cast_kernel(w_ref, o_ref):
    o_ref[...] = w_ref[...].astype(o_ref.dtype)


def _to_bf16(w):
    depth, rows, cols = w.shape
    block_rows = max(r for r in range(16, CAST_MAX_ROWS + 1, 16) if rows % r == 0)
    spec = pl.BlockSpec((1, block_rows, cols), lambda l, i: (l, i, 0))
    return pl.pallas_call(
        _cast_kernel,
        grid=(depth, rows // block_rows),
        in_specs=[spec],
        out_specs=spec,
        out_shape=jax.ShapeDtypeStruct(w.shape, BF16),
        compiler_params=_params(2),
        name="weights_to_bf16",
    )(w)


def _params(n_axes):
    return pltpu.CompilerParams(dimension_semantics=("parallel",) * n_axes,
                                vmem_limit_bytes=VMEM_LIMIT)


def _t5_bucket_np(dist):
    dist = np.maximum(dist, 0)
    exact = REL_BUCKETS // 2
    ratio = np.maximum(dist, 1).astype(np.float32) / np.float32(exact)
    log_ratio = np.log(ratio).astype(np.float32) / np.float32(math.log(REL_MAX_DIST / exact))
    large = np.minimum(exact + (log_ratio * np.float32(REL_BUCKETS - exact)).astype(np.int32),
                       REL_BUCKETS - 1)
    return np.where(dist < exact, dist, large).astype(np.int32)


MASKED_BUCKET = REL_BUCKETS


def _a_bucket_index():
    kj = np.arange(2 * A_BLOCK)[:, None]
    qi = np.arange(A_BLOCK)[None, :]
    step = qi + A_BLOCK - kj
    in_band = (step >= 0) & (step <= A_BLOCK)
    tiles = []
    for _, dil in A_PATTERNS:
        bucket = _t5_bucket_np(step * dil)
        for first_block in (False, True):
            valid = in_band & (kj >= A_BLOCK) if first_block else in_band
            tiles.append(np.where(valid, bucket, MASKED_BUCKET))
    return np.concatenate(tiles, axis=0).astype(np.int32)


def _b_bucket_index():
    kj = np.arange(BIAS_TILE)[:, None]
    qi = np.arange(BIAS_TILE)[None, :]
    e = np.arange(B_BIAS_TILES)[:, None, None]
    idx = _t5_bucket_np(e * BIAS_TILE + qi - kj)
    assert (idx[-1] == REL_BUCKETS - 1).all()
    return idx.reshape(B_BIAS_TILES * BIAS_TILE, BIAS_TILE)


C_HALF = C_ROPE // 2
C_X1A, C_X1B = C_NOPE, C_NOPE + C_HALF
C_X2A, C_X2B = LANES + C_NOPE, LANES + C_NOPE + C_HALF
C_HEAD_A_RANGES = ((0, C_X1B), (C_X2A, C_X2B))
C_HEAD_B_RANGES = ((C_X1B, C_X1B + C_HALF), (LANES, C_X2A), (C_X2B, C_X2B + C_HALF))
assert C_PAIR_LANES == 2 * LANES and C_X1B + C_HALF <= LANES


def _c_layout():
    hq = C_NOPE + C_ROPE
    hkv = C_NOPE + C_V
    nope, half = np.arange(C_NOPE), np.arange(C_HALF)
    q_idx = np.full((C_PAIRS, C_PAIR_LANES), C_HEADS * hq, np.int32)
    k_idx = np.full((C_PAIRS, C_PAIR_LANES), C_HEADS * hkv, np.int32)
    for j in range(C_PAIRS):
        a, b = 2 * j, 2 * j + 1
        q_idx[j, 0:C_NOPE] = a * hq + nope
        q_idx[j, C_X1A:C_X1A + C_HALF] = a * hq + C_NOPE + half
        q_idx[j, C_X1B:C_X1B + C_HALF] = b * hq + C_NOPE + half
        q_idx[j, LANES:LANES + C_NOPE] = b * hq + nope
        q_idx[j, C_X2A:C_X2A + C_HALF] = a * hq + C_NOPE + C_HALF + half
        q_idx[j, C_X2B:C_X2B + C_HALF] = b * hq + C_NOPE + C_HALF + half
        k_idx[j, 0:C_NOPE] = a * hkv + nope
        k_idx[j, LANES:LANES + C_NOPE] = b * hkv + nope
    v_idx = np.arange(C_HEADS)[:, None] * hkv + C_NOPE + np.arange(C_V)[None, :]
    kr_idx = np.full((C_PAIR_LANES,), C_ROPE, np.int32)
    rope_lane = np.full((C_PAIR_LANES,), -1, np.int32)
    sin_sign = np.zeros((C_PAIR_LANES,), np.float32)
    for start, second_half in ((C_X1A, False), (C_X1B, False), (C_X2A, True), (C_X2B, True)):
        kr_idx[start:start + C_HALF] = half + (C_HALF if second_half else 0)
        rope_lane[start:start + C_HALF] = half
        sin_sign[start:start + C_HALF] = 1.0 if second_half else -1.0
    return dict(q=q_idx.reshape(-1), k=k_idx.reshape(-1), v=v_idx.reshape(-1), kr=kr_idx,
                rope_lane=rope_lane, sin_sign=sin_sign)


_C_LAYOUT = _c_layout()


def _rope_tables(seq):
    pos = np.arange(seq, dtype=np.float64)
    inv_freq = ROPE_THETA ** (-np.arange(0, C_ROPE, 2, dtype=np.float64) / C_ROPE)
    ang = pos[:, None] * inv_freq[None, :]
    cos, sin = np.cos(ang), np.sin(ang)
    lane = _C_LAYOUT["rope_lane"]
    gather = np.maximum(lane, 0)
    cos_t = np.where((lane >= 0)[None, :], cos[:, gather], 1.0)
    sin_t = sin[:, gather] * _C_LAYOUT["sin_sign"][None, :]
    return jnp.asarray(cos_t, F32), jnp.asarray(sin_t, F32)


def _bias_expand_kernel(n_heads, table_ref, idx_ref, out_ref):
    rows, cols = idx_ref.shape
    for c in range(0, cols, LANES):
        idx = idx_ref[:, c:c + LANES]
        for h in range(n_heads):
            head_table = jnp.broadcast_to(table_ref[h:h + 1, :], (rows, LANES))
            out_ref[h, :, c:c + LANES] = jnp.take_along_axis(head_table, idx, axis=1)


def _bias_expand(table, idx, n_heads, head0, block_rows, name, gain=1.0):
    assert MASKED_BUCKET < LANES
    rows, cols = idx.shape
    lanes = jnp.full((n_heads, LANES), NEG, F32)
    lanes = lanes.at[:, :REL_BUCKETS].set(table[:, head0:head0 + n_heads].T * gain)
    return pl.pallas_call(
        functools.partial(_bias_expand_kernel, n_heads),
        grid=(rows // block_rows,),
        in_specs=[
            _const_spec((n_heads, LANES)),
            pl.BlockSpec((block_rows, cols), lambda i: (i, 0)),
        ],
        out_specs=pl.BlockSpec((n_heads, block_rows, cols), lambda i: (0, i, 0)),
        out_shape=jax.ShapeDtypeStruct((n_heads, rows, cols), F32),
        compiler_params=_params(1),
        name=name,
    )(lanes, jnp.asarray(idx))


def _input_kernel(x_ref, g_ref, win_ref, gq_ref, wuq_ref, gkv_ref, wuk_ref,
                  wuv_ref, wkr_ref, cos_ref, sin_ref,
                  qkva_ref, kb_ref, qbt_ref, vbt_ref, qct_ref, kc_ref, vct_ref):
    h = _rms(x_ref[...], g_ref[...]).astype(BF16)
    cos = cos_ref[...]
    sin = sin_ref[...]

    def project(c):
        t = _dot(h, win_ref[:, c:c + PROJ_CHUNK])
        if c < QB_COL0:
            qkva_ref[:, c:c + PROJ_CHUNK] = (t * A_SCORE_SCALE if c == 0 else t).astype(BF16)
        elif c == QB_COL0:
            qbt_ref[...] = (t * B_SCORE_SCALE).T.astype(BF16)
        elif c == KB_COL0:
            kb_ref[...] = t.astype(BF16)
        else:
            vbt_ref[...] = t.T.astype(BF16)

    chunks = iter(range(0, AB_COLS, PROJ_CHUNK))
    cq = _dot(h, win_ref[:, CQ_COL0:CKV_COL0])
    ckv = _dot(h, win_ref[:, CKV_COL0:KR_COL0])
    kr = _dot(h, wkr_ref[...])
    project(next(chunks))
    cqn = _rms(cq, gq_ref[...]).astype(BF16)
    project(next(chunks))
    ckvn = _rms(ckv, gkv_ref[...]).astype(BF16)
    kr_rot = kr * cos + _swap_halves(kr) * sin
    for j in range(C_PAIRS):
        cols = slice(j * C_PAIR_LANES, (j + 1) * C_PAIR_LANES)
        t = _dot(cqn, wuq_ref[:, cols])
        qct_ref[cols, :] = ((t * cos + _swap_halves(t) * sin) * C_SCORE_SCALE).T.astype(BF16)
        project(next(chunks))
    for j in range(C_PAIRS):
        cols = slice(j * C_PAIR_LANES, (j + 1) * C_PAIR_LANES)
        kc_ref[:, cols] = (_dot(ckvn, wuk_ref[:, cols]) + kr_rot).astype(BF16)
    vct_ref[...] = _dot(ckvn, wuv_ref[...]).T.astype(BF16)
    assert next(chunks, None) is None


def _input_stage(x2, layer, g, w_in, gq, wuq, gkv, wuk, wuv, wkr, cos_t, sin_t, seq):
    n = x2.shape[0]
    tm = ROW_TILE
    pos_blocks = seq // tm
    row = lambda i: (i, 0)
    col = lambda i: (0, i)
    return pl.pallas_call(
        _input_kernel,
        grid=(n // tm,),
        in_specs=[
            pl.BlockSpec((tm, D_MODEL), row),
            _layer_spec((1, D_MODEL), layer),
            _const_spec((D_MODEL, KR_COL0)),
            _layer_spec((1, C_Q_RANK), layer),
            _const_spec((C_Q_RANK, C_COLS)),
            _layer_spec((1, C_KV_RANK), layer),
            _const_spec((C_KV_RANK, C_COLS)),
            _const_spec((C_KV_RANK, C_WIDTH)),
            _const_spec((D_MODEL, C_PAIR_LANES)),
            pl.BlockSpec((tm, C_PAIR_LANES), lambda i: (i % pos_blocks, 0)),
            pl.BlockSpec((tm, C_PAIR_LANES), lambda i: (i % pos_blocks, 0)),
        ],
        out_specs=[
            pl.BlockSpec((tm, QB_COL0), row),
            pl.BlockSpec((tm, VB_COL0 - KB_COL0), row),
            pl.BlockSpec((B_WIDTH, tm), col),
            pl.BlockSpec((B_WIDTH, tm), col),
            pl.BlockSpec((C_COLS, tm), col),
            pl.BlockSpec((tm, C_COLS), row),
            pl.BlockSpec((C_WIDTH, tm), col),
        ],
        out_shape=[
            jax.ShapeDtypeStruct((n, QB_COL0), BF16),
            jax.ShapeDtypeStruct((n, VB_COL0 - KB_COL0), BF16),
            jax.ShapeDtypeStruct((B_WIDTH, n), BF16),
            jax.ShapeDtypeStruct((B_WIDTH, n), BF16),
            jax.ShapeDtypeStruct((C_COLS, n), BF16),
            jax.ShapeDtypeStruct((n, C_COLS), BF16),
            jax.ShapeDtypeStruct((C_WIDTH, n), BF16),
        ],
        compiler_params=_params(1),
        name="input_stage",
    )(x2, g, w_in, gq, wuq, gkv, wuk, wuv, wkr, cos_t, sin_t)


def _dot_tn(a, b):
    return lax.dot_general(a, b, (((0,), (0,)), ((), ())), preferred_element_type=F32)


def _dilated_kernel(q_ref, k_ref, v_ref, bias_ref, y_ref, qf_ref, kf_ref, vf_ref, m_ref, w_ref, a_ref):
    seq = q_ref.shape[0]
    chunk = ATT_TILE

    def widen(c, carry):
        rows = pl.ds(pl.multiple_of(c * chunk, chunk), chunk)
        qf_ref[rows, :] = q_ref[rows, :].astype(F32)
        kf_ref[rows, :] = k_ref[rows, :].astype(F32)
        vf_ref[rows, :] = v_ref[rows, :].astype(F32)
        return carry

    lax.fori_loop(0, seq // chunk, widen, 0)

    tile = (A_BLOCK, A_BLOCK)
    low_lane = lax.broadcasted_iota(jnp.int32, tile, 1) < HEAD_DIM
    low_row = lax.broadcasted_iota(jnp.int32, tile, 0) < HEAD_DIM
    last = len(A_PATTERNS) - 1
    for g, (_, dil) in enumerate(A_PATTERNS):
        nb = seq // dil // A_BLOCK
        span = dil * A_BLOCK

        def attend(blk, g=g, dil=dil, nb=nb, span=span):
            r = blk // nb
            n = blk % nb
            base = r + n * span
            prev_base = jnp.maximum(base - span, r)
            rows = pl.ds(base, A_BLOCK, stride=dil)
            prev_rows = pl.ds(prev_base, A_BLOCK, stride=dil)
            q = qf_ref[rows, :]
            k2 = jnp.concatenate([kf_ref[prev_rows, :], kf_ref[rows, :]], axis=0).astype(BF16)
            v2 = jnp.concatenate([vf_ref[prev_rows, :], vf_ref[rows, :]], axis=0).astype(BF16)
            first_block = jnp.where(n == 0, 1, 0)
            outs, lses = [], []
            for hh in range(2):
                qm = jnp.where(low_lane if hh == 0 else ~low_lane, q, 0.0).astype(BF16)
                s = _dot_nt(k2, qm) + bias_ref[hh, g, first_block]
                m = jnp.max(s, axis=0, keepdims=True)
                p = jnp.exp2(s - m)
                l = jnp.sum(p, axis=0, keepdims=True)
                outs.append(_dot_tn(v2, p.astype(BF16)) * (1.0 / l))
                lses.append(jnp.broadcast_to(m + jnp.log2(l), tile))
            o_blk = jnp.where(low_row, outs[0], outs[1]).T
            lse_blk = jnp.where(low_row, lses[0], lses[1]).T
            return rows, o_blk, lse_blk

        def merge(rows, o_blk, lse_blk, g=g):
            if g == 0:
                m_ref[rows, :] = lse_blk
                a_ref[rows, :] = o_blk
                return
            m_old = m_ref[rows, :]
            m_new = jnp.maximum(m_old, lse_blk)
            c_old = jnp.exp2(m_old - m_new)
            c_blk = jnp.exp2(lse_blk - m_new)
            w_new = c_blk + (c_old if g == 1 else c_old * w_ref[rows, :])
            a_new = c_old * a_ref[rows, :] + c_blk * o_blk
            if g < last:
                m_ref[rows, :] = m_new
                w_ref[rows, :] = w_new
                a_ref[rows, :] = a_new
            else:
                a_ref[rows, :] = a_new / w_new

        def step(it, carry, attend=attend, merge=merge):
            done = [attend(it * A_BLOCKS_PER_STEP + u) for u in range(A_BLOCKS_PER_STEP)]
            for args in done:
                merge(*args)
            return carry

        lax.fori_loop(0, dil * nb // A_BLOCKS_PER_STEP, step, 0)

    def narrow(c, carry):
        rows = pl.ds(pl.multiple_of(c * chunk, chunk), chunk)
        y_ref[rows, :] = a_ref[rows, :].astype(BF16)
        return carry

    lax.fori_loop(0, seq // chunk, narrow, 0)


def _dilated_attention(qkv, bias_tiles, batch, seq):
    pairs = A_HEADS // 2
    col = lambda off: (lambda b, p: (b, off * pairs + p))
    pair = 2 * HEAD_DIM
    assert pair == LANES
    state = pltpu.VMEM((seq, pair), F32)
    return pl.pallas_call(
        _dilated_kernel,
        grid=(batch, pairs),
        in_specs=[
            pl.BlockSpec((seq, pair), col(0)),
            pl.BlockSpec((seq, pair), col(1)),
            pl.BlockSpec((seq, pair), col(2)),
            pl.BlockSpec((2, len(A_PATTERNS), 2, 2 * A_BLOCK, A_BLOCK), lambda b, p: (p, 0, 0, 0, 0)),
        ],
        out_specs=pl.BlockSpec((seq, pair), lambda b, p: (b, p)),
        out_shape=jax.ShapeDtypeStruct((batch * seq, A_WIDTH), BF16),
        scratch_shapes=[state] * 6,
        compiler_params=_params(2),
        name="dilated_attention",
    )(qkv, qkv, qkv, bias_tiles)


def _flash_init(m_ref, acc_ref):
    m_ref[...] = jnp.full(m_ref.shape, NEG, F32)
    acc_ref[...] = jnp.zeros(acc_ref.shape, F32)


def _with_ones(v_t):
    return jnp.concatenate([v_t, jnp.ones((ONES_ROWS, v_t.shape[1]), v_t.dtype)], axis=0)


def _key_rows(part):
    return slice(part * KEY_PART, (part + 1) * KEY_PART)


def _store_scores(s, slot, idx, s_ref, cmax_ref):
    s_ref[slot, idx] = s
    cmax_ref[slot, idx] = jnp.max(s, axis=0, keepdims=True)


def _masked_part(slot, idx, part, masked, s_ref):
    s = s_ref[slot, idx, _key_rows(part), :]
    if masked:
        key = lax.broadcasted_iota(jnp.int32, s.shape, 0) + part * KEY_PART
        s = jnp.where(key <= lax.broadcasted_iota(jnp.int32, s.shape, 1), s, NEG)
    return s


def _flash_begin(slot, idx, masked, m_ref, s_ref, cmax_ref):
    if masked:
        col_max = functools.reduce(jnp.maximum, [
            jnp.max(_masked_part(slot, idx, part, True, s_ref), axis=0, keepdims=True)
            for part in range(ATT_TILE // KEY_PART)])
    else:
        col_max = cmax_ref[slot, idx]
    m_prev = m_ref[idx]
    m_new = jnp.maximum(m_prev, col_max)
    m_ref[idx] = m_new
    return m_new, jnp.exp2(m_prev - m_new)


def _flash_part(slot, idx, part, masked, m_new, v_ext, s_ref):
    p = jnp.exp2(_masked_part(slot, idx, part, masked, s_ref) - m_new)
    return _dot(v_ext[:, _key_rows(part)], p.astype(BF16))


def _flash_result(idx, dv, acc_ref):
    acc = acc_ref[idx]
    return acc[:dv] * (1.0 / acc[dv:dv + 1])


def _flash_sweep(n_tiles, scores, absorb, finish, m_ref, acc_ref, s_ref, cmax_ref):
    n_items = n_tiles * (n_tiles + 1) // 2
    assert n_items % 2 == 0

    def following(i, c):
        end = c == i
        return jnp.where(end, i + 1, i), jnp.where(end, 0, c + 1)

    parts = ATT_TILE // KEY_PART

    def item(nxt, cur, diag, slot):
        for idx in range(2):
            scores(*nxt, 1 - slot, idx)
            m_new, alpha = _flash_begin(slot, idx, diag, m_ref, s_ref, cmax_ref)
            pv = None
            for part in range(parts):
                piece = absorb(cur, slot, idx, part, diag, m_new)
                pv = piece if pv is None else pv + piece
            acc_ref[idx] = alpha * acc_ref[idx] + pv

    for idx in range(2):
        scores(0, 0, 0, idx)

    def two_items(u, carry):
        i0, c0 = carry
        i1, c1 = following(i0, c0)
        i2, c2 = following(i1, c1)
        i2s = jnp.minimum(i2, n_tiles - 1)

        def variant(diag0, diag1):
            def run():
                item((i1, c1), c0, diag0, 0)
                if diag0:
                    finish(i0)
                item((i2s, c2), c1, diag1, 1)
                if diag1:
                    finish(i1)
            return run

        kind = 2 * (c0 == i0).astype(jnp.int32) + (c1 == i1).astype(jnp.int32)
        lax.switch(kind, [variant(False, False), variant(False, True),
                          variant(True, False), variant(True, True)])
        return i2, c2

    lax.fori_loop(0, n_items // 2, two_items, (jnp.int32(0), jnp.int32(0)))


def _row_in_ranges(shape, ranges):
    row = lax.broadcasted_iota(jnp.int32, shape, 0)
    hit = None
    for lo, hi in ranges:
        r = (row >= lo) & (row < hi)
        hit = r if hit is None else hit | r
    return hit


def _diff_bias(bias_ref, d):
    sub = ATT_TILE // BIAS_TILE
    rows = [jnp.concatenate([bias_ref[0, jnp.clip(sub * d + qa - ka, 0, B_BIAS_TILES - 1)]
                             for qa in range(sub)], axis=1) for ka in range(sub)]
    return jnp.concatenate(rows, axis=0)


def _tile_cols(i):
    return pl.ds(pl.multiple_of(i * ATT_TILE, ATT_TILE), ATT_TILE)


def _diff_kernel(lam_init, qt_ref, k_ref, vt_ref, bias_ref, lam_ref, g_ref, y_ref,
                 qm_ref, m_ref, acc_ref, s_ref, cmax_ref):
    t = ATT_TILE
    n_tiles = qt_ref.shape[1] // t

    def split_queries(i, carry):
        qf = qt_ref[:, _tile_cols(i)].astype(F32)
        first = lax.broadcasted_iota(jnp.int32, qf.shape, 0) < B_QK_DIM
        qm_ref[0, :, _tile_cols(i)] = jnp.where(first, qf, 0.0).astype(BF16)
        qm_ref[1, :, _tile_cols(i)] = jnp.where(first, 0.0, qf).astype(BF16)
        return carry

    lax.fori_loop(0, n_tiles, split_queries, 0)
    lv = lam_ref[...]
    lam = (jnp.exp(jnp.sum(lv[0:1] * lv[1:2], axis=-1, keepdims=True))
           - jnp.exp(jnp.sum(lv[2:3] * lv[3:4], axis=-1, keepdims=True)) + lam_init)
    _flash_init(m_ref, acc_ref)

    def scores(i, c, slot, idx):
        k = k_ref[_tile_cols(c), :]
        s = _dot(k, qm_ref[idx, :, _tile_cols(i)]) + _diff_bias(bias_ref, i - c)
        _store_scores(s, slot, idx, s_ref, cmax_ref)

    def absorb(c, slot, idx, part, masked, m_new):
        v_ext = _with_ones(vt_ref[:, _tile_cols(c)])
        return _flash_part(slot, idx, part, masked, m_new, v_ext, s_ref)

    def finish(i):
        o = _flash_result(0, B_V_DIM, acc_ref) - lam * _flash_result(1, B_V_DIM, acc_ref)
        y = o * lax.rsqrt(jnp.mean(o * o, axis=0, keepdims=True) + EPS) * g_ref[...] * (1.0 - lam_init)
        y_ref[_tile_cols(i), :] = y.T.astype(BF16)
        _flash_init(m_ref, acc_ref)

    _flash_sweep(n_tiles, scores, absorb, finish, m_ref, acc_ref, s_ref, cmax_ref)


def _diff_attention(qb_t, kb, vb_t, bias_tiles, lam_vecs, g_col, batch, seq, lam_init):
    t = ATT_TILE
    qk = 2 * B_QK_DIM
    return pl.pallas_call(
        functools.partial(_diff_kernel, lam_init),
        grid=(batch, B_HEADS),
        in_specs=[
            pl.BlockSpec((qk, seq), lambda b, h: (h, b)),
            pl.BlockSpec((seq, qk), lambda b, h: (b, h)),
            pl.BlockSpec((B_V_DIM, seq), lambda b, h: (h, b)),
            pl.BlockSpec((1, B_BIAS_TILES, BIAS_TILE, BIAS_TILE), lambda b, h: (h, 0, 0, 0)),
            _const_spec((4, B_QK_DIM)),
            _const_spec((B_V_DIM, 1)),
        ],
        out_specs=pl.BlockSpec((seq, B_V_DIM), lambda b, h: (b, h)),
        out_shape=jax.ShapeDtypeStruct((batch * seq, B_WIDTH), BF16),
        scratch_shapes=[
            pltpu.VMEM((2, qk, seq), BF16),
            pltpu.VMEM((2, 1, t), F32),
            pltpu.VMEM((2, B_V_DIM + ONES_ROWS, t), F32),
            pltpu.VMEM((2, 2, t, t), F32),
            pltpu.VMEM((2, 2, 1, t), F32),
        ],
        compiler_params=_params(2),
        name="diff_attention",
    )(qb_t, kb, vb_t, bias_tiles, lam_vecs, g_col)


def _mla_kernel(qt_ref, k_ref, vt_ref, y_ref, qm_ref, m_ref, acc_ref, s_ref, cmax_ref):
    t = ATT_TILE
    n_tiles = qt_ref.shape[1] // t

    def split_heads(i, carry):
        qf = qt_ref[:, _tile_cols(i)].astype(F32)
        qm_ref[0, :, _tile_cols(i)] = jnp.where(_row_in_ranges(qf.shape, C_HEAD_A_RANGES), qf, 0.0).astype(BF16)
        qm_ref[1, :, _tile_cols(i)] = jnp.where(_row_in_ranges(qf.shape, C_HEAD_B_RANGES), qf, 0.0).astype(BF16)
        return carry

    lax.fori_loop(0, n_tiles, split_heads, 0)
    _flash_init(m_ref, acc_ref)

    def scores(i, c, slot, idx):
        k = k_ref[_tile_cols(c), :]
        _store_scores(_dot(k, qm_ref[idx, :, _tile_cols(i)]), slot, idx, s_ref, cmax_ref)

    def absorb(c, slot, idx, part, masked, m_new):
        v_t = vt_ref[idx * C_V:(idx + 1) * C_V, _tile_cols(c)]
        return _flash_part(slot, idx, part, masked, m_new, _with_ones(v_t), s_ref)

    def finish(i):
        o = jnp.concatenate([_flash_result(0, C_V, acc_ref), _flash_result(1, C_V, acc_ref)], axis=0)
        y_ref[_tile_cols(i), :] = o.T.astype(BF16)
        _flash_init(m_ref, acc_ref)

    _flash_sweep(n_tiles, scores, absorb, finish, m_ref, acc_ref, s_ref, cmax_ref)


def _mla_attention(qc_t, kc, vc_t, batch, seq):
    t = ATT_TILE
    return pl.pallas_call(
        _mla_kernel,
        grid=(batch, C_PAIRS),
        in_specs=[
            pl.BlockSpec((C_PAIR_LANES, seq), lambda b, j: (j, b)),
            pl.BlockSpec((seq, C_PAIR_LANES), lambda b, j: (b, j)),
            pl.BlockSpec((2 * C_V, seq), lambda b, j: (j, b)),
        ],
        out_specs=pl.BlockSpec((seq, 2 * C_V), lambda b, j: (b, j)),
        out_shape=jax.ShapeDtypeStruct((batch * seq, C_WIDTH), BF16),
        scratch_shapes=[
            pltpu.VMEM((2, C_PAIR_LANES, seq), BF16),
            pltpu.VMEM((2, 1, t), F32),
            pltpu.VMEM((2, C_V + ONES_ROWS, t), F32),
            pltpu.VMEM((2, 2, t, t), F32),
            pltpu.VMEM((2, 2, 1, t), F32),
        ],
        compiler_params=_params(2),
        name="mla_attention",
    )(qc_t, kc, vc_t)


def _sigmoid(z):
    return 1.0 / (1.0 + jnp.exp(-z))


def _merge_kernel(x_ref, g_ref, ya_ref, yb_ref, yc_ref, wg_ref, bg_ref, wa_ref, wb_ref, wc_ref,
                  wo_ref, out_ref):
    x = x_ref[...]
    h = _rms(x, g_ref[...]).astype(BF16)
    merged = jnp.zeros(x.shape, F32)
    for k, (y_ref, w_ref) in enumerate(((ya_ref, wa_ref), (yb_ref, wb_ref), (yc_ref, wc_ref))):
        cols = slice(k * D_MODEL, (k + 1) * D_MODEL)
        gate = _sigmoid(_dot(h, wg_ref[:, cols]) + bg_ref[:, cols])
        merged = merged + gate * _dot(y_ref[...], w_ref[...])
    out_ref[...] = x + _dot(merged.astype(BF16), wo_ref[...])


def _merge_stage(x2, layer, g, ya, yb, yc, wg, bg, wa, wb, wc, wo):
    n = x2.shape[0]
    tm = ROW_TILE
    row = lambda i: (i, 0)
    return pl.pallas_call(
        _merge_kernel,
        grid=(n // tm,),
        in_specs=[
            pl.BlockSpec((tm, D_MODEL), row),
            _layer_spec((1, D_MODEL), layer),
            pl.BlockSpec((tm, A_WIDTH), row),
            pl.BlockSpec((tm, B_WIDTH), row),
            pl.BlockSpec((tm, C_WIDTH), row),
            _layer_spec((D_MODEL, N_BRANCH * D_MODEL), layer),
            _layer_spec((1, N_BRANCH * D_MODEL), layer),
            _layer_spec((A_WIDTH, D_MODEL), layer),
            _layer_spec((B_WIDTH, D_MODEL), layer),
            _layer_spec((C_WIDTH, D_MODEL), layer),
            _layer_spec((D_MODEL, D_MODEL), layer),
        ],
        out_specs=pl.BlockSpec((tm, D_MODEL), row),
        out_shape=jax.ShapeDtypeStruct((n, D_MODEL), F32),
        compiler_params=_params(1),
        name="gated_merge",
    )(x2, g, ya, yb, yc, wg, bg, wa, wb, wc, wo)


def _ffn_kernel(final_norm, x_ref, g_ref, wg_ref, wu_ref, wd_ref, gf_ref, out_ref):
    x = x_ref[...]
    h = _rms(x, g_ref[...]).astype(BF16)
    acc = jnp.zeros(x.shape, F32)
    for c in range(0, FFN_HIDDEN, FFN_CHUNK):
        cols = slice(c, c + FFN_CHUNK)
        gate = _dot(h, wg_ref[:, cols])
        act = gate * _sigmoid(gate) * _dot(h, wu_ref[:, cols])
        acc = acc + _dot(act.astype(BF16), wd_ref[cols, :])
    y = x + acc
    if final_norm:
        y = _rms(y, gf_ref[...])
    out_ref[...] = y


def _ffn_stage(x2, layer, g, wg, wu, wd, gf, final_norm):
    n = x2.shape[0]
    tm = ROW_TILE
    row = lambda i: (i, 0)
    return pl.pallas_call(
        functools.partial(_ffn_kernel, final_norm),
        grid=(n // tm,),
        in_specs=[
            pl.BlockSpec((tm, D_MODEL), row),
            _layer_spec((1, D_MODEL), layer),
            _layer_spec((D_MODEL, FFN_HIDDEN), layer),
            _layer_spec((D_MODEL, FFN_HIDDEN), layer),
            _layer_spec((FFN_HIDDEN, D_MODEL), layer),
            _const_spec((1, D_MODEL)),
        ],
        out_specs=pl.BlockSpec((tm, D_MODEL), row),
        out_shape=jax.ShapeDtypeStruct((n, D_MODEL), F32),
        compiler_params=_params(1),
        name="swiglu_final" if final_norm else "swiglu",
    )(x2, g, wg, wu, wd, gf)


def _gather_cols(w, idx):
    idx = np.asarray(idx)
    zero = idx == w.shape[1]
    cuts = [0] + [j for j in range(1, len(idx))
                  if zero[j] != zero[j - 1] or (not zero[j] and idx[j] != idx[j - 1] + 1)]
    pieces = []
    for lo, hi in zip(cuts, cuts[1:] + [len(idx)]):
        if idx[lo] == w.shape[1]:
            pieces.append(jnp.zeros((w.shape[0], hi - lo), w.dtype))
        else:
            pieces.append(w[:, int(idx[lo]):int(idx[lo]) + hi - lo])
    return jnp.concatenate(pieces, axis=1)


def kernel(x, rel_bias_table, ln_mix_g, w_in, lambda_q1, lambda_k1, lambda_q2, lambda_k2, diff_subln_g, mla_q_norm_g, w_uq, mla_kv_norm_g, w_ukv, w_gate, b_gate, w_br_a, w_br_b, w_br_c, w_o, ln_ffn_g, w_ffn_gate, w_ffn_up, w_ffn_down, final_norm_g):
    batch, seq, _ = x.shape
    depth = w_in.shape[0]
    assert all(seq % (A_BLOCK * dil) == 0 for _, dil in A_PATTERNS)
    assert (seq // A_BLOCK) % A_BLOCKS_PER_STEP == 0
    assert seq % ATT_TILE == 0 and seq % ROW_TILE == 0
    lay = _C_LAYOUT
    cos_t, sin_t = _rope_tables(seq)

    a_tiles = _bias_expand(rel_bias_table, _a_bucket_index(), A_HEADS, 0, 2 * A_BLOCK, "bias_expand_a",
                           gain=LOG2E)
    a_tiles = a_tiles.reshape(A_HEADS, len(A_PATTERNS), 2, 2 * A_BLOCK, A_BLOCK)
    b_tiles = _bias_expand(rel_bias_table, _b_bucket_index(), B_HEADS, A_HEADS, BIAS_TILE,
                           "bias_expand_b", gain=LOG2E).reshape(B_HEADS, B_BIAS_TILES, BIAS_TILE, BIAS_TILE)

    rows = lambda v: v[:, None, :]
    w_gate16, w_o16 = _to_bf16(w_gate), _to_bf16(w_o)
    w_br16 = [_to_bf16(w) for w in (w_br_a, w_br_b, w_br_c)]
    w_ffn16 = [_to_bf16(w) for w in (w_ffn_gate, w_ffn_up, w_ffn_down)]
    g_mix, g_q, g_kv, g_ffn, b_gate3 = (rows(v) for v in (ln_mix_g, mla_q_norm_g, mla_kv_norm_g, ln_ffn_g, b_gate))

    x2 = x.reshape(batch * seq, D_MODEL)
    for l in range(depth):
        qkv_a, kb, qb_t, vb_t, qc_t, kc, vc_t = _input_stage(
            x2, l, g_mix, w_in[l][:, :KR_COL0].astype(BF16), g_q,
            _gather_cols(w_uq[l], lay["q"]).astype(BF16), g_kv,
            _gather_cols(w_ukv[l], lay["k"]).astype(BF16),
            _gather_cols(w_ukv[l], lay["v"]).astype(BF16),
            _gather_cols(w_in[l][:, KR_COL0:], lay["kr"]).astype(BF16),
            cos_t, sin_t, seq)

        ya = _dilated_attention(qkv_a, a_tiles, batch, seq)

        lam_init = 0.8 - 0.6 * math.exp(-0.3 * l)
        lam_vecs = jnp.stack([lambda_q1[l], lambda_k1[l], lambda_q2[l], lambda_k2[l]]).astype(F32)
        yb = _diff_attention(qb_t, kb, vb_t, b_tiles, lam_vecs, diff_subln_g[l][:, None],
                             batch, seq, lam_init)
        yc = _mla_attention(qc_t, kc, vc_t, batch, seq)

        x2 = _merge_stage(x2, l, g_mix, ya, yb, yc, w_gate16, b_gate3, *w_br16, w_o16)
        x2 = _ffn_stage(x2, l, g_ffn, *w_ffn16, final_norm_g[None, :], final_norm=(l == depth - 1))
    return x2.reshape(batch, seq, D_MODEL)
```

```python
import functools
import math

import numpy as np
import jax
import jax.numpy as jnp
from jax import lax
from jax.experimental import pallas as pl
from jax.experimental.pallas import tpu as pltpu

LANES = 128
D_MODEL = 1024
HEAD_DIM = 64
A_HEADS = 8
A_PATTERNS = ((2048, 16), (512, 4), (128, 1))
A_WIDTH = A_HEADS * HEAD_DIM
A_BLOCK = 128
A_BLOCKS_PER_STEP = 32
B_HEADS = 4
B_QK_DIM = HEAD_DIM
B_V_DIM = 2 * HEAD_DIM
B_WIDTH = B_HEADS * B_V_DIM
C_HEADS = 8
C_PAIRS = C_HEADS // 2
C_Q_RANK = 768
C_KV_RANK = 256
C_NOPE = 64
C_ROPE = 32
C_V = 64
C_WIDTH = C_HEADS * C_V
C_PAIR_LANES = 256
C_COLS = C_PAIRS * C_PAIR_LANES
ROPE_THETA = 10000.0
REL_BUCKETS = 32
REL_MAX_DIST = 2048
AB_COLS = 3 * A_WIDTH + 4 * B_HEADS * B_QK_DIM + B_WIDTH
QB_COL0 = 3 * A_WIDTH
KB_COL0 = QB_COL0 + 2 * B_HEADS * B_QK_DIM
VB_COL0 = KB_COL0 + 2 * B_HEADS * B_QK_DIM
CQ_COL0 = AB_COLS
CKV_COL0 = CQ_COL0 + C_Q_RANK
KR_COL0 = CKV_COL0 + C_KV_RANK
N_BRANCH = 3
FFN_HIDDEN = 2816
EPS = 1e-6
NEG = -1e30
LOG2E = math.log2(math.e)
A_SCORE_SCALE = HEAD_DIM ** -0.5 * LOG2E
B_SCORE_SCALE = B_QK_DIM ** -0.5 * LOG2E
C_SCORE_SCALE = (C_NOPE + C_ROPE) ** -0.5 * LOG2E
ONES_ROWS = 16

ROW_TILE = 1024
ATT_TILE = 512
KEY_PART = 256
BIAS_TILE = 256
B_BIAS_TILES = 8
FFN_CHUNK = 256
PROJ_CHUNK = 512
CAST_MAX_ROWS = 704
VMEM_LIMIT = 56 * 1024 * 1024

F32 = jnp.float32
BF16 = jnp.bfloat16


def _dot(a, b):
    return jnp.dot(a, b, preferred_element_type=F32)


def _dot_nt(a, b):
    return lax.dot_general(a, b, (((1,), (1,)), ((), ())), preferred_element_type=F32)


def _rms(x, g):
    return x * lax.rsqrt(jnp.mean(x * x, axis=-1, keepdims=True) + EPS) * g


def _swap_halves(t):
    half = t.shape[1] // 2
    return jnp.concatenate([t[:, half:], t[:, :half]], axis=1)


def _const_spec(shape):
    nd = len(shape)
    return pl.BlockSpec(shape, lambda *_: (0,) * nd, pipeline_mode=pl.Buffered(1))


def _layer_spec(shape, layer):
    nd = len(shape)
    return pl.BlockSpec((None,) + tuple(shape), lambda *_: (layer,) + (0,) * nd, pipeline_mode=pl.Buffered(1))


def _cast_kernel(w_ref, o_ref):
    o_ref[...] = w_ref[...].astype(o_ref.dtype)


def _to_bf16(w):
    depth, rows, cols = w.shape
    block_rows = max(r for r in range(16, CAST_MAX_ROWS + 1, 16) if rows % r == 0)
    spec = pl.BlockSpec((1, block_rows, cols), lambda l, i: (l, i, 0))
    return pl.pallas_call(
        _cast_kernel,
        grid=(depth, rows // block_rows),
        in_specs=[spec],
        out_specs=spec,
        out_shape=jax.ShapeDtypeStruct(w.shape, BF16),
        compiler_params=_params(2),
        name="weights_to_bf16",
    )(w)


def _params(n_axes):
    return pltpu.CompilerParams(dimension_semantics=("parallel",) * n_axes,
                                vmem_limit_bytes=VMEM_LIMIT)


def _t5_bucket_np(dist):
    dist = np.maximum(dist, 0)
    exact = REL_BUCKETS // 2
    ratio = np.maximum(dist, 1).astype(np.float32) / np.float32(exact)
    log_ratio = np.log(ratio).astype(np.float32) / np.float32(math.log(REL_MAX_DIST / exact))
    large = np.minimum(exact + (log_ratio * np.float32(REL_BUCKETS - exact)).astype(np.int32),
                       REL_BUCKETS - 1)
    return np.where(dist < exact, dist, large).astype(np.int32)


MASKED_BUCKET = REL_BUCKETS


def _a_bucket_index():
    kj = np.arange(2 * A_BLOCK)[:, None]
    qi = np.arange(A_BLOCK)[None, :]
    step = qi + A_BLOCK - kj
    in_band = (step >= 0) & (step <= A_BLOCK)
    tiles = []
    for _, dil in A_PATTERNS:
        bucket = _t5_bucket_np(step * dil)
        for first_block in (False, True):
            valid = in_band & (kj >= A_BLOCK) if first_block else in_band
            tiles.append(np.where(valid, bucket, MASKED_BUCKET))
    return np.concatenate(tiles, axis=0).astype(np.int32)


def _b_bucket_index():
    kj = np.arange(BIAS_TILE)[:, None]
    qi = np.arange(BIAS_TILE)[None, :]
    e = np.arange(B_BIAS_TILES)[:, None, None]
    idx = _t5_bucket_np(e * BIAS_TILE + qi - kj)
    assert (idx[-1] == REL_BUCKETS - 1).all()
    return idx.reshape(B_BIAS_TILES * BIAS_TILE, BIAS_TILE)


C_HALF = C_ROPE // 2
C_X1A, C_X1B = C_NOPE, C_NOPE + C_HALF
C_X2A, C_X2B = LANES + C_NOPE, LANES + C_NOPE + C_HALF
C_HEAD_A_RANGES = ((0, C_X1B), (C_X2A, C_X2B))
C_HEAD_B_RANGES = ((C_X1B, C_X1B + C_HALF), (LANES, C_X2A), (C_X2B, C_X2B + C_HALF))
assert C_PAIR_LANES == 2 * LANES and C_X1B + C_HALF <= LANES


def _c_layout():
    hq = C_NOPE + C_ROPE
    hkv = C_NOPE + C_V
    nope, half = np.arange(C_NOPE), np.arange(C_HALF)
    q_idx = np.full((C_PAIRS, C_PAIR_LANES), C_HEADS * hq, np.int32)
    k_idx = np.full((C_PAIRS, C_PAIR_LANES), C_HEADS * hkv, np.int32)
    for j in range(C_PAIRS):
        a, b = 2 * j, 2 * j + 1
        q_idx[j, 0:C_NOPE] = a * hq + nope
        q_idx[j, C_X1A:C_X1A + C_HALF] = a * hq + C_NOPE + half
        q_idx[j, C_X1B:C_X1B + C_HALF] = b * hq + C_NOPE + half
        q_idx[j, LANES:LANES + C_NOPE] = b * hq + nope
        q_idx[j, C_X2A:C_X2A + C_HALF] = a * hq + C_NOPE + C_HALF + half
        q_idx[j, C_X2B:C_X2B + C_HALF] = b * hq + C_NOPE + C_HALF + half
        k_idx[j, 0:C_NOPE] = a * hkv + nope
        k_idx[j, LANES:LANES + C_NOPE] = b * hkv + nope
    v_idx = np.arange(C_HEADS)[:, None] * hkv + C_NOPE + np.arange(C_V)[None, :]
    kr_idx = np.full((C_PAIR_LANES,), C_ROPE, np.int32)
    rope_lane = np.full((C_PAIR_LANES,), -1, np.int32)
    sin_sign = np.zeros((C_PAIR_LANES,), np.float32)
    for start, second_half in ((C_X1A, False), (C_X1B, False), (C_X2A, True), (C_X2B, True)):
        kr_idx[start:start + C_HALF] = half + (C_HALF if second_half else 0)
        rope_lane[start:start + C_HALF] = half
        sin_sign[start:start + C_HALF] = 1.0 if second_half else -1.0
    return dict(q=q_idx.reshape(-1), k=k_idx.reshape(-1), v=v_idx.reshape(-1), kr=kr_idx,
                rope_lane=rope_lane, sin_sign=sin_sign)


_C_LAYOUT = _c_layout()


def _rope_tables(seq):
    pos = np.arange(seq, dtype=np.float64)
    inv_freq = ROPE_THETA ** (-np.arange(0, C_ROPE, 2, dtype=np.float64) / C_ROPE)
    ang = pos[:, None] * inv_freq[None, :]
    cos, sin = np.cos(ang), np.sin(ang)
    lane = _C_LAYOUT["rope_lane"]
    gather = np.maximum(lane, 0)
    cos_t = np.where((lane >= 0)[None, :], cos[:, gather], 1.0)
    sin_t = sin[:, gather] * _C_LAYOUT["sin_sign"][None, :]
    return jnp.asarray(cos_t, F32), jnp.asarray(sin_t, F32)


def _bias_expand_kernel(n_heads, table_ref, idx_ref, out_ref):
    rows, cols = idx_ref.shape
    for c in range(0, cols, LANES):
        idx = idx_ref[:, c:c + LANES]
        for h in range(n_heads):
            head_table = jnp.broadcast_to(table_ref[h:h + 1, :], (rows, LANES))
            out_ref[h, :, c:c + LANES] = jnp.take_along_axis(head_table, idx, axis=1)


def _bias_expand(table, idx, n_heads, head0, block_rows, name, gain=1.0):
    assert MASKED_BUCKET < LANES
    rows, cols = idx.shape
    lanes = jnp.full((n_heads, LANES), NEG, F32)
    lanes = lanes.at[:, :REL_BUCKETS].set(table[:, head0:head0 + n_heads].T * gain)
    return pl.pallas_call(
        functools.partial(_bias_expand_kernel, n_heads),
        grid=(rows // block_rows,),
        in_specs=[
            _const_spec((n_heads, LANES)),
            pl.BlockSpec((block_rows, cols), lambda i: (i, 0)),
        ],
        out_specs=pl.BlockSpec((n_heads, block_rows, cols), lambda i: (0, i, 0)),
        out_shape=jax.ShapeDtypeStruct((n_heads, rows, cols), F32),
        compiler_params=_params(1),
        name=name,
    )(lanes, jnp.asarray(idx))


def _input_kernel(x_ref, g_ref, win_ref, gq_ref, wuq_ref, gkv_ref, wuk_ref,
                  wuv_ref, wkr_ref, cos_ref, sin_ref,
                  qkva_ref, kb_ref, qbt_ref, vbt_ref, qct_ref, kc_ref, vct_ref):
    h = _rms(x_ref[...], g_ref[...]).astype(BF16)
    cos = cos_ref[...]
    sin = sin_ref[...]

    def project(c):
        t = _dot(h, win_ref[:, c:c + PROJ_CHUNK])
        if c < QB_COL0:
            qkva_ref[:, c:c + PROJ_CHUNK] = (t * A_SCORE_SCALE if c == 0 else t).astype(BF16)
        elif c == QB_COL0:
            qbt_ref[...] = (t * B_SCORE_SCALE).T.astype(BF16)
        elif c == KB_COL0:
            kb_ref[...] = t.astype(BF16)
        else:
            vbt_ref[...] = t.T.astype(BF16)

    chunks = iter(range(0, AB_COLS, PROJ_CHUNK))
    cq = _dot(h, win_ref[:, CQ_COL0:CKV_COL0])
    ckv = _dot(h, win_ref[:, CKV_COL0:KR_COL0])
    kr = _dot(h, wkr_ref[...])
    project(next(chunks))
    cqn = _rms(cq, gq_ref[...]).astype(BF16)
    project(next(chunks))
    ckvn = _rms(ckv, gkv_ref[...]).astype(BF16)
    kr_rot = kr * cos + _swap_halves(kr) * sin
    for j in range(C_PAIRS):
        cols = slice(j * C_PAIR_LANES, (j + 1) * C_PAIR_LANES)
        t = _dot(cqn, wuq_ref[:, cols])
        qct_ref[cols, :] = ((t * cos + _swap_halves(t) * sin) * C_SCORE_SCALE).T.astype(BF16)
        project(next(chunks))
    for j in range(C_PAIRS):
        cols = slice(j * C_PAIR_LANES, (j + 1) * C_PAIR_LANES)
        kc_ref[:, cols] = (_dot(ckvn, wuk_ref[:, cols]) + kr_rot).astype(BF16)
    vct_ref[...] = _dot(ckvn, wuv_ref[...]).T.astype(BF16)
    assert next(chunks, None) is None


def _input_stage(x2, layer, g, w_in, gq, wuq, gkv, wuk, wuv, wkr, cos_t, sin_t, seq):
    n = x2.shape[0]
    tm = ROW_TILE
    pos_blocks = seq // tm
    row = lambda i: (i, 0)
    col = lambda i: (0, i)
    return pl.pallas_call(
        _input_kernel,
        grid=(n // tm,),
        in_specs=[
            pl.BlockSpec((tm, D_MODEL), row),
            _layer_spec((1, D_MODEL), layer),
            _const_spec((D_MODEL, KR_COL0)),
            _layer_spec((1, C_Q_RANK), layer),
            _const_spec((C_Q_RANK, C_COLS)),
            _layer_spec((1, C_KV_RANK), layer),
            _const_spec((C_KV_RANK, C_COLS)),
            _const_spec((C_KV_RANK, C_WIDTH)),
            _const_spec((D_MODEL, C_PAIR_LANES)),
            pl.BlockSpec((tm, C_PAIR_LANES), lambda i: (i % pos_blocks, 0)),
            pl.BlockSpec((tm, C_PAIR_LANES), lambda i: (i % pos_blocks, 0)),
        ],
        out_specs=[
            pl.BlockSpec((tm, QB_COL0), row),
            pl.BlockSpec((tm, VB_COL0 - KB_COL0), row),
            pl.BlockSpec((B_WIDTH, tm), col),
            pl.BlockSpec((B_WIDTH, tm), col),
            pl.BlockSpec((C_COLS, tm), col),
            pl.BlockSpec((tm, C_COLS), row),
            pl.BlockSpec((C_WIDTH, tm), col),
        ],
        out_shape=[
            jax.ShapeDtypeStruct((n, QB_COL0), BF16),
            jax.ShapeDtypeStruct((n, VB_COL0 - KB_COL0), BF16),
            jax.ShapeDtypeStruct((B_WIDTH, n), BF16),
            jax.ShapeDtypeStruct((B_WIDTH, n), BF16),
            jax.ShapeDtypeStruct((C_COLS, n), BF16),
            jax.ShapeDtypeStruct((n, C_COLS), BF16),
            jax.ShapeDtypeStruct((C_WIDTH, n), BF16),
        ],
        compiler_params=_params(1),
        name="input_stage",
    )(x2, g, w_in, gq, wuq, gkv, wuk, wuv, wkr, cos_t, sin_t)


def _dot_tn(a, b):
    return lax.dot_general(a, b, (((0,), (0,)), ((), ())), preferred_element_type=F32)


def _dilated_kernel(q_ref, k_ref, v_ref, bias_ref, y_ref, qf_ref, kf_ref, vf_ref, m_ref, w_ref, a_ref):
    seq = q_ref.shape[0]
    chunk = ATT_TILE

    def widen(c, carry):
        rows = pl.ds(pl.multiple_of(c * chunk, chunk), chunk)
        qf_ref[rows, :] = q_ref[rows, :].astype(F32)
        kf_ref[rows, :] = k_ref[rows, :].astype(F32)
        vf_ref[rows, :] = v_ref[rows, :].astype(F32)
        return carry

    lax.fori_loop(0, seq // chunk, widen, 0)

    tile = (A_BLOCK, A_BLOCK)
    low_lane = lax.broadcasted_iota(jnp.int32, tile, 1) < HEAD_DIM
    low_row = lax.broadcasted_iota(jnp.int32, tile, 0) < HEAD_DIM
    last = len(A_PATTERNS) - 1
    for g, (_, dil) in enumerate(A_PATTERNS):
        nb = seq // dil // A_BLOCK
        span = dil * A_BLOCK

        def attend(blk, g=g, dil=dil, nb=nb, span=span):
            r = blk // nb
            n = blk % nb
            base = r + n * span
            prev_base = jnp.maximum(base - span, r)
            rows = pl.ds(base, A_BLOCK, stride=dil)
            prev_rows = pl.ds(prev_base, A_BLOCK, stride=dil)
            q = qf_ref[rows, :]
            k2 = jnp.concatenate([kf_ref[prev_rows, :], kf_ref[rows, :]], axis=0).astype(BF16)
            v2 = jnp.concatenate([vf_ref[prev_rows, :], vf_ref[rows, :]], axis=0).astype(BF16)
            first_block = jnp.where(n == 0, 1, 0)
            outs, lses = [], []
            for hh in range(2):
                qm = jnp.where(low_lane if hh == 0 else ~low_lane, q, 0.0).astype(BF16)
                s = _dot_nt(k2, qm) + bias_ref[hh, g, first_block]
                m = jnp.max(s, axis=0, keepdims=True)
                p = jnp.exp2(s - m)
                l = jnp.sum(p, axis=0, keepdims=True)
                outs.append(_dot_tn(v2, p.astype(BF16)) * (1.0 / l))
                lses.append(jnp.broadcast_to(m + jnp.log2(l), tile))
            o_blk = jnp.where(low_row, outs[0], outs[1]).T
            lse_blk = jnp.where(low_row, lses[0], lses[1]).T
            return rows, o_blk, lse_blk

        def merge(rows, o_blk, lse_blk, g=g):
            if g == 0:
                m_ref[rows, :] = lse_blk
                a_ref[rows, :] = o_blk
                return
            m_old = m_ref[rows, :]
            m_new = jnp.maximum(m_old, lse_blk)
            c_old = jnp.exp2(m_old - m_new)
            c_blk = jnp.exp2(lse_blk - m_new)
            w_new = c_blk + (c_old if g == 1 else c_old * w_ref[rows, :])
            a_new = c_old * a_ref[rows, :] + c_blk * o_blk
            if g < last:
                m_ref[rows, :] = m_new
                w_ref[rows, :] = w_new
                a_ref[rows, :] = a_new
            else:
                a_ref[rows, :] = a_new / w_new

        def step(it, carry, attend=attend, merge=merge):
            done = [attend(it * A_BLOCKS_PER_STEP + u) for u in range(A_BLOCKS_PER_STEP)]
            for args in done:
                merge(*args)
            return carry

        lax.fori_loop(0, dil * nb // A_BLOCKS_PER_STEP, step, 0)

    def narrow(c, carry):
        rows = pl.ds(pl.multiple_of(c * chunk, chunk), chunk)
        y_ref[rows, :] = a_ref[rows, :].astype(BF16)
        return carry

    lax.fori_loop(0, seq // chunk, narrow, 0)


def _dilated_attention(qkv, bias_tiles, batch, seq):
    pairs = A_HEADS // 2
    col = lambda off: (lambda b, p: (b, off * pairs + p))
    pair = 2 * HEAD_DIM
    assert pair == LANES
    state = pltpu.VMEM((seq, pair), F32)
    return pl.pallas_call(
        _dilated_kernel,
        grid=(batch, pairs),
        in_specs=[
            pl.BlockSpec((seq, pair), col(0)),
            pl.BlockSpec((seq, pair), col(1)),
            pl.BlockSpec((seq, pair), col(2)),
            pl.BlockSpec((2, len(A_PATTERNS), 2, 2 * A_BLOCK, A_BLOCK), lambda b, p: (p, 0, 0, 0, 0)),
        ],
        out_specs=pl.BlockSpec((seq, pair), lambda b, p: (b, p)),
        out_shape=jax.ShapeDtypeStruct((batch * seq, A_WIDTH), BF16),
        scratch_shapes=[state] * 6,
        compiler_params=_params(2),
        name="dilated_attention",
    )(qkv, qkv, qkv, bias_tiles)


def _flash_init(m_ref, acc_ref):
    m_ref[...] = jnp.full(m_ref.shape, NEG, F32)
    acc_ref[...] = jnp.zeros(acc_ref.shape, F32)


def _with_ones(v_t):
    return jnp.concatenate([v_t, jnp.ones((ONES_ROWS, v_t.shape[1]), v_t.dtype)], axis=0)


def _key_rows(part):
    return slice(part * KEY_PART, (part + 1) * KEY_PART)


def _store_scores(s, slot, idx, s_ref, cmax_ref):
    s_ref[slot, idx] = s
    cmax_ref[slot, idx] = jnp.max(s, axis=0, keepdims=True)


def _masked_part(slot, idx, part, masked, s_ref):
    s = s_ref[slot, idx, _key_rows(part), :]
    if masked:
        key = lax.broadcasted_iota(jnp.int32, s.shape, 0) + part * KEY_PART
        s = jnp.where(key <= lax.broadcasted_iota(jnp.int32, s.shape, 1), s, NEG)
    return s


def _flash_begin(slot, idx, masked, m_ref, s_ref, cmax_ref):
    if masked:
        col_max = functools.reduce(jnp.maximum, [
            jnp.max(_masked_part(slot, idx, part, True, s_ref), axis=0, keepdims=True)
            for part in range(ATT_TILE // KEY_PART)])
    else:
        col_max = cmax_ref[slot, idx]
    m_prev = m_ref[idx]
    m_new = jnp.maximum(m_prev, col_max)
    m_ref[idx] = m_new
    return m_new, jnp.exp2(m_prev - m_new)


def _flash_part(slot, idx, part, masked, m_new, v_ext, s_ref):
    p = jnp.exp2(_masked_part(slot, idx, part, masked, s_ref) - m_new)
    return _dot(v_ext[:, _key_rows(part)], p.astype(BF16))


def _flash_result(idx, dv, acc_ref):
    acc = acc_ref[idx]
    return acc[:dv] * (1.0 / acc[dv:dv + 1])


def _flash_sweep(n_tiles, scores, absorb, finish, m_ref, acc_ref, s_ref, cmax_ref):
    parts = ATT_TILE // KEY_PART
    items = [(i, c) for i in range(n_tiles) for c in range(i + 1)]

    def item(nxt, cur, diag, slot):
        for idx in range(2):
            if nxt is not None:
                scores(*nxt, 1 - slot, idx)
            m_new, alpha = _flash_begin(slot, idx, diag, m_ref, s_ref, cmax_ref)
            pv = None
            for part in range(parts):
                piece = absorb(cur, slot, idx, part, diag, m_new)
                pv = piece if pv is None else pv + piece
            acc_ref[idx] = alpha * acc_ref[idx] + pv

    for idx in range(2):
        scores(0, 0, 0, idx)
    for e, (i, c) in enumerate(items):
        item(items[e + 1] if e + 1 < len(items) else None, c, c == i, e % 2)
        if c == i:
            finish(i)


def _row_in_ranges(shape, ranges):
    row = lax.broadcasted_iota(jnp.int32, shape, 0)
    hit = None
    for lo, hi in ranges:
        r = (row >= lo) & (row < hi)
        hit = r if hit is None else hit | r
    return hit


def _diff_bias(bias_ref, d):
    sub = ATT_TILE // BIAS_TILE
    rows = [jnp.concatenate([bias_ref[0, jnp.clip(sub * d + qa - ka, 0, B_BIAS_TILES - 1)]
                             for qa in range(sub)], axis=1) for ka in range(sub)]
    return jnp.concatenate(rows, axis=0)


def _tile_cols(i):
    return pl.ds(pl.multiple_of(i * ATT_TILE, ATT_TILE), ATT_TILE)


def _diff_kernel(lam_init, qt_ref, k_ref, vt_ref, bias_ref, lam_ref, g_ref, y_ref,
                 qm_ref, m_ref, acc_ref, s_ref, cmax_ref):
    t = ATT_TILE
    n_tiles = qt_ref.shape[1] // t

    def split_queries(i, carry):
        qf = qt_ref[:, _tile_cols(i)].astype(F32)
        first = lax.broadcasted_iota(jnp.int32, qf.shape, 0) < B_QK_DIM
        qm_ref[0, :, _tile_cols(i)] = jnp.where(first, qf, 0.0).astype(BF16)
        qm_ref[1, :, _tile_cols(i)] = jnp.where(first, 0.0, qf).astype(BF16)
        return carry

    lax.fori_loop(0, n_tiles, split_queries, 0)
    lv = lam_ref[...]
    lam = (jnp.exp(jnp.sum(lv[0:1] * lv[1:2], axis=-1, keepdims=True))
           - jnp.exp(jnp.sum(lv[2:3] * lv[3:4], axis=-1, keepdims=True)) + lam_init)
    _flash_init(m_ref, acc_ref)

    def scores(i, c, slot, idx):
        k = k_ref[_tile_cols(c), :]
        s = _dot(k, qm_ref[idx, :, _tile_cols(i)]) + _diff_bias(bias_ref, i - c)
        _store_scores(s, slot, idx, s_ref, cmax_ref)

    def absorb(c, slot, idx, part, masked, m_new):
        v_ext = _with_ones(vt_ref[:, _tile_cols(c)])
        return _flash_part(slot, idx, part, masked, m_new, v_ext, s_ref)

    def finish(i):
        o = _flash_result(0, B_V_DIM, acc_ref) - lam * _flash_result(1, B_V_DIM, acc_ref)
        y = o * lax.rsqrt(jnp.mean(o * o, axis=0, keepdims=True) + EPS) * g_ref[...] * (1.0 - lam_init)
        y_ref[_tile_cols(i), :] = y.T.astype(BF16)
        _flash_init(m_ref, acc_ref)

    _flash_sweep(n_tiles, scores, absorb, finish, m_ref, acc_ref, s_ref, cmax_ref)


def _diff_attention(qb_t, kb, vb_t, bias_tiles, lam_vecs, g_col, batch, seq, lam_init):
    t = ATT_TILE
    qk = 2 * B_QK_DIM
    return pl.pallas_call(
        functools.partial(_diff_kernel, lam_init),
        grid=(batch, B_HEADS),
        in_specs=[
            pl.BlockSpec((qk, seq), lambda b, h: (h, b)),
            pl.BlockSpec((seq, qk), lambda b, h: (b, h)),
            pl.BlockSpec((B_V_DIM, seq), lambda b, h: (h, b)),
            pl.BlockSpec((1, B_BIAS_TILES, BIAS_TILE, BIAS_TILE), lambda b, h: (h, 0, 0, 0)),
            _const_spec((4, B_QK_DIM)),
            _const_spec((B_V_DIM, 1)),
        ],
        out_specs=pl.BlockSpec((seq, B_V_DIM), lambda b, h: (b, h)),
        out_shape=jax.ShapeDtypeStruct((batch * seq, B_WIDTH), BF16),
        scratch_shapes=[
            pltpu.VMEM((2, qk, seq), BF16),
            pltpu.VMEM((2, 1, t), F32),
            pltpu.VMEM((2, B_V_DIM + ONES_ROWS, t), F32),
            pltpu.VMEM((2, 2, t, t), F32),
            pltpu.VMEM((2, 2, 1, t), F32),
        ],
        compiler_params=_params(2),
        name="diff_attention",
    )(qb_t, kb, vb_t, bias_tiles, lam_vecs, g_col)


def _mla_kernel(qt_ref, k_ref, vt_ref, y_ref, qm_ref, m_ref, acc_ref, s_ref, cmax_ref):
    t = ATT_TILE
    n_tiles = qt_ref.shape[1] // t

    def split_heads(i, carry):
        qf = qt_ref[:, _tile_cols(i)].astype(F32)
        qm_ref[0, :, _tile_cols(i)] = jnp.where(_row_in_ranges(qf.shape, C_HEAD_A_RANGES), qf, 0.0).astype(BF16)
        qm_ref[1, :, _tile_cols(i)] = jnp.where(_row_in_ranges(qf.shape, C_HEAD_B_RANGES), qf, 0.0).astype(BF16)
        return carry

    lax.fori_loop(0, n_tiles, split_heads, 0)
    _flash_init(m_ref, acc_ref)

    def scores(i, c, slot, idx):
        k = k_ref[_tile_cols(c), :]
        _store_scores(_dot(k, qm_ref[idx, :, _tile_cols(i)]), slot, idx, s_ref, cmax_ref)

    def absorb(c, slot, idx, part, masked, m_new):
        v_t = vt_ref[idx * C_V:(idx + 1) * C_V, _tile_cols(c)]
        return _flash_part(slot, idx, part, masked, m_new, _with_ones(v_t), s_ref)

    def finish(i):
        o = jnp.concatenate([_flash_result(0, C_V, acc_ref), _flash_result(1, C_V, acc_ref)], axis=0)
        y_ref[_tile_cols(i), :] = o.T.astype(BF16)
        _flash_init(m_ref, acc_ref)

    _flash_sweep(n_tiles, scores, absorb, finish, m_ref, acc_ref, s_ref, cmax_ref)


def _mla_attention(qc_t, kc, vc_t, batch, seq):
    t = ATT_TILE
    return pl.pallas_call(
        _mla_kernel,
        grid=(batch, C_PAIRS),
        in_specs=[
            pl.BlockSpec((C_PAIR_LANES, seq), lambda b, j: (j, b)),
            pl.BlockSpec((seq, C_PAIR_LANES), lambda b, j: (b, j)),
            pl.BlockSpec((2 * C_V, seq), lambda b, j: (j, b)),
        ],
        out_specs=pl.BlockSpec((seq, 2 * C_V), lambda b, j: (b, j)),
        out_shape=jax.ShapeDtypeStruct((batch * seq, C_WIDTH), BF16),
        scratch_shapes=[
            pltpu.VMEM((2, C_PAIR_LANES, seq), BF16),
            pltpu.VMEM((2, 1, t), F32),
            pltpu.VMEM((2, C_V + ONES_ROWS, t), F32),
            pltpu.VMEM((2, 2, t, t), F32),
            pltpu.VMEM((2, 2, 1, t), F32),
        ],
        compiler_params=_params(2),
        name="mla_attention",
    )(qc_t, kc, vc_t)


def _sigmoid(z):
    return 1.0 / (1.0 + jnp.exp(-z))


def _merge_kernel(x_ref, g_ref, ya_ref, yb_ref, yc_ref, wg_ref, bg_ref, wa_ref, wb_ref, wc_ref,
                  wo_ref, out_ref):
    x = x_ref[...]
    h = _rms(x, g_ref[...]).astype(BF16)
    merged = jnp.zeros(x.shape, F32)
    for k, (y_ref, w_ref) in enumerate(((ya_ref, wa_ref), (yb_ref, wb_ref), (yc_ref, wc_ref))):
        cols = slice(k * D_MODEL, (k + 1) * D_MODEL)
        gate = _sigmoid(_dot(h, wg_ref[:, cols]) + bg_ref[:, cols])
        merged = merged + gate * _dot(y_ref[...], w_ref[...])
    out_ref[...] = x + _dot(merged.astype(BF16), wo_ref[...])


def _merge_stage(x2, layer, g, ya, yb, yc, wg, bg, wa, wb, wc, wo):
    n = x2.shape[0]
    tm = ROW_TILE
    row = lambda i: (i, 0)
    return pl.pallas_call(
        _merge_kernel,
        grid=(n // tm,),
        in_specs=[
            pl.BlockSpec((tm, D_MODEL), row),
            _layer_spec((1, D_MODEL), layer),
            pl.BlockSpec((tm, A_WIDTH), row),
            pl.BlockSpec((tm, B_WIDTH), row),
            pl.BlockSpec((tm, C_WIDTH), row),
            _layer_spec((D_MODEL, N_BRANCH * D_MODEL), layer),
            _layer_spec((1, N_BRANCH * D_MODEL), layer),
            _layer_spec((A_WIDTH, D_MODEL), layer),
            _layer_spec((B_WIDTH, D_MODEL), layer),
            _layer_spec((C_WIDTH, D_MODEL), layer),
            _layer_spec((D_MODEL, D_MODEL), layer),
        ],
        out_specs=pl.BlockSpec((tm, D_MODEL), row),
        out_shape=jax.ShapeDtypeStruct((n, D_MODEL), F32),
        compiler_params=_params(1),
        name="gated_merge",
    )(x2, g, ya, yb, yc, wg, bg, wa, wb, wc, wo)


def _ffn_kernel(final_norm, x_ref, g_ref, wg_ref, wu_ref, wd_ref, gf_ref, out_ref):
    x = x_ref[...]
    h = _rms(x, g_ref[...]).astype(BF16)
    acc = jnp.zeros(x.shape, F32)
    for c in range(0, FFN_HIDDEN, FFN_CHUNK):
        cols = slice(c, c + FFN_CHUNK)
        gate = _dot(h, wg_ref[:, cols])
        act = gate * _sigmoid(gate) * _dot(h, wu_ref[:, cols])
        acc = acc + _dot(act.astype(BF16), wd_ref[cols, :])
    y = x + acc
    if final_norm:
        y = _rms(y, gf_ref[...])
    out_ref[...] = y


def _ffn_stage(x2, layer, g, wg, wu, wd, gf, final_norm):
    n = x2.shape[0]
    tm = ROW_TILE
    row = lambda i: (i, 0)
    return pl.pallas_call(
        functools.partial(_ffn_kernel, final_norm),
        grid=(n // tm,),
        in_specs=[
            pl.BlockSpec((tm, D_MODEL), row),
            _layer_spec((1, D_MODEL), layer),
            _layer_spec((D_MODEL, FFN_HIDDEN), layer),
            _layer_spec((D_MODEL, FFN_HIDDEN), layer),
            _layer_spec((FFN_HIDDEN, D_MODEL), layer),
            _const_spec((1, D_MODEL)),
        ],
        out_specs=pl.BlockSpec((tm, D_MODEL), row),
        out_shape=jax.ShapeDtypeStruct((n, D_MODEL), F32),
        compiler_params=_params(1),
        name="swiglu_final" if final_norm else "swiglu",
    )(x2, g, wg, wu, wd, gf)


def _gather_cols(w, idx):
    idx = np.asarray(idx)
    zero = idx == w.shape[1]
    cuts = [0] + [j for j in range(1, len(idx))
                  if zero[j] != zero[j - 1] or (not zero[j] and idx[j] != idx[j - 1] + 1)]
    pieces = []
    for lo, hi in zip(cuts, cuts[1:] + [len(idx)]):
        if idx[lo] == w.shape[1]:
            pieces.append(jnp.zeros((w.shape[0], hi - lo), w.dtype))
        else:
            pieces.append(w[:, int(idx[lo]):int(idx[lo]) + hi - lo])
    return jnp.concatenate(pieces, axis=1)


def kernel(x, rel_bias_table, ln_mix_g, w_in, lambda_q1, lambda_k1, lambda_q2, lambda_k2, diff_subln_g, mla_q_norm_g, w_uq, mla_kv_norm_g, w_ukv, w_gate, b_gate, w_br_a, w_br_b, w_br_c, w_o, ln_ffn_g, w_ffn_gate, w_ffn_up, w_ffn_down, final_norm_g):
    batch, seq, _ = x.shape
    depth = w_in.shape[0]
    assert all(seq % (A_BLOCK * dil) == 0 for _, dil in A_PATTERNS)
    assert (seq // A_BLOCK) % A_BLOCKS_PER_STEP == 0
    assert seq % ATT_TILE == 0 and seq % ROW_TILE == 0
    lay = _C_LAYOUT
    cos_t, sin_t = _rope_tables(seq)

    a_tiles = _bias_expand(rel_bias_table, _a_bucket_index(), A_HEADS, 0, 2 * A_BLOCK, "bias_expand_a",
                           gain=LOG2E)
    a_tiles = a_tiles.reshape(A_HEADS, len(A_PATTERNS), 2, 2 * A_BLOCK, A_BLOCK)
    b_tiles = _bias_expand(rel_bias_table, _b_bucket_index(), B_HEADS, A_HEADS, BIAS_TILE,
                           "bias_expand_b", gain=LOG2E).reshape(B_HEADS, B_BIAS_TILES, BIAS_TILE, BIAS_TILE)

    rows = lambda v: v[:, None, :]
    w_gate16, w_o16 = _to_bf16(w_gate), _to_bf16(w_o)
    w_br16 = [_to_bf16(w) for w in (w_br_a, w_br_b, w_br_c)]
    w_ffn16 = [_to_bf16(w) for w in (w_ffn_gate, w_ffn_up, w_ffn_down)]
    g_mix, g_q, g_kv, g_ffn, b_gate3 = (rows(v) for v in (ln_mix_g, mla_q_norm_g, mla_kv_norm_g, ln_ffn_g, b_gate))

    x2 = x.reshape(batch * seq, D_MODEL)
    for l in range(depth):
        qkv_a, kb, qb_t, vb_t, qc_t, kc, vc_t = _input_stage(
            x2, l, g_mix, w_in[l][:, :KR_COL0].astype(BF16), g_q,
            _gather_cols(w_uq[l], lay["q"]).astype(BF16), g_kv,
            _gather_cols(w_ukv[l], lay["k"]).astype(BF16),
            _gather_cols(w_ukv[l], lay["v"]).astype(BF16),
            _gather_cols(w_in[l][:, KR_COL0:], lay["kr"]).astype(BF16),
            cos_t, sin_t, seq)

        ya = _dilated_attention(qkv_a, a_tiles, batch, seq)

        lam_init = 0.8 - 0.6 * math.exp(-0.3 * l)
        lam_vecs = jnp.stack([lambda_q1[l], lambda_k1[l], lambda_q2[l], lambda_k2[l]]).astype(F32)
        yb = _diff_attention(qb_t, kb, vb_t, b_tiles, lam_vecs, diff_subln_g[l][:, None],
                             batch, seq, lam_init)
        yc = _mla_attention(qc_t, kc, vc_t, batch, seq)

        x2 = _merge_stage(x2, l, g_mix, ya, yb, yc, w_gate16, b_gate3, *w_br16, w_o16)
        x2 = _ffn_stage(x2, l, g_ffn, *w_ffn16, final_norm_g[None, :], final_norm=(l == depth - 1))
    return x2.reshape(batch, seq, D_MODEL)
```
